```python
import jax, jax.numpy as jnp
from jax import lax
import numpy as np

D_MODEL = 1024
BATCH = 8
SEQ = 2048
DEPTH = 4
DEC_BATCH = 128
DEC_SEQ = 8
PAST_LEN = 16384
PAGE_SIZE = 128

N_MIXERS = 2
N_POOL_LAYERS = (DEPTH + 1) // 2
N_GLA_LAYERS = DEPTH // 2
EXPAND = 2
D_INNER = EXPAND * D_MODEL
POOL_WINDOWS = (2, 4, 8, 16)
N_POOL_GROUPS = len(POOL_WINDOWS)
POOL_GROUP = D_INNER // N_POOL_GROUPS
POOL_BUF = max(POOL_WINDOWS) - 1
GLA_HEADS = 4
GLA_DK = D_MODEL // 2
GLA_DV = D_INNER
GLA_HK = GLA_DK // GLA_HEADS
GLA_HV = GLA_DV // GLA_HEADS
GLA_RANK = 16
GLA_GATE_NORM = 16.0
GLA_CHUNK = 32
GLA_IN = 2 * GLA_DK + 2 * GLA_DV + GLA_RANK
EPS = 1e-6
F32 = jnp.float32

kernel_name = 'hybrid_pool_gla_adaln_step'


def rmsnorm(x, g):
    xf = x.astype(F32)
    y = xf * lax.rsqrt(jnp.mean(xf * xf, axis=-1, keepdims=True) + EPS)
    return (y * g.astype(F32)).astype(x.dtype)


def pool_mix(u, buf, pos0, w_grp, scale):
    b, L, _ = u.shape
    ext = jnp.concatenate([buf.astype(F32), u.astype(F32)], axis=1)
    cs = jnp.concatenate([jnp.zeros((b, 1, D_INNER), F32), jnp.cumsum(ext, axis=1)], axis=1)
    cs = cs.reshape(b, POOL_BUF + L + 1, N_POOL_GROUPS, POOL_GROUP)
    pos = pos0 + jnp.arange(L)
    hi = POOL_BUF + 1
    means = []
    for gi, w in enumerate(POOL_WINDOWS):
        s = cs[:, hi:hi + L, gi] - cs[:, hi - w:hi - w + L, gi]
        cnt = jnp.minimum(pos + 1, w).astype(F32)
        means.append(s / cnt[None, :, None])
    mean = jnp.stack(means, axis=2)
    diff = mean - ext[:, POOL_BUF:].reshape(b, L, N_POOL_GROUPS, POOL_GROUP)
    mixed = jnp.einsum('blgc,gcd->blgd', diff, w_grp.astype(F32)).reshape(b, L, D_INNER)
    return mixed * scale.astype(F32), ext[:, -POOL_BUF:]


def pool_branch(h, buf, pos0, w_in, w_grp, scale, w_out):
    uz = h @ w_in
    u, z = jnp.split(uz, 2, axis=-1)
    mixed, new_buf = pool_mix(u, buf, pos0, w_grp, scale)
    y = (mixed * jax.nn.silu(z.astype(F32))).astype(h.dtype) @ w_out
    return y, new_buf.astype(buf.dtype)


def gla_recurrence(q, k, v, lg, s0):
    b, L, H, _ = q.shape
    C = GLA_CHUNK
    n = -(-L // C)
    pad = n * C - L

    def chunks(a):
        a = jnp.pad(a.astype(F32), ((0, 0), (0, pad), (0, 0), (0, 0)))
        return a.reshape(b, n, C, H, a.shape[-1]).transpose(1, 0, 3, 2, 4)

    q, k, v, lg = chunks(q), chunks(k), chunks(v), chunks(lg)
    bcum = jnp.cumsum(lg, axis=3)
    blast = bcum[:, :, :, -1:]
    q_dec = q * jnp.exp(bcum)
    k_intra = k * jnp.exp(-bcum)
    k_state = k * jnp.exp(blast - bcum)
    causal = jnp.tril(jnp.ones((C, C), bool))
    att = jnp.where(causal, jnp.einsum('nbhid,nbhjd->nbhij', q_dec, k_intra), 0.0)
    o_intra = jnp.einsum('nbhij,nbhje->nbhie', att, v)

    def step(s, inp):
        qc, kc, vc, dc = inp
        o_inter = jnp.einsum('bhid,bhde->bhie', qc, s)
        s = dc[..., None] * s + jnp.einsum('bhjd,bhje->bhde', kc, vc)
        return s, o_inter

    s_fin, o_inter = lax.scan(step, s0.astype(F32), (q_dec, k_state, v, jnp.exp(blast[:, :, :, 0])))
    o = (o_intra + o_inter).transpose(1, 0, 3, 2, 4).reshape(b, n * C, H, -1)[:, :L]
    return o, s_fin


def gla_branch(h, s0, w_in, w_gate_up, b_gate, norm_g, w_out):
    b, L, _ = h.shape
    proj = h @ w_in
    q, k, v, z, gl = jnp.split(proj, [GLA_DK, 2 * GLA_DK, 2 * GLA_DK + GLA_DV, 2 * GLA_DK + 2 * GLA_DV], axis=-1)
    lg = jax.nn.log_sigmoid((gl @ w_gate_up + b_gate).astype(F32)) / GLA_GATE_NORM
    q = q.astype(F32).reshape(b, L, GLA_HEADS, GLA_HK) * (GLA_HK ** -0.5)
    k = k.reshape(b, L, GLA_HEADS, GLA_HK)
    v = v.reshape(b, L, GLA_HEADS, GLA_HV)
    lg = lg.reshape(b, L, GLA_HEADS, GLA_HK)
    o, s_new = gla_recurrence(q, k, v, lg, s0)
    o = o * lax.rsqrt(jnp.mean(o * o, axis=-1, keepdims=True) + EPS)
    o = o.reshape(b, L, GLA_DV) * norm_g.astype(F32)
    y = (o * jax.nn.silu(z.astype(F32))).astype(h.dtype) @ w_out
    return y, s_new.astype(s0.dtype)


def run_trunk(x, c, pool_bufs, gla_states, pos0, ada_w, ada_b, norm_g, pool_w_in, pool_w_grp, pool_scale,
              pool_w_out, gla_w_in, gla_w_gate_up, gla_b_gate, gla_norm_g, gla_w_out, final_g):
    new_pool, new_gla = [], []
    cs = jax.nn.silu(c)
    for li in range(DEPTH):
        mod = cs @ ada_w[li] + ada_b[li]
        shift, scale, gate = jnp.split(mod, 3, axis=-1)
        h = rmsnorm(x, norm_g[li]) * (1.0 + scale[:, None]) + shift[:, None]
        j = li // N_MIXERS
        if li % N_MIXERS == 0:
            y, st = pool_branch(h, pool_bufs[j], pos0, pool_w_in[j], pool_w_grp[j], pool_scale[j], pool_w_out[j])
            new_pool.append(st)
        else:
            y, st = gla_branch(h, gla_states[j], gla_w_in[j], gla_w_gate_up[j], gla_b_gate[j], gla_norm_g[j], gla_w_out[j])
            new_gla.append(st)
        x = x + gate[:, None] * y
    return rmsnorm(x, final_g), jnp.stack(new_pool), jnp.stack(new_gla)


def setup_inputs(seed: int = 0) -> dict:
    key = jax.random.key(seed)
    ks = jax.random.split(key, 20)
    nrm = lambda k, s, sc: jax.random.normal(k, s, F32) * sc
    return {
        'x_prompt': nrm(ks[0], (BATCH, SEQ, D_MODEL), 1.0),
        'x_sample': nrm(ks[1], (DEC_BATCH, DEC_SEQ, D_MODEL), 1.0),
        'c_prompt': nrm(ks[2], (BATCH, D_MODEL), 1.0),
        'c_sample': nrm(ks[3], (DEC_BATCH, D_MODEL), 1.0),
        'state_pool': nrm(ks[4], (N_POOL_LAYERS, DEC_BATCH, POOL_BUF, D_INNER), 1.0),
        'state_gla': nrm(ks[5], (N_GLA_LAYERS, DEC_BATCH, GLA_HEADS, GLA_HK, GLA_HV), 1.0),
        'ada_w': nrm(ks[6], (DEPTH, D_MODEL, 3 * D_MODEL), 0.5 * D_MODEL ** -0.5),
        'ada_b': nrm(ks[7], (DEPTH, 3 * D_MODEL), 0.02),
        'norm_g': 1.0 + nrm(ks[8], (DEPTH, D_MODEL), 0.02),
        'pool_w_in': nrm(ks[9], (N_POOL_LAYERS, D_MODEL, 2 * D_INNER), D_MODEL ** -0.5),
        'pool_w_grp': nrm(ks[10], (N_POOL_LAYERS, N_POOL_GROUPS, POOL_GROUP, POOL_GROUP), POOL_GROUP ** -0.5),
        'pool_scale': 1.0 + nrm(ks[11], (N_POOL_LAYERS, D_INNER), 0.02),
        'pool_w_out': nrm(ks[12], (N_POOL_LAYERS, D_INNER, D_MODEL), D_INNER ** -0.5),
        'gla_w_in': nrm(ks[13], (N_GLA_LAYERS, D_MODEL, GLA_IN), D_MODEL ** -0.5),
        'gla_w_gate_up': nrm(ks[14], (N_GLA_LAYERS, GLA_RANK, GLA_DK), GLA_RANK ** -0.5),
        'gla_b_gate': nrm(ks[15], (N_GLA_LAYERS, GLA_DK), 0.1),
        'gla_norm_g': 1.0 + nrm(ks[16], (N_GLA_LAYERS, GLA_DV), 0.02),
        'gla_w_out': nrm(ks[17], (N_GLA_LAYERS, GLA_DV, D_MODEL), GLA_DV ** -0.5),
        'final_g': 1.0 + nrm(ks[18], (D_MODEL,), 0.02),
    }


def reference(x_prompt, x_sample, c_prompt, c_sample, state_pool, state_gla, ada_w, ada_b, norm_g,
              pool_w_in, pool_w_grp, pool_scale, pool_w_out, gla_w_in, gla_w_gate_up, gla_b_gate,
              gla_norm_g, gla_w_out, final_g):
    weights = (ada_w, ada_b, norm_g, pool_w_in, pool_w_grp, pool_scale, pool_w_out,
               gla_w_in, gla_w_gate_up, gla_b_gate, gla_norm_g, gla_w_out, final_g)
    pool0 = jnp.zeros((N_POOL_LAYERS, BATCH, POOL_BUF, D_INNER), x_prompt.dtype)
    gla0 = jnp.zeros((N_GLA_LAYERS, BATCH, GLA_HEADS, GLA_HK, GLA_HV), x_prompt.dtype)
    y_prompt, pool_prompt, gla_prompt = run_trunk(x_prompt, c_prompt, pool0, gla0, 0, *weights)
    y_sample, pool_sample, gla_sample = run_trunk(x_sample, c_sample, state_pool, state_gla, PAST_LEN, *weights)
    return (y_prompt, y_sample, pool_prompt, gla_prompt, pool_sample, gla_sample)
```

```python
import functools

import jax
import jax.numpy as jnp
from jax import lax
from jax.experimental import pallas as pl
from jax.experimental.pallas import tpu as pltpu

D_MODEL = 1024
DEPTH = 4
PAST_LEN = 16384
D_INNER = 2048
POOL_WINDOWS = (2, 4, 8, 16)
POOL_GROUP = D_INNER // len(POOL_WINDOWS)
POOL_BUF = max(POOL_WINDOWS) - 1
POOL_HALO = POOL_BUF + 1
GLA_HEADS = 4
GLA_DK = 512
GLA_DV = D_INNER
GLA_HK = GLA_DK // GLA_HEADS
GLA_HV = GLA_DV // GLA_HEADS
GLA_RANK = 16
GLA_GATE_NORM = 16.0
GLA_CHUNK = 64
GLA_GL_PAD = 128
EPS = 1e-6
F32 = jnp.float32
BF16 = jnp.bfloat16

VMEM_LIMIT_BYTES = 62 * 1024 * 1024


def _silu(v):
    return v / (1.0 + jnp.exp(-v))


def _mod_norm(x, g, shift, scale):
    ms = jnp.mean(x * x, axis=-1, keepdims=True)
    y = x * lax.rsqrt(ms + EPS) * g
    return y * (1.0 + scale) + shift


def _const_spec(shape):
    nd = len(shape)
    return pl.BlockSpec(shape, lambda *_: (0,) * nd, pipeline_mode=pl.Buffered(1))


def _ada_kernel(c_ref, w_ref, b_ref, o_ref):
    cs = _silu(c_ref[...]).astype(BF16)
    o_ref[0, 0] = jnp.dot(cs, w_ref[0].astype(BF16), preferred_element_type=F32) + b_ref[0, 0]


def _ada_mods(c_all, ada_w, ada_b):
    nb = c_all.shape[0]
    b4 = ada_b.reshape(DEPTH, 3, 1, D_MODEL)
    return pl.pallas_call(
        _ada_kernel,
        grid=(DEPTH, 3),
        in_specs=[
            pl.BlockSpec((nb, D_MODEL), lambda li, j: (0, 0)),
            pl.BlockSpec((1, D_MODEL, D_MODEL), lambda li, j: (li, 0, j)),
            pl.BlockSpec((1, 1, 1, D_MODEL), lambda li, j: (li, j, 0, 0)),
        ],
        out_specs=pl.BlockSpec((1, 1, nb, D_MODEL), lambda li, j: (li, j, 0, 0)),
        out_shape=jax.ShapeDtypeStruct((DEPTH, 3, nb, D_MODEL), F32),
        name="ada_mods",
    )(c_all, ada_w, b4)


def _pool_kernel(*refs, tb, tl, n_l, pos0, fresh, has_prev):
    it = iter(refs)
    x_ref, sh_ref, sc_ref, gt_ref, g_ref = (next(it) for _ in range(5))
    win_ref, wgrp_ref, pscale_ref, wout_ref = (next(it) for _ in range(4))
    buf_ref = None if fresh else next(it)
    if has_prev:
        next(it)
    y_ref, nbuf_ref, ext_ref, gated_ref = (next(it) for _ in range(4))

    l = pl.program_id(1)
    m = tb * tl

    @pl.when(l == 0)
    def _():
        ext_ref[:, 0:POOL_HALO, :] = jnp.zeros((tb, POOL_HALO, D_INNER), F32)
        if not fresh:
            ext_ref[:, 1:POOL_HALO, :] = buf_ref[0]

    x = x_ref[...]
    h = _mod_norm(x, g_ref[...], sh_ref[...], sc_ref[...]).reshape(m, D_MODEL).astype(BF16)
    uz = jnp.dot(h, win_ref[...], preferred_element_type=F32)
    ext_ref[:, POOL_HALO:POOL_HALO + tl, :] = uz[:, :D_INNER].reshape(tb, tl, D_INNER)

    for gi, w in enumerate(POOL_WINDOWS):
        c0 = gi * POOL_GROUP
        u_g = ext_ref[:, POOL_HALO:POOL_HALO + tl, c0:c0 + POOL_GROUP]
        s = u_g
        for j in range(1, w):
            s = s + ext_ref[:, POOL_HALO - j:POOL_HALO - j + tl, c0:c0 + POOL_GROUP]
        if pos0 >= w - 1:
            mean = s * (1.0 / w)
        else:
            pos = pos0 + l * tl + lax.broadcasted_iota(jnp.int32, (tb, tl, POOL_GROUP), 1)
            mean = s / jnp.minimum(pos + 1, w).astype(F32)
        diff = (mean - u_g).reshape(m, POOL_GROUP).astype(BF16)
        mixed = jnp.dot(diff, wgrp_ref[gi], preferred_element_type=F32) * pscale_ref[:, c0:c0 + POOL_GROUP]
        z_g = uz[:, D_INNER + c0:D_INNER + c0 + POOL_GROUP]
        gated_ref[:, c0:c0 + POOL_GROUP] = (mixed * _silu(z_g)).astype(BF16)

    y = jnp.dot(gated_ref[...], wout_ref[...], preferred_element_type=F32)
    y_ref[...] = x + gt_ref[...] * y.reshape(tb, tl, D_MODEL)

    @pl.when(l == n_l - 1)
    def _():
        nbuf_ref[0] = ext_ref[:, tl + 1:tl + POOL_HALO, :]

    if n_l > 1:
        @pl.when(l < n_l - 1)
        def _():
            ext_ref[:, 0:POOL_HALO, :] = ext_ref[:, tl:tl + POOL_HALO, :]


def _pool_layer(x, shift, scale, gate, g, w_in, w_grp, pscale, w_out, state, prev_out, *, layer, n_layers,
                tb, tl, pos0):
    b, L, _ = x.shape
    n_l = L // tl
    fresh = state is None
    has_prev = prev_out is not None
    seq = lambda bi, li: (bi, li, 0)
    per_b = lambda bi, li: (bi, 0, 0)
    in_specs = [
        pl.BlockSpec((tb, tl, D_MODEL), seq),
        pl.BlockSpec((tb, 1, D_MODEL), per_b),
        pl.BlockSpec((tb, 1, D_MODEL), per_b),
        pl.BlockSpec((tb, 1, D_MODEL), per_b),
        _const_spec((1, D_MODEL)),
        _const_spec((D_MODEL, 2 * D_INNER)),
        _const_spec((len(POOL_WINDOWS), POOL_GROUP, POOL_GROUP)),
        _const_spec((1, D_INNER)),
        _const_spec((D_INNER, D_MODEL)),
    ]
    args = [x, shift, scale, gate, g, w_in, w_grp, pscale, w_out]
    if not fresh:
        in_specs.append(pl.BlockSpec((1, tb, POOL_BUF, D_INNER), lambda bi, li: (layer, bi, 0, 0)))
        args.append(state)
    aliases = {}
    if has_prev:
        in_specs.append(pl.BlockSpec(memory_space=pl.ANY))
        aliases = {len(args): 1}
        args.append(prev_out)
    kern = functools.partial(_pool_kernel, tb=tb, tl=tl, n_l=n_l, pos0=pos0, fresh=fresh, has_prev=has_prev)
    return pl.pallas_call(
        kern,
        grid=(b // tb, n_l),
        in_specs=in_specs,
        out_specs=[
            pl.BlockSpec((tb, tl, D_MODEL), seq),
            pl.BlockSpec((1, tb, POOL_BUF, D_INNER), lambda bi, li: (layer, bi, 0, 0)),
        ],
        out_shape=[
            jax.ShapeDtypeStruct(x.shape, F32),
            jax.ShapeDtypeStruct((n_layers, b, POOL_BUF, D_INNER), F32),
        ],
        scratch_shapes=[
            pltpu.VMEM((tb, POOL_HALO + tl, D_INNER), F32),
            pltpu.VMEM((tb * tl, D_INNER), BF16),
        ],
        input_output_aliases=aliases,
        compiler_params=pltpu.CompilerParams(
            dimension_semantics=("arbitrary", "arbitrary"), vmem_limit_bytes=VMEM_LIMIT_BYTES),
        name=f"pool_layer{layer}_{'fresh' if fresh else 'cont'}",
    )(*args)


def _gla_kernel(*refs, tb, tl, n_l, fresh, has_prev, final):
    it = iter(refs)
    x_ref, sh_ref, sc_ref, gt_ref, g_ref = (next(it) for _ in range(5))
    win_ref, wup_ref, bg_ref, ng_ref, wout_ref = (next(it) for _ in range(5))
    fg_ref = next(it) if final else None
    s0_ref = None if fresh else next(it)
    if has_prev:
        next(it)
    y_ref, s_ref, gated_ref = (next(it) for _ in range(3))

    l = pl.program_id(1)
    m = tb * tl
    seg = min(tl, GLA_CHUNK)
    n_seg = GLA_CHUNK // seg
    n_chunk = m // GLA_CHUNK

    @pl.when(l == 0)
    def _():
        if fresh:
            s_ref[...] = jnp.zeros(s_ref.shape, F32)
        else:
            s_ref[...] = s0_ref[...]

    x = x_ref[...]
    h = _mod_norm(x, g_ref[...], sh_ref[...], sc_ref[...]).reshape(m, D_MODEL).astype(BF16)
    proj = jnp.dot(h, win_ref[...], preferred_element_type=F32)
    o_v, o_z, o_gl = 2 * GLA_DK, 2 * GLA_DK + GLA_DV, 2 * GLA_DK + 2 * GLA_DV
    gl = proj[:, o_gl:o_gl + GLA_GL_PAD].astype(BF16)
    gate_pre = jnp.dot(gl, wup_ref[...], preferred_element_type=F32) + bg_ref[...]
    lg = (jnp.minimum(gate_pre, 0.0) - jnp.log(1.0 + jnp.exp(-jnp.abs(gate_pre)))) * (1.0 / GLA_GATE_NORM)

    ri = lax.broadcasted_iota(jnp.int32, (GLA_CHUNK, GLA_CHUNK), 0)
    ci = lax.broadcasted_iota(jnp.int32, (GLA_CHUNK, GLA_CHUNK), 1)
    sh = seg.bit_length() - 1
    same = (ri >> sh) == (ci >> sh)
    causal = same & (ci <= ri)
    mid = same & ((ci & (seg - 1)) <= (seg // 2 - 1))
    pat = jnp.concatenate([jnp.where(mk, 1.0, 0.0).astype(F32) for mk in (causal, mid, same)], axis=0)

    for c in range(n_chunk):
        r0 = c * GLA_CHUNK
        rows = slice(r0, r0 + GLA_CHUNK)
        sums = jnp.dot(pat, lg[rows], preferred_element_type=F32, precision=lax.Precision.HIGHEST)
        bcum, bref, btot = sums[:GLA_CHUNK], sums[GLA_CHUNK:2 * GLA_CHUNK], sums[2 * GLA_CHUNK:]
        q = proj[rows, 0:GLA_DK] * (GLA_HK ** -0.5)
        k = proj[rows, GLA_DK:2 * GLA_DK]
        q_intra = (q * jnp.exp(bcum - bref)).astype(BF16)
        k_intra = (k * jnp.exp(bref - bcum)).astype(BF16)
        q_dec = (q * jnp.exp(bcum)).astype(BF16)
        k_state = (k * jnp.exp(btot - bcum)).astype(BF16)
        decay = jnp.exp(btot)
        for hd in range(GLA_HEADS):
            kc = slice(hd * GLA_HK, (hd + 1) * GLA_HK)
            vc0 = o_v + hd * GLA_HV
            v_h = proj[rows, vc0:vc0 + GLA_HV].astype(BF16)
            att = lax.dot_general(q_intra[:, kc], k_intra[:, kc], (((1,), (1,)), ((), ())),
                                  preferred_element_type=F32)
            att = jnp.where(causal, att, 0.0).astype(BF16)
            o_h = jnp.dot(att, v_h, preferred_element_type=F32)
            o_inter = []
            for sg in range(n_seg):
                srows = slice(sg * seg, (sg + 1) * seg)
                si = (r0 + sg * seg) // tl
                s_old = s_ref[0, si, hd]
                o_inter.append(jnp.dot(q_dec[srows, kc], s_old.astype(BF16), preferred_element_type=F32))
                d_col = jnp.transpose(jnp.broadcast_to(decay[sg * seg:sg * seg + 1, kc], (GLA_HK, GLA_HK)))
                upd = lax.dot_general(k_state[srows, kc], v_h[srows], (((0,), (0,)), ((), ())),
                                      preferred_element_type=F32)
                s_ref[0, si, hd] = s_old * jnp.concatenate([d_col] * (GLA_HV // GLA_HK), axis=1) + upd
            o_h = o_h + (o_inter[0] if n_seg == 1 else jnp.concatenate(o_inter, axis=0))
            o_h = o_h * lax.rsqrt(jnp.mean(o_h * o_h, axis=-1, keepdims=True) + EPS)
            hc = slice(hd * GLA_HV, (hd + 1) * GLA_HV)
            z_h = proj[rows, o_z + hd * GLA_HV:o_z + (hd + 1) * GLA_HV]
            gated_ref[rows, hc] = (o_h * ng_ref[:, hc] * _silu(z_h)).astype(BF16)

    y = jnp.dot(gated_ref[...], wout_ref[...], preferred_element_type=F32)
    out = x + gt_ref[...] * y.reshape(tb, tl, D_MODEL)
    if final:
        out = out * lax.rsqrt(jnp.mean(out * out, axis=-1, keepdims=True) + EPS) * fg_ref[...]
    y_ref[...] = out


def _gla_layer(x, shift, scale, gate, g, w_in, w_up, b_gate, norm_g, w_out, final_g, state, prev_out, *,
               layer, n_layers, tb, tl):
    b, L, _ = x.shape
    n_l = L // tl
    fresh = state is None
    has_prev = prev_out is not None
    final = final_g is not None
    n_in = w_in.shape[1]
    seq = lambda bi, li: (bi, li, 0)
    per_b = lambda bi, li: (bi, 0, 0)
    st_block = (1, tb, GLA_HEADS, GLA_HK, GLA_HV)
    st_map = lambda bi, li: (layer, bi, 0, 0, 0)
    in_specs = [
        pl.BlockSpec((tb, tl, D_MODEL), seq),
        pl.BlockSpec((tb, 1, D_MODEL), per_b),
        pl.BlockSpec((tb, 1, D_MODEL), per_b),
        pl.BlockSpec((tb, 1, D_MODEL), per_b),
        _const_spec((1, D_MODEL)),
        _const_spec((D_MODEL, n_in)),
        _const_spec((GLA_GL_PAD, GLA_DK)),
        _const_spec((1, GLA_DK)),
        _const_spec((1, GLA_DV)),
        _const_spec((GLA_DV, D_MODEL)),
    ]
    args = [x, shift, scale, gate, g, w_in, w_up, b_gate, norm_g, w_out]
    if final:
        in_specs.append(_const_spec((1, D_MODEL)))
        args.append(final_g)
    if not fresh:
        in_specs.append(pl.BlockSpec(st_block, st_map))
        args.append(state)
    aliases = {}
    if has_prev:
        in_specs.append(pl.BlockSpec(memory_space=pl.ANY))
        aliases = {len(args): 1}
        args.append(prev_out)
    kern = functools.partial(_gla_kernel, tb=tb, tl=tl, n_l=n_l, fresh=fresh, has_prev=has_prev, final=final)
    return pl.pallas_call(
        kern,
        grid=(b // tb, n_l),
        in_specs=in_specs,
        out_specs=[pl.BlockSpec((tb, tl, D_MODEL), seq), pl.BlockSpec(st_block, st_map)],
        out_shape=[
            jax.ShapeDtypeStruct(x.shape, F32),
            jax.ShapeDtypeStruct((n_layers, b, GLA_HEADS, GLA_HK, GLA_HV), F32),
        ],
        scratch_shapes=[pltpu.VMEM((tb * tl, GLA_DV), BF16)],
        input_output_aliases=aliases,
        compiler_params=pltpu.CompilerParams(
            dimension_semantics=("arbitrary", "arbitrary"), vmem_limit_bytes=VMEM_LIMIT_BYTES),
        name=f"gla_layer{layer}_{'fresh' if fresh else 'cont'}",
    )(*args)


def _run_trunk(x, mods, pool_state, gla_state, pos0, w, *, tb_pool, tb_gla, tl):
    b = x.shape[0]
    n_pool, n_gla = (DEPTH + 1) // 2, DEPTH // 2
    new_pool = new_gla = None
    for li in range(DEPTH):
        shift, scale, gate = (mods[li, j].reshape(b, 1, D_MODEL) for j in range(3))
        j = li // 2
        if li % 2 == 0:
            x, new_pool = _pool_layer(
                x, shift, scale, gate, w["norm_g"][li], w["pool_w_in"][j], w["pool_w_grp"][j],
                w["pool_scale"][j], w["pool_w_out"][j], pool_state, new_pool,
                layer=j, n_layers=n_pool, tb=tb_pool, tl=tl, pos0=pos0)
        else:
            x, new_gla = _gla_layer(
                x, shift, scale, gate, w["norm_g"][li], w["gla_w_in"][j], w["gla_w_up"][j], w["gla_b_gate"][j],
                w["gla_norm_g"][j], w["gla_w_out"][j], w["final_g"] if li == DEPTH - 1 else None,
                gla_state, new_gla, layer=j, n_layers=n_gla, tb=tb_gla, tl=tl)
    return x, new_pool, new_gla


def kernel(x_prompt, x_sample, c_prompt, c_sample, state_pool, state_gla, ada_w, ada_b, norm_g, pool_w_in,
           pool_w_grp, pool_scale, pool_w_out, gla_w_in, gla_w_gate_up, gla_b_gate, gla_norm_g, gla_w_out,
           final_g):
    n_prompt = x_prompt.shape[0]
    n_gla = gla_w_in.shape[0]
    mods = _ada_mods(jnp.concatenate([c_prompt, c_sample], axis=0), ada_w, ada_b)

    o_gl = 2 * GLA_DK + 2 * GLA_DV
    w_in_pad = jnp.concatenate(
        [gla_w_in, jnp.zeros((n_gla, D_MODEL, GLA_GL_PAD - GLA_RANK), gla_w_in.dtype)], axis=-1)
    assert w_in_pad.shape[-1] == o_gl + GLA_GL_PAD
    w_up_pad = jnp.concatenate(
        [gla_w_gate_up, jnp.zeros((n_gla, GLA_GL_PAD - GLA_RANK, GLA_DK), gla_w_gate_up.dtype)], axis=1)
    w = {
        "norm_g": norm_g.reshape(DEPTH, 1, D_MODEL),
        "pool_w_in": pool_w_in.astype(BF16),
        "pool_w_grp": pool_w_grp.astype(BF16),
        "pool_scale": pool_scale.reshape(-1, 1, D_INNER),
        "pool_w_out": pool_w_out.astype(BF16),
        "gla_w_in": w_in_pad.astype(BF16),
        "gla_w_up": w_up_pad.astype(BF16),
        "gla_b_gate": gla_b_gate.reshape(-1, 1, GLA_DK),
        "gla_norm_g": gla_norm_g.reshape(-1, 1, GLA_DV),
        "gla_w_out": gla_w_out.astype(BF16),
        "final_g": final_g.reshape(1, D_MODEL),
    }
    y_p, pool_p, gla_p = _run_trunk(x_prompt, mods[:, :, :n_prompt], None, None, 0, w,
                                    tb_pool=1, tb_gla=1, tl=256)
    y_s, pool_s, gla_s = _run_trunk(x_sample, mods[:, :, n_prompt:], state_pool, state_gla, PAST_LEN, w,
                                    tb_pool=32, tb_gla=8, tl=x_sample.shape[1])
    return (y_p, y_s, pool_p, gla_p, pool_s, gla_s)
```

```python
import functools

import jax
import jax.numpy as jnp
from jax import lax
from jax.experimental import pallas as pl
from jax.experimental.pallas import tpu as pltpu

D_MODEL = 1024
DEPTH = 4
PAST_LEN = 16384
D_INNER = 2048
POOL_WINDOWS = (2, 4, 8, 16)
POOL_GROUP = D_INNER // len(POOL_WINDOWS)
POOL_BUF = max(POOL_WINDOWS) - 1
POOL_HALO = POOL_BUF + 1
GLA_HEADS = 4
GLA_DK = 512
GLA_DV = D_INNER
GLA_HK = GLA_DK // GLA_HEADS
GLA_HV = GLA_DV // GLA_HEADS
GLA_RANK = 16
GLA_GATE_NORM = 16.0
GLA_CHUNK = 64
GLA_GL_PAD = 128
EPS = 1e-6
F32 = jnp.float32
BF16 = jnp.bfloat16

VMEM_LIMIT_BYTES = 62 * 1024 * 1024


def _silu(v):
    return v / (1.0 + jnp.exp(-v))


def _mod_norm(x, g, shift, scale):
    ms = jnp.mean(x * x, axis=-1, keepdims=True)
    y = x * lax.rsqrt(ms + EPS) * g
    return y * (1.0 + scale) + shift


def _const_spec(shape):
    nd = len(shape)
    return pl.BlockSpec(shape, lambda *_: (0,) * nd, pipeline_mode=pl.Buffered(1))


def _ada_kernel(c_ref, w_ref, b_ref, o_ref):
    cs = _silu(c_ref[...]).astype(BF16)
    o_ref[0, 0] = jnp.dot(cs, w_ref[0].astype(BF16), preferred_element_type=F32) + b_ref[0, 0]


def _ada_mods(c_all, ada_w, ada_b):
    nb = c_all.shape[0]
    b4 = ada_b.reshape(DEPTH, 3, 1, D_MODEL)
    return pl.pallas_call(
        _ada_kernel,
        grid=(DEPTH, 3),
        in_specs=[
            pl.BlockSpec((nb, D_MODEL), lambda li, j: (0, 0)),
            pl.BlockSpec((1, D_MODEL, D_MODEL), lambda li, j: (li, 0, j)),
            pl.BlockSpec((1, 1, 1, D_MODEL), lambda li, j: (li, j, 0, 0)),
        ],
        out_specs=pl.BlockSpec((1, 1, nb, D_MODEL), lambda li, j: (li, j, 0, 0)),
        out_shape=jax.ShapeDtypeStruct((DEPTH, 3, nb, D_MODEL), F32),
        name="ada_mods",
    )(c_all, ada_w, b4)


def _pool_kernel(*refs, tb, tl, n_l, pos0, fresh, has_prev):
    it = iter(refs)
    x_ref, sh_ref, sc_ref, gt_ref, g_ref = (next(it) for _ in range(5))
    win_ref, wgrp_ref, pscale_ref, wout_ref = (next(it) for _ in range(4))
    buf_ref = None if fresh else next(it)
    if has_prev:
        next(it)
    y_ref, nbuf_ref, ext_ref, gated_ref = (next(it) for _ in range(4))

    l = pl.program_id(1)
    m = tb * tl

    @pl.when(l == 0)
    def _():
        ext_ref[:, 0:POOL_HALO, :] = jnp.zeros((tb, POOL_HALO, D_INNER), F32)
        if not fresh:
            ext_ref[:, 1:POOL_HALO, :] = buf_ref[0]

    x = x_ref[...]
    h = _mod_norm(x, g_ref[...], sh_ref[...], sc_ref[...]).reshape(m, D_MODEL).astype(BF16)
    uz = jnp.dot(h, win_ref[...], preferred_element_type=F32)
    ext_ref[:, POOL_HALO:POOL_HALO + tl, :] = uz[:, :D_INNER].reshape(tb, tl, D_INNER)

    for gi, w in enumerate(POOL_WINDOWS):
        c0 = gi * POOL_GROUP
        u_g = ext_ref[:, POOL_HALO:POOL_HALO + tl, c0:c0 + POOL_GROUP]
        s = u_g
        for j in range(1, w):
            s = s + ext_ref[:, POOL_HALO - j:POOL_HALO - j + tl, c0:c0 + POOL_GROUP]
        if pos0 >= w - 1:
            mean = s * (1.0 / w)
        else:
            pos = pos0 + l * tl + lax.broadcasted_iota(jnp.int32, (tb, tl, POOL_GROUP), 1)
            mean = s / jnp.minimum(pos + 1, w).astype(F32)
        diff = (mean - u_g).reshape(m, POOL_GROUP).astype(BF16)
        mixed = jnp.dot(diff, wgrp_ref[gi], preferred_element_type=F32) * pscale_ref[:, c0:c0 + POOL_GROUP]
        z_g = uz[:, D_INNER + c0:D_INNER + c0 + POOL_GROUP]
        gated_ref[:, c0:c0 + POOL_GROUP] = (mixed * _silu(z_g)).astype(BF16)

    y = jnp.dot(gated_ref[...], wout_ref[...], preferred_element_type=F32)
    y_ref[...] = x + gt_ref[...] * y.reshape(tb, tl, D_MODEL)

    @pl.when(l == n_l - 1)
    def _():
        nbuf_ref[0] = ext_ref[:, tl + 1:tl + POOL_HALO, :]

    if n_l > 1:
        @pl.when(l < n_l - 1)
        def _():
            ext_ref[:, 0:POOL_HALO, :] = ext_ref[:, tl:tl + POOL_HALO, :]


def _pool_layer(x, shift, scale, gate, g, w_in, w_grp, pscale, w_out, state, prev_out, *, layer, n_layers,
                tb, tl, pos0):
    b, L, _ = x.shape
    n_l = L // tl
    fresh = state is None
    has_prev = prev_out is not None
    seq = lambda bi, li: (bi, li, 0)
    per_b = lambda bi, li: (bi, 0, 0)
    in_specs = [
        pl.BlockSpec((tb, tl, D_MODEL), seq),
        pl.BlockSpec((tb, 1, D_MODEL), per_b),
        pl.BlockSpec((tb, 1, D_MODEL), per_b),
        pl.BlockSpec((tb, 1, D_MODEL), per_b),
        _const_spec((1, D_MODEL)),
        _const_spec((D_MODEL, 2 * D_INNER)),
        _const_spec((len(POOL_WINDOWS), POOL_GROUP, POOL_GROUP)),
        _const_spec((1, D_INNER)),
        _const_spec((D_INNER, D_MODEL)),
    ]
    args = [x, shift, scale, gate, g, w_in, w_grp, pscale, w_out]
    if not fresh:
        in_specs.append(pl.BlockSpec((1, tb, POOL_BUF, D_INNER), lambda bi, li: (layer, bi, 0, 0)))
        args.append(state)
    aliases = {}
    if has_prev:
        in_specs.append(pl.BlockSpec(memory_space=pl.ANY))
        aliases = {len(args): 1}
        args.append(prev_out)
    kern = functools.partial(_pool_kernel, tb=tb, tl=tl, n_l=n_l, pos0=pos0, fresh=fresh, has_prev=has_prev)
    return pl.pallas_call(
        kern,
        grid=(b // tb, n_l),
        in_specs=in_specs,
        out_specs=[
            pl.BlockSpec((tb, tl, D_MODEL), seq),
            pl.BlockSpec((1, tb, POOL_BUF, D_INNER), lambda bi, li: (layer, bi, 0, 0)),
        ],
        out_shape=[
            jax.ShapeDtypeStruct(x.shape, F32),
            jax.ShapeDtypeStruct((n_layers, b, POOL_BUF, D_INNER), F32),
        ],
        scratch_shapes=[
            pltpu.VMEM((tb, POOL_HALO + tl, D_INNER), F32),
            pltpu.VMEM((tb * tl, D_INNER), BF16),
        ],
        input_output_aliases=aliases,
        compiler_params=pltpu.CompilerParams(
            dimension_semantics=("arbitrary", "arbitrary"), vmem_limit_bytes=VMEM_LIMIT_BYTES),
        name=f"pool_layer{layer}_{'fresh' if fresh else 'cont'}",
    )(*args)


def _gla_kernel(*refs, tb, tl, n_l, fresh, has_prev, final):
    it = iter(refs)
    x_ref, sh_ref, sc_ref, gt_ref, g_ref = (next(it) for _ in range(5))
    win_ref, wup_ref, bg_ref, ng_ref, wout_ref = (next(it) for _ in range(5))
    fg_ref = next(it) if final else None
    s0_ref = None if fresh else next(it)
    if has_prev:
        next(it)
    y_ref, s_ref, gated_ref = (next(it) for _ in range(3))

    l = pl.program_id(1)
    m = tb * tl

    @pl.when(l == 0)
    def _():
        if fresh:
            s_ref[...] = jnp.zeros(s_ref.shape, F32)
        else:
            s_ref[...] = s0_ref[...]

    x = x_ref[...]
    h = _mod_norm(x, g_ref[...], sh_ref[...], sc_ref[...]).reshape(m, D_MODEL).astype(BF16)
    proj = jnp.dot(h, win_ref[...], preferred_element_type=F32)
    o_v, o_z, o_gl = 2 * GLA_DK, 2 * GLA_DK + GLA_DV, 2 * GLA_DK + 2 * GLA_DV
    gl = proj[:, o_gl:o_gl + GLA_GL_PAD].astype(BF16)
    gate_pre = jnp.dot(gl, wup_ref[...], preferred_element_type=F32) + bg_ref[...]
    lg = (jnp.minimum(gate_pre, 0.0) - jnp.log(1.0 + jnp.exp(-jnp.abs(gate_pre)))) * (1.0 / GLA_GATE_NORM)

    def finish_head(rows, hd, o_h):
        o_h = o_h * lax.rsqrt(jnp.mean(o_h * o_h, axis=-1, keepdims=True) + EPS)
        hc = slice(hd * GLA_HV, (hd + 1) * GLA_HV)
        z_h = proj[rows, o_z + hd * GLA_HV:o_z + (hd + 1) * GLA_HV]
        gated_ref[rows, hc] = (o_h * ng_ref[:, hc] * _silu(z_h)).astype(BF16)

    if tl >= 2 * GLA_CHUNK:
        _gla_long_chunks(proj, lg, s_ref, finish_head, m=m, tl=tl)
    else:
        _gla_short_chunks(proj, lg, s_ref, finish_head, m=m, tl=tl)

    y = jnp.dot(gated_ref[...], wout_ref[...], preferred_element_type=F32)
    out = x + gt_ref[...] * y.reshape(tb, tl, D_MODEL)
    if final:
        out = out * lax.rsqrt(jnp.mean(out * out, axis=-1, keepdims=True) + EPS) * fg_ref[...]
    y_ref[...] = out


def _prefix_sums(pat, lg_rows):
    hi = lg_rows.astype(BF16)
    lo = (lg_rows - hi.astype(F32)).astype(BF16)
    return jnp.dot(pat, hi, preferred_element_type=F32) + jnp.dot(pat, lo, preferred_element_type=F32)


def _decay_cols(decay_row):
    d_col = jnp.transpose(jnp.broadcast_to(decay_row, (GLA_HK, GLA_HK)))
    return jnp.concatenate([d_col] * (GLA_HV // GLA_HK), axis=1)


_NT = (((1,), (1,)), ((), ()))
_TN = (((0,), (0,)), ((), ()))


def _gla_long_chunks(proj, lg, s_ref, finish_head, *, m, tl):
    hb = GLA_CHUNK
    cr = 2 * hb
    o_v = 2 * GLA_DK
    ri = lax.broadcasted_iota(jnp.int32, (cr, cr), 0)
    ci = lax.broadcasted_iota(jnp.int32, (cr, cr), 1)
    causal = ci <= ri
    tri = jnp.where(causal, 1.0, 0.0).astype(BF16)
    zeros = jnp.zeros((hb, GLA_HK), BF16)
    for c in range(m // cr):
        rows = slice(c * cr, (c + 1) * cr)
        si = (c * cr) // tl
        bcum = _prefix_sums(tri, lg[rows])
        b_a, b_b = bcum[:hb], bcum[hb:]
        ref_a, bnd = bcum[hb // 2 - 1:hb // 2], bcum[hb - 1:hb]
        ref_b, tot = bcum[hb + hb // 2 - 1:hb + hb // 2], bcum[cr - 1:cr]
        q = proj[rows, 0:GLA_DK] * (GLA_HK ** -0.5)
        k = proj[rows, GLA_DK:2 * GLA_DK]
        q_a, q_b, k_a, k_b = q[:hb], q[hb:], k[:hb], k[hb:]
        qi_a = (q_a * jnp.exp(b_a - ref_a)).astype(BF16)
        ki_a = (k_a * jnp.exp(ref_a - b_a)).astype(BF16)
        qi_b = (q_b * jnp.exp(b_b - ref_b)).astype(BF16)
        ki_b = (k_b * jnp.exp(ref_b - b_b)).astype(BF16)
        qx_b = (q_b * jnp.exp(b_b - bnd)).astype(BF16)
        kx_a = (k_a * jnp.exp(bnd - b_a)).astype(BF16)
        q_dec = (q * jnp.exp(bcum)).astype(BF16)
        k_state = (k * jnp.exp(tot - bcum)).astype(BF16)
        decay = jnp.exp(tot)
        for hd in range(GLA_HEADS):
            kc = slice(hd * GLA_HK, (hd + 1) * GLA_HK)
            v_h = proj[rows, o_v + hd * GLA_HV:o_v + (hd + 1) * GLA_HV].astype(BF16)
            s_old = s_ref[0, si, hd]
            att_a = lax.dot_general(qi_a[:, kc], jnp.concatenate([ki_a[:, kc], zeros], axis=0), _NT,
                                    preferred_element_type=F32)
            keys_b = jnp.concatenate([jnp.concatenate([kx_a[:, kc], zeros], axis=1),
                                      jnp.concatenate([zeros, ki_b[:, kc]], axis=1)], axis=0)
            att_b = lax.dot_general(jnp.concatenate([qx_b[:, kc], qi_b[:, kc]], axis=1), keys_b, _NT,
                                    preferred_element_type=F32)
            att = jnp.where(causal, jnp.concatenate([att_a, att_b], axis=0), 0.0).astype(BF16)
            o_h = jnp.dot(jnp.concatenate([att, q_dec[:, kc]], axis=1),
                          jnp.concatenate([v_h, s_old.astype(BF16)], axis=0), preferred_element_type=F32)
            upd = lax.dot_general(k_state[:, kc], v_h, _TN, preferred_element_type=F32)
            s_ref[0, si, hd] = s_old * _decay_cols(decay[:, kc]) + upd
            finish_head(rows, hd, o_h)


def _gla_short_chunks(proj, lg, s_ref, finish_head, *, m, tl):
    seg = tl
    n_seg = GLA_CHUNK // seg
    o_v = 2 * GLA_DK
    ri = lax.broadcasted_iota(jnp.int32, (GLA_CHUNK, GLA_CHUNK), 0)
    ci = lax.broadcasted_iota(jnp.int32, (GLA_CHUNK, GLA_CHUNK), 1)
    sh = seg.bit_length() - 1
    same = (ri >> sh) == (ci >> sh)
    causal = same & (ci <= ri)
    mid = same & ((ci & (seg - 1)) <= (seg // 2 - 1))
    pat = jnp.concatenate([jnp.where(mk, 1.0, 0.0).astype(BF16) for mk in (causal, mid, same)], axis=0)

    for c in range(m // GLA_CHUNK):
        r0 = c * GLA_CHUNK
        rows = slice(r0, r0 + GLA_CHUNK)
        sums = _prefix_sums(pat, lg[rows])
        bcum, bref, btot = sums[:GLA_CHUNK], sums[GLA_CHUNK:2 * GLA_CHUNK], sums[2 * GLA_CHUNK:]
        q = proj[rows, 0:GLA_DK] * (GLA_HK ** -0.5)
        k = proj[rows, GLA_DK:2 * GLA_DK]
        q_intra = (q * jnp.exp(bcum - bref)).astype(BF16)
        k_intra = (k * jnp.exp(bref - bcum)).astype(BF16)
        q_dec = (q * jnp.exp(bcum)).astype(BF16)
        k_state = (k * jnp.exp(btot - bcum)).astype(BF16)
        decay = jnp.exp(btot)
        for hd in range(GLA_HEADS):
            kc = slice(hd * GLA_HK, (hd + 1) * GLA_HK)
            vc0 = o_v + hd * GLA_HV
            v_h = proj[rows, vc0:vc0 + GLA_HV].astype(BF16)
            att = lax.dot_general(q_intra[:, kc], k_intra[:, kc], _NT, preferred_element_type=F32)
            att = jnp.where(causal, att, 0.0).astype(BF16)
            o_h = jnp.dot(att, v_h, preferred_element_type=F32)
            o_inter = []
            for sg in range(n_seg):
                srows = slice(sg * seg, (sg + 1) * seg)
                si = (r0 + sg * seg) // tl
                s_old = s_ref[0, si, hd]
                o_inter.append(jnp.dot(q_dec[srows, kc], s_old.astype(BF16), preferred_element_type=F32))
                upd = lax.dot_general(k_state[srows, kc], v_h[srows], _TN, preferred_element_type=F32)
                s_ref[0, si, hd] = s_old * _decay_cols(decay[sg * seg:sg * seg + 1, kc]) + upd
            finish_head(rows, hd, o_h + jnp.concatenate(o_inter, axis=0))


def _gla_layer(x, shift, scale, gate, g, w_in, w_up, b_gate, norm_g, w_out, final_g, state, prev_out, *,
               layer, n_layers, tb, tl):
    b, L, _ = x.shape
    n_l = L // tl
    fresh = state is None
    has_prev = prev_out is not None
    final = final_g is not None
    n_in = w_in.shape[1]
    seq = lambda bi, li: (bi, li, 0)
    per_b = lambda bi, li: (bi, 0, 0)
    st_block = (1, tb, GLA_HEADS, GLA_HK, GLA_HV)
    st_map = lambda bi, li: (layer, bi, 0, 0, 0)
    in_specs = [
        pl.BlockSpec((tb, tl, D_MODEL), seq),
        pl.BlockSpec((tb, 1, D_MODEL), per_b),
        pl.BlockSpec((tb, 1, D_MODEL), per_b),
        pl.BlockSpec((tb, 1, D_MODEL), per_b),
        _const_spec((1, D_MODEL)),
        _const_spec((D_MODEL, n_in)),
        _const_spec((GLA_GL_PAD, GLA_DK)),
        _const_spec((1, GLA_DK)),
        _const_spec((1, GLA_DV)),
        _const_spec((GLA_DV, D_MODEL)),
    ]
    args = [x, shift, scale, gate, g, w_in, w_up, b_gate, norm_g, w_out]
    if final:
        in_specs.append(_const_spec((1, D_MODEL)))
        args.append(final_g)
    if not fresh:
        in_specs.append(pl.BlockSpec(st_block, st_map))
        args.append(state)
    aliases = {}
    if has_prev:
        in_specs.append(pl.BlockSpec(memory_space=pl.ANY))
        aliases = {len(args): 1}
        args.append(prev_out)
    kern = functools.partial(_gla_kernel, tb=tb, tl=tl, n_l=n_l, fresh=fresh, has_prev=has_prev, final=final)
    return pl.pallas_call(
        kern,
        grid=(b // tb, n_l),
        in_specs=in_specs,
        out_specs=[pl.BlockSpec((tb, tl, D_MODEL), seq), pl.BlockSpec(st_block, st_map)],
        out_shape=[
            jax.ShapeDtypeStruct(x.shape, F32),
            jax.ShapeDtypeStruct((n_layers, b, GLA_HEADS, GLA_HK, GLA_HV), F32),
        ],
        scratch_shapes=[pltpu.VMEM((tb * tl, GLA_DV), BF16)],
        input_output_aliases=aliases,
        compiler_params=pltpu.CompilerParams(
            dimension_semantics=("arbitrary", "arbitrary"), vmem_limit_bytes=VMEM_LIMIT_BYTES),
        name=f"gla_layer{layer}_{'fresh' if fresh else 'cont'}",
    )(*args)


def _run_trunk(x, mods, pool_state, gla_state, pos0, w, *, tb_pool, tb_gla, tl):
    b = x.shape[0]
    n_pool, n_gla = (DEPTH + 1) // 2, DEPTH // 2
    new_pool = new_gla = None
    for li in range(DEPTH):
        shift, scale, gate = (mods[li, j].reshape(b, 1, D_MODEL) for j in range(3))
        j = li // 2
        if li % 2 == 0:
            x, new_pool = _pool_layer(
                x, shift, scale, gate, w["norm_g"][li], w["pool_w_in"][j], w["pool_w_grp"][j],
                w["pool_scale"][j], w["pool_w_out"][j], pool_state, new_pool,
                layer=j, n_layers=n_pool, tb=tb_pool, tl=tl, pos0=pos0)
        else:
            x, new_gla = _gla_layer(
                x, shift, scale, gate, w["norm_g"][li], w["gla_w_in"][j], w["gla_w_up"][j], w["gla_b_gate"][j],
                w["gla_norm_g"][j], w["gla_w_out"][j], w["final_g"] if li == DEPTH - 1 else None,
                gla_state, new_gla, layer=j, n_layers=n_gla, tb=tb_gla, tl=tl)
    return x, new_pool, new_gla


def kernel(x_prompt, x_sample, c_prompt, c_sample, state_pool, state_gla, ada_w, ada_b, norm_g, pool_w_in,
           pool_w_grp, pool_scale, pool_w_out, gla_w_in, gla_w_gate_up, gla_b_gate, gla_norm_g, gla_w_out,
           final_g):
    n_prompt = x_prompt.shape[0]
    n_gla = gla_w_in.shape[0]
    mods = _ada_mods(jnp.concatenate([c_prompt, c_sample], axis=0), ada_w, ada_b)

    o_gl = 2 * GLA_DK + 2 * GLA_DV
    w_in_pad = jnp.concatenate(
        [gla_w_in, jnp.zeros((n_gla, D_MODEL, GLA_GL_PAD - GLA_RANK), gla_w_in.dtype)], axis=-1)
    assert w_in_pad.shape[-1] == o_gl + GLA_GL_PAD
    w_up_pad = jnp.concatenate(
        [gla_w_gate_up, jnp.zeros((n_gla, GLA_GL_PAD - GLA_RANK, GLA_DK), gla_w_gate_up.dtype)], axis=1)
    w = {
        "norm_g": norm_g.reshape(DEPTH, 1, D_MODEL),
        "pool_w_in": pool_w_in.astype(BF16),
        "pool_w_grp": pool_w_grp.astype(BF16),
        "pool_scale": pool_scale.reshape(-1, 1, D_INNER),
        "pool_w_out": pool_w_out.astype(BF16),
        "gla_w_in": w_in_pad.astype(BF16),
        "gla_w_up": w_up_pad.astype(BF16),
        "gla_b_gate": gla_b_gate.reshape(-1, 1, GLA_DK),
        "gla_norm_g": gla_norm_g.reshape(-1, 1, GLA_DV),
        "gla_w_out": gla_w_out.astype(BF16),
        "final_g": final_g.reshape(1, D_MODEL),
    }
    y_p, pool_p, gla_p = _run_trunk(x_prompt, mods[:, :, :n_prompt], None, None, 0, w,
                                    tb_pool=1, tb_gla=1, tl=256)
    y_s, pool_s, gla_s = _run_trunk(x_sample, mods[:, :, n_prompt:], state_pool, state_gla, PAST_LEN, w,
                                    tb_pool=32, tb_gla=8, tl=x_sample.shape[1])
    return (y_p, y_s, pool_p, gla_p, pool_s, gla_s)
```

```python
import functools

import jax
import jax.numpy as jnp
from jax import lax
from jax.experimental import pallas as pl
from jax.experimental.pallas import tpu as pltpu

D_MODEL = 1024
DEPTH = 4
PAST_LEN = 16384
D_INNER = 2048
POOL_WINDOWS = (2, 4, 8, 16)
POOL_GROUP = D_INNER // len(POOL_WINDOWS)
POOL_BUF = max(POOL_WINDOWS) - 1
POOL_HALO = POOL_BUF + 1
GLA_HEADS = 4
GLA_DK = 512
GLA_DV = D_INNER
GLA_HK = GLA_DK // GLA_HEADS
GLA_HV = GLA_DV // GLA_HEADS
GLA_RANK = 16
GLA_GATE_NORM = 16.0
GLA_CHUNK = 64
GLA_GL_PAD = 128
EPS = 1e-6
F32 = jnp.float32
BF16 = jnp.bfloat16

VMEM_LIMIT_BYTES = 62 * 1024 * 1024


def _silu(v):
    return v / (1.0 + jnp.exp(-v))


def _mod_norm(x, g, shift, scale):
    ms = jnp.mean(x * x, axis=-1, keepdims=True)
    y = x * lax.rsqrt(ms + EPS) * g
    return y * (1.0 + scale) + shift


def _const_spec(shape, lead=(), tail=None):
    idx = tuple(lead) + tuple(tail if tail is not None else (0,) * len(shape))
    return pl.BlockSpec((None,) * len(lead) + tuple(shape), lambda *_: idx, pipeline_mode=pl.Buffered(1))


def _ada_kernel(c_ref, w_ref, b_ref, o_ref):
    cs = _silu(c_ref[...]).astype(BF16)
    o_ref[0, 0] = jnp.dot(cs, w_ref[0].astype(BF16), preferred_element_type=F32) + b_ref[0, 0]


def _ada_mods(c_all, ada_w, ada_b):
    nb = c_all.shape[0]
    b4 = ada_b.reshape(DEPTH, 3, 1, D_MODEL)
    return pl.pallas_call(
        _ada_kernel,
        grid=(DEPTH, 3),
        in_specs=[
            pl.BlockSpec((nb, D_MODEL), lambda li, j: (0, 0)),
            pl.BlockSpec((1, D_MODEL, D_MODEL), lambda li, j: (li, 0, j)),
            pl.BlockSpec((1, 1, 1, D_MODEL), lambda li, j: (li, j, 0, 0)),
        ],
        out_specs=pl.BlockSpec((1, 1, nb, D_MODEL), lambda li, j: (li, j, 0, 0)),
        out_shape=jax.ShapeDtypeStruct((DEPTH, 3, nb, D_MODEL), F32),
        name="ada_mods",
    )(c_all, ada_w, b4)


def _pool_kernel(*refs, tb, tl, n_l, pos0, fresh, has_prev):
    it = iter(refs)
    x_ref, sh_ref, sc_ref, gt_ref, g_ref = (next(it) for _ in range(5))
    win_ref, wgrp_ref, pscale_ref, wout_ref = (next(it) for _ in range(4))
    buf_ref = None if fresh else next(it)
    if has_prev:
        next(it)
    y_ref, nbuf_ref, ext_ref, gated_ref = (next(it) for _ in range(4))

    l = pl.program_id(1)
    m = tb * tl

    @pl.when(l == 0)
    def _():
        ext_ref[:, 0:POOL_HALO, :] = jnp.zeros((tb, POOL_HALO, D_INNER), F32)
        if not fresh:
            ext_ref[:, 1:POOL_HALO, :] = buf_ref[0]

    x = x_ref[...]
    h = _mod_norm(x, g_ref[...], sh_ref[...], sc_ref[...]).reshape(m, D_MODEL).astype(BF16)
    uz = jnp.dot(h, win_ref[...], preferred_element_type=F32)
    ext_ref[:, POOL_HALO:POOL_HALO + tl, :] = uz[:, :D_INNER].reshape(tb, tl, D_INNER)

    for gi, w in enumerate(POOL_WINDOWS):
        c0 = gi * POOL_GROUP
        u_g = ext_ref[:, POOL_HALO:POOL_HALO + tl, c0:c0 + POOL_GROUP]
        s = u_g
        for j in range(1, w):
            s = s + ext_ref[:, POOL_HALO - j:POOL_HALO - j + tl, c0:c0 + POOL_GROUP]
        if pos0 >= w - 1:
            mean = s * (1.0 / w)
        else:
            pos = pos0 + l * tl + lax.broadcasted_iota(jnp.int32, (tb, tl, POOL_GROUP), 1)
            mean = s / jnp.minimum(pos + 1, w).astype(F32)
        diff = (mean - u_g).reshape(m, POOL_GROUP).astype(BF16)
        mixed = jnp.dot(diff, wgrp_ref[gi], preferred_element_type=F32) * pscale_ref[:, c0:c0 + POOL_GROUP]
        z_g = uz[:, D_INNER + c0:D_INNER + c0 + POOL_GROUP]
        gated_ref[:, c0:c0 + POOL_GROUP] = (mixed * _silu(z_g)).astype(BF16)

    y = jnp.dot(gated_ref[...], wout_ref[...], preferred_element_type=F32)
    y_ref[...] = x + gt_ref[...] * y.reshape(tb, tl, D_MODEL)

    @pl.when(l == n_l - 1)
    def _():
        nbuf_ref[0] = ext_ref[:, tl + 1:tl + POOL_HALO, :]

    if n_l > 1:
        @pl.when(l < n_l - 1)
        def _():
            ext_ref[:, 0:POOL_HALO, :] = ext_ref[:, tl:tl + POOL_HALO, :]


def _pool_layer(x, shift, scale, gate, g, w_in, w_grp, pscale, w_out, state, prev_out, *, layer, n_layers,
                tb, tl, pos0):
    b, L, _ = x.shape
    n_l = L // tl
    fresh = state is None
    has_prev = prev_out is not None
    seq = lambda bi, li: (bi, li, 0)
    per_b = lambda bi, li: (bi, 0, 0)
    in_specs = [
        pl.BlockSpec((tb, tl, D_MODEL), seq),
        pl.BlockSpec((tb, 1, D_MODEL), per_b),
        pl.BlockSpec((tb, 1, D_MODEL), per_b),
        pl.BlockSpec((tb, 1, D_MODEL), per_b),
        _const_spec((1, D_MODEL)),
        _const_spec((D_MODEL, 2 * D_INNER), lead=(layer,)),
        _const_spec((len(POOL_WINDOWS), POOL_GROUP, POOL_GROUP), lead=(layer,)),
        _const_spec((1, D_INNER), lead=(layer,)),
        _const_spec((D_INNER, D_MODEL), lead=(layer,)),
    ]
    args = [x, shift, scale, gate, g, w_in, w_grp, pscale, w_out]
    if not fresh:
        in_specs.append(pl.BlockSpec((1, tb, POOL_BUF, D_INNER), lambda bi, li: (layer, bi, 0, 0)))
        args.append(state)
    aliases = {}
    if has_prev:
        in_specs.append(pl.BlockSpec(memory_space=pl.ANY))
        aliases = {len(args): 1}
        args.append(prev_out)
    kern = functools.partial(_pool_kernel, tb=tb, tl=tl, n_l=n_l, pos0=pos0, fresh=fresh, has_prev=has_prev)
    return pl.pallas_call(
        kern,
        grid=(b // tb, n_l),
        in_specs=in_specs,
        out_specs=[
            pl.BlockSpec((tb, tl, D_MODEL), seq),
            pl.BlockSpec((1, tb, POOL_BUF, D_INNER), lambda bi, li: (layer, bi, 0, 0)),
        ],
        out_shape=[
            jax.ShapeDtypeStruct(x.shape, F32),
            jax.ShapeDtypeStruct((n_layers, b, POOL_BUF, D_INNER), F32),
        ],
        scratch_shapes=[
            pltpu.VMEM((tb, POOL_HALO + tl, D_INNER), F32),
            pltpu.VMEM((tb * tl, D_INNER), BF16),
        ],
        input_output_aliases=aliases,
        compiler_params=pltpu.CompilerParams(
            dimension_semantics=("arbitrary", "arbitrary"), vmem_limit_bytes=VMEM_LIMIT_BYTES),
        name=f"pool_layer{layer}_{'fresh' if fresh else 'cont'}",
    )(*args)


def _gla_kernel(*refs, tb, tl, n_l, fresh, has_prev, final):
    it = iter(refs)
    x_ref, sh_ref, sc_ref, gt_ref, g_ref = (next(it) for _ in range(5))
    win_refs = [next(it) for _ in range(GLA_N_W_BLOCKS)]
    wgl_ref, wup_ref, bg_ref, ng_ref, wout_ref = (next(it) for _ in range(5))
    fg_ref = next(it) if final else None
    s0_ref = None if fresh else next(it)
    if has_prev:
        next(it)
    y_ref, s_ref = next(it), next(it)
    qk_ref, v_ref, sz_ref, lg_ref, gated_ref = (next(it) for _ in range(5))

    l = pl.program_id(1)
    m = tb * tl

    @pl.when(l == 0)
    def _():
        if fresh:
            s_ref[...] = jnp.zeros(s_ref.shape, F32)
        else:
            s_ref[...] = s0_ref[...]

    _emit_interleaved(
        _gla_project(x_ref, sh_ref, sc_ref, g_ref, win_refs, wgl_ref, wup_ref, bg_ref,
                     qk_ref, v_ref, sz_ref, lg_ref, m=m))
    _emit_interleaved(
        _gla_mix_out(x_ref, gt_ref, ng_ref, wout_ref, fg_ref, qk_ref, v_ref, sz_ref, lg_ref, gated_ref, s_ref,
                     y_ref, tb=tb, tl=tl))


def _emit_interleaved(*stages):
    live = list(stages)
    while live:
        for stage in list(live):
            try:
                next(stage)
            except StopIteration:
                live.remove(stage)


GLA_COL_PIECE = 512


def _gla_project(x_ref, sh_ref, sc_ref, g_ref, win_refs, wgl_ref, wup_ref, bg_ref,
                 qk_ref, v_ref, sz_ref, lg_ref, *, m):
    h = _mod_norm(x_ref[...], g_ref[...], sh_ref[...], sc_ref[...]).reshape(m, D_MODEL).astype(BF16)

    def piece(c0):
        blk, off = divmod(c0, GLA_W_BLOCK)
        return jnp.dot(h, win_refs[blk][:, off:off + GLA_COL_PIECE], preferred_element_type=F32)

    o_v, o_z = 2 * GLA_DK, 2 * GLA_DK + GLA_DV
    for c0 in range(0, 2 * GLA_DK, GLA_COL_PIECE):
        qk_ref[:, c0:c0 + GLA_COL_PIECE] = piece(c0)
        yield
    for c0 in range(0, GLA_DV, GLA_COL_PIECE):
        v_ref[:, c0:c0 + GLA_COL_PIECE] = piece(o_v + c0).astype(BF16)
        yield
    for c0 in range(0, GLA_DV, GLA_COL_PIECE):
        sz_ref[:, c0:c0 + GLA_COL_PIECE] = _silu(piece(o_z + c0))
        yield
    gl = jnp.dot(h, wgl_ref[...], preferred_element_type=F32).astype(BF16)
    gate_pre = jnp.dot(gl, wup_ref[...], preferred_element_type=F32) + bg_ref[...]
    lg_ref[...] = ((jnp.minimum(gate_pre, 0.0) - jnp.log(1.0 + jnp.exp(-jnp.abs(gate_pre))))
                   * (1.0 / GLA_GATE_NORM))
    yield


def _gla_mix_out(x_ref, gt_ref, ng_ref, wout_ref, fg_ref, qk_ref, v_ref, sz_ref, lg_ref, gated_ref, s_ref, y_ref,
                 *, tb, tl):
    m = tb * tl

    def finish_head(rows, hd, o_h):
        o_h = o_h * lax.rsqrt(jnp.mean(o_h * o_h, axis=-1, keepdims=True) + EPS)
        hc = slice(hd * GLA_HV, (hd + 1) * GLA_HV)
        gated_ref[rows, hc] = (o_h * ng_ref[:, hc] * sz_ref[rows, hc]).astype(BF16)

    if tl >= 2 * GLA_CHUNK:
        yield from _gla_long_chunks(qk_ref, v_ref, lg_ref, s_ref, finish_head, m=m, tl=tl)
    else:
        yield from _gla_short_chunks(qk_ref, v_ref, lg_ref, s_ref, finish_head, m=m, tl=tl)

    gated = gated_ref[...]
    for c0 in range(0, D_MODEL, GLA_COL_PIECE // 2):
        cols = slice(c0, c0 + GLA_COL_PIECE // 2)
        y = jnp.dot(gated, wout_ref[:, cols], preferred_element_type=F32)
        y_ref[:, :, cols] = x_ref[:, :, cols] + gt_ref[:, :, cols] * y.reshape(tb, tl, GLA_COL_PIECE // 2)
        yield
    if fg_ref is not None:
        out = y_ref[...]
        y_ref[...] = out * lax.rsqrt(jnp.mean(out * out, axis=-1, keepdims=True) + EPS) * fg_ref[...]
        yield


def _gla_pipe_kernel(*refs, tl, n_l, has_prev, final):
    it = iter(refs)
    x_ref, xp_ref, sh_ref, sc_ref, gt_ref, g_ref = (next(it) for _ in range(6))
    win_refs = [next(it) for _ in range(GLA_N_W_BLOCKS)]
    wgl_ref, wup_ref, bg_ref, ng_ref, wout_ref = (next(it) for _ in range(5))
    fg_ref = next(it) if final else None
    if has_prev:
        next(it)
    y_ref, s_ref = next(it), next(it)
    slots = [tuple(next(it) for _ in range(4)) for _ in range(2)]
    gated_ref = next(it)

    i = pl.program_id(0)
    l_y = jnp.maximum(i - 1, 0) % n_l

    @pl.when(i == 0)
    def _():
        for ref in slots[1]:
            ref[...] = jnp.zeros(ref.shape, ref.dtype)

    @pl.when(l_y == 0)
    def _():
        s_ref[...] = jnp.zeros(s_ref.shape, F32)

    for parity in range(2):
        @pl.when(i % 2 == parity)
        def _():
            _emit_interleaved(
                _gla_project(x_ref, sh_ref, sc_ref, g_ref, win_refs, wgl_ref, wup_ref, bg_ref,
                             *slots[parity], m=tl),
                _gla_mix_out(xp_ref, gt_ref, ng_ref, wout_ref, fg_ref, *slots[1 - parity], gated_ref, s_ref,
                             y_ref, tb=1, tl=tl))


def _prefix_sums(pat, lg_rows):
    hi = lg_rows.astype(BF16)
    lo = (lg_rows - hi.astype(F32)).astype(BF16)
    return jnp.dot(pat, hi, preferred_element_type=F32) + jnp.dot(pat, lo, preferred_element_type=F32)


def _decay_cols(decay_row):
    d_col = jnp.transpose(jnp.broadcast_to(decay_row, (GLA_HK, GLA_HK)))
    return jnp.concatenate([d_col] * (GLA_HV // GLA_HK), axis=1)


_NT = (((1,), (1,)), ((), ()))
_TN = (((0,), (0,)), ((), ()))


def _gla_long_chunks(qk_ref, v_ref, lg_ref, s_ref, finish_head, *, m, tl):
    hb = GLA_CHUNK
    cr = 2 * hb
    ri =lax.broadcasted_iota(jnp.int32, (cr, cr), 0)
    ci = lax.broadcasted_iota(jnp.int32, (cr, cr), 1)
    causal = ci <= ri
    tri = jnp.where(causal, 1.0, 0.0).astype(BF16)
    zeros = jnp.zeros((hb, GLA_HK), BF16)
    for c in range(m // cr):
        rows = slice(c * cr, (c + 1) * cr)
        si = (c * cr) // tl
        bcum = _prefix_sums(tri, lg_ref[rows, :])
        b_a, b_b = bcum[:hb], bcum[hb:]
        ref_a, bnd = bcum[hb // 2 - 1:hb // 2], bcum[hb - 1:hb]
        ref_b, tot = bcum[hb + hb // 2 - 1:hb + hb // 2], bcum[cr - 1:cr]
        q = qk_ref[rows, 0:GLA_DK] * (GLA_HK ** -0.5)
        k = qk_ref[rows, GLA_DK:2 * GLA_DK]
        q_a, q_b, k_a, k_b = q[:hb], q[hb:], k[:hb], k[hb:]
        qi_a = (q_a * jnp.exp(b_a - ref_a)).astype(BF16)
        ki_a = (k_a * jnp.exp(ref_a - b_a)).astype(BF16)
        qi_b = (q_b * jnp.exp(b_b - ref_b)).astype(BF16)
        ki_b = (k_b * jnp.exp(ref_b - b_b)).astype(BF16)
        qx_b = (q_b * jnp.exp(b_b - bnd)).astype(BF16)
        kx_a = (k_a * jnp.exp(bnd - b_a)).astype(BF16)
        q_dec = (q * jnp.exp(bcum)).astype(BF16)
        k_state = (k * jnp.exp(tot - bcum)).astype(BF16)
        decay = jnp.exp(tot)
        yield
        for hd in range(GLA_HEADS):
            kc = slice(hd * GLA_HK, (hd + 1) * GLA_HK)
            v_h = v_ref[rows, hd * GLA_HV:(hd + 1) * GLA_HV]
            s_old = s_ref[0, si, hd]
            att_a = lax.dot_general(qi_a[:, kc], jnp.concatenate([ki_a[:, kc], zeros], axis=0), _NT,
                                    preferred_element_type=F32)
            keys_b = jnp.concatenate([jnp.concatenate([kx_a[:, kc], zeros], axis=1),
                                      jnp.concatenate([zeros, ki_b[:, kc]], axis=1)], axis=0)
            att_b = lax.dot_general(jnp.concatenate([qx_b[:, kc], qi_b[:, kc]], axis=1), keys_b, _NT,
                                    preferred_element_type=F32)
            att = jnp.where(causal, jnp.concatenate([att_a, att_b], axis=0), 0.0).astype(BF16)
            o_h = jnp.dot(jnp.concatenate([att, q_dec[:, kc]], axis=1),
                          jnp.concatenate([v_h, s_old.astype(BF16)], axis=0), preferred_element_type=F32)
            upd = lax.dot_general(k_state[:, kc], v_h, _TN, preferred_element_type=F32)
            s_ref[0, si, hd] = s_old * _decay_cols(decay[:, kc]) + upd
            finish_head(rows, hd, o_h)
            yield


def _gla_short_chunks(qk_ref, v_ref, lg_ref, s_ref, finish_head, *, m, tl):
    seg = tl
    n_seg = GLA_CHUNK // seg
    ri = lax.broadcasted_iota(jnp.int32, (GLA_CHUNK, GLA_CHUNK), 0)
    ci = lax.broadcasted_iota(jnp.int32, (GLA_CHUNK, GLA_CHUNK), 1)
    sh = seg.bit_length() - 1
    same = (ri >> sh) == (ci >> sh)
    causal = same & (ci <= ri)
    mid = same & ((ci & (seg - 1)) <= (seg // 2 - 1))
    pat = jnp.concatenate([jnp.where(mk, 1.0, 0.0).astype(BF16) for mk in (causal, mid, same)], axis=0)

    for c in range(m // GLA_CHUNK):
        r0 = c * GLA_CHUNK
        rows = slice(r0, r0 + GLA_CHUNK)
        sums = _prefix_sums(pat, lg_ref[rows, :])
        bcum, bref, btot = sums[:GLA_CHUNK], sums[GLA_CHUNK:2 * GLA_CHUNK], sums[2 * GLA_CHUNK:]
        q = qk_ref[rows, 0:GLA_DK] * (GLA_HK ** -0.5)
        k = qk_ref[rows, GLA_DK:2 * GLA_DK]
        q_intra = (q * jnp.exp(bcum - bref)).astype(BF16)
        k_intra = (k * jnp.exp(bref - bcum)).astype(BF16)
        q_dec = (q * jnp.exp(bcum)).astype(BF16)
        k_state = (k * jnp.exp(btot - bcum)).astype(BF16)
        decay = jnp.exp(btot)
        for hd in range(GLA_HEADS):
            kc = slice(hd * GLA_HK, (hd + 1) * GLA_HK)
            v_h = v_ref[rows, hd * GLA_HV:(hd + 1) * GLA_HV]
            att = lax.dot_general(q_intra[:, kc], k_intra[:, kc], _NT, preferred_element_type=F32)
            att = jnp.where(causal, att, 0.0).astype(BF16)
            o_h = jnp.dot(att, v_h, preferred_element_type=F32)
            o_inter = []
            for sg in range(n_seg):
                srows = slice(sg * seg, (sg + 1) * seg)
                si = (r0 + sg * seg) // tl
                s_old = s_ref[0, si, hd]
                o_inter.append(jnp.dot(q_dec[srows, kc], s_old.astype(BF16), preferred_element_type=F32))
                upd = lax.dot_general(k_state[srows, kc], v_h[srows], _TN, preferred_element_type=F32)
                s_ref[0, si, hd] = s_old * _decay_cols(decay[sg * seg:sg * seg + 1, kc]) + upd
            finish_head(rows, hd, o_h + jnp.concatenate(o_inter, axis=0))
            yield


GLA_W_BLOCK = 1024
GLA_N_W_BLOCKS = (2 * GLA_DK + 2 * GLA_DV) // GLA_W_BLOCK


def _gla_weight_operands(gw, layer):
    specs = [_const_spec((D_MODEL, GLA_W_BLOCK), lead=(layer,), tail=(0, k)) for k in range(GLA_N_W_BLOCKS)]
    specs += [
        _const_spec((D_MODEL, GLA_GL_PAD), lead=(layer,)),
        _const_spec((GLA_GL_PAD, GLA_DK), lead=(layer,)),
        _const_spec((1, GLA_DK), lead=(layer,)),
        _const_spec((1, GLA_DV), lead=(layer,)),
        _const_spec((GLA_DV, D_MODEL), lead=(layer,)),
    ]
    arrays = [gw["w_in"]] * GLA_N_W_BLOCKS + [gw["w_gl"], gw["w_up"], gw["b_gate"], gw["norm_g"], gw["w_out"]]
    return specs, arrays


def _gla_layer(x, shift, scale, gate, g, gw, final_g, state, prev_out, *, layer, n_layers, tb, tl):
    b, L, _ = x.shape
    n_l = L // tl
    fresh = state is None
    has_prev = prev_out is not None
    final = final_g is not None
    m = tb * tl
    if fresh and tb == 1 and tl >= 2 * GLA_CHUNK:
        return _gla_layer_pipelined(x, shift, scale, gate, g, gw, final_g, prev_out,
                                    layer=layer, n_layers=n_layers, tl=tl)
    seq = lambda bi, li: (bi, li, 0)
    per_b = lambda bi, li: (bi, 0, 0)
    st_block = (1, tb, GLA_HEADS, GLA_HK, GLA_HV)
    st_map = lambda bi, li: (layer, bi, 0, 0, 0)
    w_specs, w_args = _gla_weight_operands(gw, layer)
    in_specs = [
        pl.BlockSpec((tb, tl, D_MODEL), seq),
        pl.BlockSpec((tb, 1, D_MODEL), per_b),
        pl.BlockSpec((tb, 1, D_MODEL), per_b),
        pl.BlockSpec((tb, 1, D_MODEL), per_b),
        _const_spec((1, D_MODEL)),
        *w_specs,
    ]
    args = [x, shift, scale, gate, g, *w_args]
    if final:
        in_specs.append(_const_spec((1, D_MODEL)))
        args.append(final_g)
    if not fresh:
        in_specs.append(pl.BlockSpec(st_block, st_map))
        args.append(state)
    aliases = {}
    if has_prev:
        in_specs.append(pl.BlockSpec(memory_space=pl.ANY))
        aliases = {len(args): 1}
        args.append(prev_out)
    kern = functools.partial(_gla_kernel, tb=tb, tl=tl, n_l=n_l, fresh=fresh, has_prev=has_prev, final=final)
    return pl.pallas_call(
        kern,
        grid=(b // tb, n_l),
        in_specs=in_specs,
        out_specs=[pl.BlockSpec((tb, tl, D_MODEL), seq), pl.BlockSpec(st_block, st_map)],
        out_shape=[
            jax.ShapeDtypeStruct(x.shape, F32),
            jax.ShapeDtypeStruct((n_layers, b, GLA_HEADS, GLA_HK, GLA_HV), F32),
        ],
        scratch_shapes=_gla_scratch(1, m),
        input_output_aliases=aliases,
        compiler_params=pltpu.CompilerParams(
            dimension_semantics=("arbitrary", "arbitrary"), vmem_limit_bytes=VMEM_LIMIT_BYTES),
        name=f"gla_layer{layer}_{'fresh' if fresh else 'cont'}",
    )(*args)


def _gla_scratch(n_slots, m):
    slot = [
        pltpu.VMEM((m, 2 * GLA_DK), F32),
        pltpu.VMEM((m, GLA_DV), BF16),
        pltpu.VMEM((m, GLA_DV), F32),
        pltpu.VMEM((m, GLA_DK), F32),
    ]
    return slot * n_slots + [pltpu.VMEM((m, GLA_DV), BF16)]


def _gla_layer_pipelined(x, shift, scale, gate, g, gw, final_g, prev_out, *, layer, n_layers, tl):
    b, L, _ = x.shape
    n_l = L // tl
    n_tiles = b * n_l
    has_prev = prev_out is not None
    final = final_g is not None
    t_x = lambda i: jnp.minimum(i, n_tiles - 1)
    t_y = lambda i: jnp.maximum(i - 1, 0)
    tile_x = lambda i: (t_x(i) // n_l, t_x(i) % n_l, 0)
    tile_y = lambda i: (t_y(i) // n_l, t_y(i) % n_l, 0)
    seq_x = lambda i: (t_x(i) // n_l, 0, 0)
    seq_y = lambda i: (t_y(i) // n_l, 0, 0)
    st_block = (1, 1, GLA_HEADS, GLA_HK, GLA_HV)
    w_specs, w_args = _gla_weight_operands(gw, layer)
    in_specs = [
        pl.BlockSpec((1, tl, D_MODEL), tile_x),
        pl.BlockSpec((1, tl, D_MODEL), tile_y),
        pl.BlockSpec((1, 1, D_MODEL), seq_x),
        pl.BlockSpec((1, 1, D_MODEL), seq_x),
        pl.BlockSpec((1, 1, D_MODEL), seq_y),
        _const_spec((1, D_MODEL)),
        *w_specs,
    ]
    args = [x, x, shift, scale, gate, g, *w_args]
    if final:
        in_specs.append(_const_spec((1, D_MODEL)))
        args.append(final_g)
    aliases = {}
    if has_prev:
        in_specs.append(pl.BlockSpec(memory_space=pl.ANY))
        aliases = {len(args): 1}
        args.append(prev_out)
    kern = functools.partial(_gla_pipe_kernel, tl=tl, n_l=n_l, has_prev=has_prev, final=final)
    return pl.pallas_call(
        kern,
        grid=(n_tiles + 1,),
        in_specs=in_specs,
        out_specs=[
            pl.BlockSpec((1, tl, D_MODEL), tile_y),
            pl.BlockSpec(st_block, lambda i: (layer, t_y(i) // n_l, 0, 0, 0)),
        ],
        out_shape=[
            jax.ShapeDtypeStruct(x.shape, F32),
            jax.ShapeDtypeStruct((n_layers, b, GLA_HEADS, GLA_HK, GLA_HV), F32),
        ],
        scratch_shapes=_gla_scratch(2, tl),
        input_output_aliases=aliases,
        compiler_params=pltpu.CompilerParams(
            dimension_semantics=("arbitrary",), vmem_limit_bytes=VMEM_LIMIT_BYTES),
        name=f"gla_layer{layer}_fresh",
    )(*args)


def _run_trunk(x, mods, pool_state, gla_state, pos0, w, *, tb_pool, tb_gla, tl):
    b = x.shape[0]
    n_pool, n_gla = (DEPTH + 1) // 2, DEPTH // 2
    new_pool = new_gla = None
    for li in range(DEPTH):
        shift, scale, gate = (mods[li, j].reshape(b, 1, D_MODEL) for j in range(3))
        j = li // 2
        if li % 2 == 0:
            x, new_pool = _pool_layer(
                x, shift, scale, gate, w["norm_g"][li], w["pool_w_in"], w["pool_w_grp"],
                w["pool_scale"], w["pool_w_out"], pool_state, new_pool,
                layer=j, n_layers=n_pool, tb=tb_pool, tl=tl, pos0=pos0)
        else:
            x, new_gla = _gla_layer(
                x, shift, scale, gate, w["norm_g"][li], w["gla"], w["final_g"] if li == DEPTH - 1 else None,
                gla_state, new_gla, layer=j, n_layers=n_gla, tb=tb_gla, tl=tl)
    return x, new_pool, new_gla


def kernel(x_prompt, x_sample, c_prompt, c_sample, state_pool, state_gla, ada_w, ada_b, norm_g, pool_w_in,
           pool_w_grp, pool_scale, pool_w_out, gla_w_in, gla_w_gate_up, gla_b_gate, gla_norm_g, gla_w_out,
           final_g):
    n_prompt = x_prompt.shape[0]
    n_gla = gla_w_in.shape[0]
    mods = _ada_mods(jnp.concatenate([c_prompt, c_sample], axis=0), ada_w, ada_b)

    o_gl = 2 * GLA_DK + 2 * GLA_DV
    assert gla_w_in.shape[-1] == o_gl + GLA_RANK
    w_gl_pad = jnp.concatenate(
        [gla_w_in[:, :, o_gl:], jnp.zeros((n_gla, D_MODEL, GLA_GL_PAD - GLA_RANK), gla_w_in.dtype)], axis=-1)
    w_up_pad = jnp.concatenate(
        [gla_w_gate_up, jnp.zeros((n_gla, GLA_GL_PAD - GLA_RANK, GLA_DK), gla_w_gate_up.dtype)], axis=1)
    w = {
        "norm_g": norm_g.reshape(DEPTH, 1, D_MODEL),
        "pool_w_in": pool_w_in.astype(BF16),
        "pool_w_grp": pool_w_grp.astype(BF16),
        "pool_scale": pool_scale.reshape(-1, 1, D_INNER),
        "pool_w_out": pool_w_out.astype(BF16),
        "gla": {
            "w_in": gla_w_in.astype(BF16),
            "w_gl": w_gl_pad.astype(BF16),
            "w_up": w_up_pad.astype(BF16),
            "b_gate": gla_b_gate.reshape(-1, 1, GLA_DK),
            "norm_g": gla_norm_g.reshape(-1, 1, GLA_DV),
            "w_out": gla_w_out.astype(BF16),
        },
        "final_g": final_g.reshape(1, D_MODEL),
    }
    y_p, pool_p, gla_p = _run_trunk(x_prompt, mods[:, :, :n_prompt], None, None, 0, w,
                                    tb_pool=1, tb_gla=1, tl=256)
    y_s, pool_s, gla_s = _run_trunk(x_sample, mods[:, :, n_prompt:], state_pool, state_gla, PAST_LEN, w,
                                    tb_pool=32, tb_gla=8, tl=x_sample.shape[1])
    return (y_p, y_s, pool_p, gla_p, pool_s, gla_s)
```

```python
import functools
import types

import jax
import jax.numpy as jnp
from jax import lax
from jax.experimental import pallas as pl
from jax.experimental.pallas import tpu as pltpu

D_MODEL = 1024
DEPTH = 4
PAST_LEN = 16384
D_INNER = 2048
POOL_WINDOWS = (2, 4, 8, 16)
POOL_GROUP = D_INNER // len(POOL_WINDOWS)
POOL_BUF = max(POOL_WINDOWS) - 1
POOL_HALO = POOL_BUF + 1
GLA_HEADS = 4
GLA_DK = 512
GLA_DV = D_INNER
GLA_HK = GLA_DK // GLA_HEADS
GLA_HV = GLA_DV // GLA_HEADS
GLA_RANK = 16
GLA_GATE_NORM = 16.0
GLA_CHUNK = 64
GLA_GL_PAD = 128
GLA_COL_PIECE = 512
GLA_OUT_PIECE = 256
GLA_W_BLOCK = 1024
GLA_N_W_BLOCKS = (2 * GLA_DK + 2 * GLA_DV) // GLA_W_BLOCK
EPS = 1e-6
F32 = jnp.float32
BF16 = jnp.bfloat16

VMEM_LIMIT_BYTES = 62 * 1024 * 1024

_NT = (((1,), (1,)), ((), ()))
_TN = (((0,), (0,)), ((), ()))


def _silu(v):
    return v / (1.0 + jnp.exp(-v))


def _mod_norm(x, g, shift, scale):
    ms = jnp.mean(x * x, axis=-1, keepdims=True)
    y = x * lax.rsqrt(ms + EPS) * g
    return y * (1.0 + scale) + shift


def _const_spec(shape, lead=(), tail=None):
    idx = tuple(lead) + tuple(tail if tail is not None else (0,) * len(shape))
    return pl.BlockSpec((None,) * len(lead) + tuple(shape), lambda *_: idx, pipeline_mode=pl.Buffered(1))


def _ada_kernel(c_ref, w_ref, b_ref, o_ref):
    cs = _silu(c_ref[...]).astype(BF16)
    o_ref[0, 0] = jnp.dot(cs, w_ref[0].astype(BF16), preferred_element_type=F32) + b_ref[0, 0]


def _ada_mods(c_all, ada_w, ada_b):
    nb = c_all.shape[0]
    b4 = ada_b.reshape(DEPTH, 3, 1, D_MODEL)
    return pl.pallas_call(
        _ada_kernel,
        grid=(DEPTH, 3),
        in_specs=[
            pl.BlockSpec((nb, D_MODEL), lambda li, j: (0, 0)),
            pl.BlockSpec((1, D_MODEL, D_MODEL), lambda li, j: (li, 0, j)),
            pl.BlockSpec((1, 1, 1, D_MODEL), lambda li, j: (li, j, 0, 0)),
        ],
        out_specs=pl.BlockSpec((1, 1, nb, D_MODEL), lambda li, j: (li, j, 0, 0)),
        out_shape=jax.ShapeDtypeStruct((DEPTH, 3, nb, D_MODEL), F32),
        name="ada_mods",
    )(c_all, ada_w, b4)


def _pool_kernel(*refs, tb, tl, n_l, pos0, fresh, has_prev):
    it = iter(refs)
    x_ref, sh_ref, sc_ref, gt_ref, g_ref = (next(it) for _ in range(5))
    win_ref, wgrp_ref, pscale_ref, wout_ref = (next(it) for _ in range(4))
    buf_ref = None if fresh else next(it)
    if has_prev:
        next(it)
    y_ref, nbuf_ref, ext_ref, gated_ref = (next(it) for _ in range(4))

    l = pl.program_id(1)
    m = tb * tl

    @pl.when(l == 0)
    def _():
        ext_ref[:, 0:POOL_HALO, :] = jnp.zeros((tb, POOL_HALO, D_INNER), F32)
        if not fresh:
            ext_ref[:, 1:POOL_HALO, :] = buf_ref[0]

    x = x_ref[...]
    h = _mod_norm(x, g_ref[...], sh_ref[...], sc_ref[...]).reshape(m, D_MODEL).astype(BF16)
    uz = jnp.dot(h, win_ref[...], preferred_element_type=F32)
    ext_ref[:, POOL_HALO:POOL_HALO + tl, :] = uz[:, :D_INNER].reshape(tb, tl, D_INNER)

    for gi, w in enumerate(POOL_WINDOWS):
        c0 = gi * POOL_GROUP
        u_g = ext_ref[:, POOL_HALO:POOL_HALO + tl, c0:c0 + POOL_GROUP]
        s = u_g
        for j in range(1, w):
            s = s + ext_ref[:, POOL_HALO - j:POOL_HALO - j + tl, c0:c0 + POOL_GROUP]
        if pos0 >= w - 1:
            mean = s * (1.0 / w)
        else:
            pos = pos0 + l * tl + lax.broadcasted_iota(jnp.int32, (tb, tl, POOL_GROUP), 1)
            mean = s / jnp.minimum(pos + 1, w).astype(F32)
        diff = (mean - u_g).reshape(m, POOL_GROUP).astype(BF16)
        mixed = jnp.dot(diff, wgrp_ref[gi], preferred_element_type=F32) * pscale_ref[:, c0:c0 + POOL_GROUP]
        z_g = uz[:, D_INNER + c0:D_INNER + c0 + POOL_GROUP]
        gated_ref[:, c0:c0 + POOL_GROUP] = (mixed * _silu(z_g)).astype(BF16)

    y = jnp.dot(gated_ref[...], wout_ref[...], preferred_element_type=F32)
    y_ref[...] = x + gt_ref[...] * y.reshape(tb, tl, D_MODEL)

    @pl.when(l == n_l - 1)
    def _():
        nbuf_ref[0] = ext_ref[:, tl + 1:tl + POOL_HALO, :]

    if n_l > 1:
        @pl.when(l < n_l - 1)
        def _():
            ext_ref[:, 0:POOL_HALO, :] = ext_ref[:, tl:tl + POOL_HALO, :]


def _pool_layer(x, shift, scale, gate, g, w_in, w_grp, pscale, w_out, state, prev_out, *, layer, n_layers,
                tb, tl, pos0):
    b, L, _ = x.shape
    n_l = L // tl
    fresh = state is None
    has_prev = prev_out is not None
    seq = lambda bi, li: (bi, li, 0)
    per_b = lambda bi, li: (bi, 0, 0)
    in_specs = [
        pl.BlockSpec((tb, tl, D_MODEL), seq),
        pl.BlockSpec((tb, 1, D_MODEL), per_b),
        pl.BlockSpec((tb, 1, D_MODEL), per_b),
        pl.BlockSpec((tb, 1, D_MODEL), per_b),
        _const_spec((1, D_MODEL)),
        _const_spec((D_MODEL, 2 * D_INNER), lead=(layer,)),
        _const_spec((len(POOL_WINDOWS), POOL_GROUP, POOL_GROUP), lead=(layer,)),
        _const_spec((1, D_INNER), lead=(layer,)),
        _const_spec((D_INNER, D_MODEL), lead=(layer,)),
    ]
    args = [x, shift, scale, gate, g, w_in, w_grp, pscale, w_out]
    if not fresh:
        in_specs.append(pl.BlockSpec((1, tb, POOL_BUF, D_INNER), lambda bi, li: (layer, bi, 0, 0)))
        args.append(state)
    aliases = {}
    if has_prev:
        in_specs.append(pl.BlockSpec(memory_space=pl.ANY))
        aliases = {len(args): 1}
        args.append(prev_out)
    kern = functools.partial(_pool_kernel, tb=tb, tl=tl, n_l=n_l, pos0=pos0, fresh=fresh, has_prev=has_prev)
    return pl.pallas_call(
        kern,
        grid=(b // tb, n_l),
        in_specs=in_specs,
        out_specs=[
            pl.BlockSpec((tb, tl, D_MODEL), seq),
            pl.BlockSpec((1, tb, POOL_BUF, D_INNER), lambda bi, li: (layer, bi, 0, 0)),
        ],
        out_shape=[
            jax.ShapeDtypeStruct(x.shape, F32),
            jax.ShapeDtypeStruct((n_layers, b, POOL_BUF, D_INNER), F32),
        ],
        scratch_shapes=[
            pltpu.VMEM((tb, POOL_HALO + tl, D_INNER), F32),
            pltpu.VMEM((tb * tl, D_INNER), BF16),
        ],
        input_output_aliases=aliases,
        compiler_params=pltpu.CompilerParams(
            dimension_semantics=("arbitrary", "arbitrary"), vmem_limit_bytes=VMEM_LIMIT_BYTES),
        name=f"pool_layer{layer}_{'fresh' if fresh else 'cont'}",
    )(*args)


def _gla_kernel(*refs, tb, tl, n_l, fresh, has_prev, final):
    e = types.SimpleNamespace(tb=tb, tl=tl, m=tb * tl)
    it = iter(refs)
    e.x, e.sh, e.sc, e.gt, e.g = (next(it) for _ in range(5))
    e.win = [next(it) for _ in range(GLA_N_W_BLOCKS)]
    e.wgl, e.wup, e.bg, e.ng, e.wout = (next(it) for _ in range(5))
    e.fg = next(it) if final else None
    s0_ref = None if fresh else next(it)
    if has_prev:
        next(it)
    e.y, e.s = next(it), next(it)
    e.qk, e.v, e.sz, e.lg, e.gated, e.acc = (next(it) for _ in range(6))
    e.long = tl >= 2 * GLA_CHUNK
    e.rb = 2 * GLA_CHUNK if e.long else GLA_CHUNK
    n_rb = e.m // e.rb

    @pl.when(pl.program_id(1) == 0)
    def _():
        if fresh:
            e.s[...] = jnp.zeros(e.s.shape, F32)
        else:
            e.s[...] = s0_ref[...]

    proj = _gla_project(e)
    mixes = [_gla_mix(e, r) for r in range(n_rb)]
    out = _gla_out(e)
    for _ in range(3):
        next(proj)
    for step in range(GLA_HEADS + 1):
        pending = 2 if step < GLA_HEADS else 0
        for mix in mixes:
            if pending:
                next(proj)
                pending -= 1
            next(mix)
        for _ in range(pending):
            next(proj)
        if step >= 1:
            next(out)


def _block_rows(e, r):
    return slice(r * e.rb, (r + 1) * e.rb)


def _gla_project(e):
    h = _mod_norm(e.x[...], e.g[...], e.sh[...], e.sc[...]).reshape(e.m, D_MODEL).astype(BF16)

    def piece(c0):
        blk, off = divmod(c0, GLA_W_BLOCK)
        return jnp.dot(h, e.win[blk][:, off:off + GLA_COL_PIECE], preferred_element_type=F32)

    o_v, o_z = 2 * GLA_DK, 2 * GLA_DK + GLA_DV
    for c0 in range(0, 2 * GLA_DK, GLA_COL_PIECE):
        e.qk[:, c0:c0 + GLA_COL_PIECE] = piece(c0)
        yield
    gl = jnp.dot(h, e.wgl[...], preferred_element_type=F32).astype(BF16)
    gate_pre = jnp.dot(gl, e.wup[...], preferred_element_type=F32) + e.bg[...]
    e.lg[...] = ((jnp.minimum(gate_pre, 0.0) - jnp.log(1.0 + jnp.exp(-jnp.abs(gate_pre))))
                 * (1.0 / GLA_GATE_NORM))
    yield
    for c0 in range(0, GLA_DV, GLA_COL_PIECE):
        e.v[:, c0:c0 + GLA_COL_PIECE] = piece(o_v + c0).astype(BF16)
        yield
        e.sz[:, c0:c0 + GLA_COL_PIECE] = _silu(piece(o_z + c0))
        yield


def _gla_out(e):
    assert GLA_COL_PIECE == GLA_HV
    for hd in range(GLA_HEADS):
        hc = slice(hd * GLA_HV, (hd + 1) * GLA_HV)
        part = jnp.dot(e.gated[:, hc], e.wout[hc, :], preferred_element_type=F32)
        if hd == 0:
            e.acc[...] = part
        elif hd < GLA_HEADS - 1:
            e.acc[...] += part
        else:
            out = e.x[...] + e.gt[...] * (e.acc[...] + part).reshape(e.tb, e.tl, D_MODEL)
            if e.fg is not None:
                out = out * lax.rsqrt(jnp.mean(out * out, axis=-1, keepdims=True) + EPS) * e.fg[...]
            e.y[...] = out
        yield


def _finish_head(e, rows, hd, o_h):
    o_h = o_h * lax.rsqrt(jnp.mean(o_h * o_h, axis=-1, keepdims=True) + EPS)
    hc = slice(hd * GLA_HV, (hd + 1) * GLA_HV)
    e.gated[rows, hc] = (o_h * e.ng[:, hc] * e.sz[rows, hc]).astype(BF16)


def _prefix_sums(pat, lg_rows):
    hi = lg_rows.astype(BF16)
    lo = (lg_rows - hi.astype(F32)).astype(BF16)
    return jnp.dot(pat, hi, preferred_element_type=F32) + jnp.dot(pat, lo, preferred_element_type=F32)


def _decay_cols(decay_row):
    d_col = jnp.transpose(jnp.broadcast_to(decay_row, (GLA_HK, GLA_HK)))
    return jnp.concatenate([d_col] * (GLA_HV // GLA_HK), axis=1)


def _gla_mix(e, r):
    return _gla_long_chunk(e, r) if e.long else _gla_short_chunk(e, r)


def _gla_long_chunk(e, r):
    hb = GLA_CHUNK
    cr = 2 * hb
    rows = _block_rows(e, r)
    si = (r * cr) // e.tl
    ri = lax.broadcasted_iota(jnp.int32, (cr, cr), 0)
    ci = lax.broadcasted_iota(jnp.int32, (cr, cr), 1)
    causal = ci <= ri
    tri = jnp.where(causal, 1.0, 0.0).astype(BF16)
    zeros = jnp.zeros((hb, GLA_HK), BF16)
    bcum = _prefix_sums(tri, e.lg[rows, :])
    b_a, b_b = bcum[:hb], bcum[hb:]
    ref_a, bnd = bcum[hb // 2 - 1:hb // 2], bcum[hb - 1:hb]
    ref_b, tot = bcum[hb + hb // 2 - 1:hb + hb // 2], bcum[cr - 1:cr]
    q = e.qk[rows, 0:GLA_DK] * (GLA_HK ** -0.5)
    k = e.qk[rows, GLA_DK:2 * GLA_DK]
    q_a, q_b, k_a, k_b = q[:hb], q[hb:], k[:hb], k[hb:]
    qi_a = (q_a * jnp.exp(b_a - ref_a)).astype(BF16)
    ki_a = (k_a * jnp.exp(ref_a - b_a)).astype(BF16)
    qi_b = (q_b * jnp.exp(b_b - ref_b)).astype(BF16)
    ki_b = (k_b * jnp.exp(ref_b - b_b)).astype(BF16)
    qx_b = (q_b * jnp.exp(b_b - bnd)).astype(BF16)
    kx_a = (k_a * jnp.exp(bnd - b_a)).astype(BF16)
    q_dec = (q * jnp.exp(bcum)).astype(BF16)
    k_state = (k * jnp.exp(tot - bcum)).astype(BF16)
    decay = jnp.exp(tot)
    yield
    for hd in range(GLA_HEADS):
        kc = slice(hd * GLA_HK, (hd + 1) * GLA_HK)
        v_h = e.v[rows, hd * GLA_HV:(hd + 1) * GLA_HV]
        s_old = e.s[0, si, hd]
        att_a = lax.dot_general(qi_a[:, kc], jnp.concatenate([ki_a[:, kc], zeros], axis=0), _NT,
                                preferred_element_type=F32)
        keys_b = jnp.concatenate([jnp.concatenate([kx_a[:, kc], zeros], axis=1),
                                  jnp.concatenate([zeros, ki_b[:, kc]], axis=1)], axis=0)
        att_b = lax.dot_general(jnp.concatenate([qx_b[:, kc], qi_b[:, kc]], axis=1), keys_b, _NT,
                                preferred_element_type=F32)
        att = jnp.where(causal, jnp.concatenate([att_a, att_b], axis=0), 0.0).astype(BF16)
        o_h = jnp.dot(jnp.concatenate([att, q_dec[:, kc]], axis=1),
                      jnp.concatenate([v_h, s_old.astype(BF16)], axis=0), preferred_element_type=F32)
        upd = lax.dot_general(k_state[:, kc], v_h, _TN, preferred_element_type=F32)
        e.s[0, si, hd] = s_old * _decay_cols(decay[:, kc]) + upd
        _finish_head(e, rows, hd, o_h)
        yield


def _gla_short_chunk(e, r):
    seg = e.tl
    n_seg = GLA_CHUNK // seg
    r0 = r * GLA_CHUNK
    rows = _block_rows(e, r)
    ri = lax.broadcasted_iota(jnp.int32, (GLA_CHUNK, GLA_CHUNK), 0)
    ci = lax.broadcasted_iota(jnp.int32, (GLA_CHUNK, GLA_CHUNK), 1)
    sh = seg.bit_length() - 1
    same = (ri >> sh) == (ci >> sh)
    causal = same & (ci <= ri)
    mid = same & ((ci & (seg - 1)) <= (seg // 2 - 1))
    pat = jnp.concatenate([jnp.where(mk, 1.0, 0.0).astype(BF16) for mk in (causal, mid, same)], axis=0)
    sums = _prefix_sums(pat, e.lg[rows, :])
    bcum, bref, btot = sums[:GLA_CHUNK], sums[GLA_CHUNK:2 * GLA_CHUNK], sums[2 * GLA_CHUNK:]
    q = e.qk[rows, 0:GLA_DK] * (GLA_HK ** -0.5)
    k = e.qk[rows, GLA_DK:2 * GLA_DK]
    q_intra = (q * jnp.exp(bcum - bref)).astype(BF16)
    k_intra = (k * jnp.exp(bref - bcum)).astype(BF16)
    q_dec = (q * jnp.exp(bcum)).astype(BF16)
    k_state = (k * jnp.exp(btot - bcum)).astype(BF16)
    decay = jnp.exp(btot)
    yield
    for hd in range(GLA_HEADS):
        kc = slice(hd * GLA_HK, (hd + 1) * GLA_HK)
        v_h = e.v[rows, hd * GLA_HV:(hd + 1) * GLA_HV]
        att = lax.dot_general(q_intra[:, kc], k_intra[:, kc], _NT, preferred_element_type=F32)
        att = jnp.where(causal, att, 0.0).astype(BF16)
        o_h = jnp.dot(att, v_h, preferred_element_type=F32)
        o_inter = []
        for sg in range(n_seg):
            srows = slice(sg * seg, (sg + 1) * seg)
            si = (r0 + sg * seg) // e.tl
            s_old = e.s[0, si, hd]
            o_inter.append(jnp.dot(q_dec[srows, kc], s_old.astype(BF16), preferred_element_type=F32))
            upd = lax.dot_general(k_state[srows, kc], v_h[srows], _TN, preferred_element_type=F32)
            e.s[0, si, hd] = s_old * _decay_cols(decay[sg * seg:sg * seg + 1, kc]) + upd
        _finish_head(e, rows, hd, o_h + jnp.concatenate(o_inter, axis=0))
        yield


def _gla_layer(x, shift, scale, gate, g, gw, final_g, state, prev_out, *, layer, n_layers, tb, tl):
    b, L, _ = x.shape
    n_l = L // tl
    fresh = state is None
    has_prev = prev_out is not None
    final = final_g is not None
    m = tb * tl
    seq = lambda bi, li: (bi, li, 0)
    per_b = lambda bi, li: (bi, 0, 0)
    st_block = (1, tb, GLA_HEADS, GLA_HK, GLA_HV)
    st_map = lambda bi, li: (layer, bi, 0, 0, 0)
    in_specs = [
        pl.BlockSpec((tb, tl, D_MODEL), seq),
        pl.BlockSpec((tb, 1, D_MODEL), per_b),
        pl.BlockSpec((tb, 1, D_MODEL), per_b),
        pl.BlockSpec((tb, 1, D_MODEL), per_b),
        _const_spec((1, D_MODEL)),
        *[_const_spec((D_MODEL, GLA_W_BLOCK), lead=(layer,), tail=(0, k)) for k in range(GLA_N_W_BLOCKS)],
        _const_spec((D_MODEL, GLA_GL_PAD), lead=(layer,)),
        _const_spec((GLA_GL_PAD, GLA_DK), lead=(layer,)),
        _const_spec((1, GLA_DK), lead=(layer,)),
        _const_spec((1, GLA_DV), lead=(layer,)),
        _const_spec((GLA_DV, D_MODEL), lead=(layer,)),
    ]
    args = [x, shift, scale, gate, g, *[gw["w_in"]] * GLA_N_W_BLOCKS, gw["w_gl"], gw["w_up"], gw["b_gate"],
            gw["norm_g"], gw["w_out"]]
    if final:
        in_specs.append(_const_spec((1, D_MODEL)))
        args.append(final_g)
    if not fresh:
        in_specs.append(pl.BlockSpec(st_block, st_map))
        args.append(state)
    aliases = {}
    if has_prev:
        in_specs.append(pl.BlockSpec(memory_space=pl.ANY))
        aliases = {len(args): 1}
        args.append(prev_out)
    kern = functools.partial(_gla_kernel, tb=tb, tl=tl, n_l=n_l, fresh=fresh, has_prev=has_prev, final=final)
    return pl.pallas_call(
        kern,
        grid=(b // tb, n_l),
        in_specs=in_specs,
        out_specs=[pl.BlockSpec((tb, tl, D_MODEL), seq), pl.BlockSpec(st_block, st_map)],
        out_shape=[
            jax.ShapeDtypeStruct(x.shape, F32),
            jax.ShapeDtypeStruct((n_layers, b, GLA_HEADS, GLA_HK, GLA_HV), F32),
        ],
        scratch_shapes=[
            pltpu.VMEM((m, 2 * GLA_DK), F32),
            pltpu.VMEM((m, GLA_DV), BF16),
            pltpu.VMEM((m, GLA_DV), F32),
            pltpu.VMEM((m, GLA_DK), F32),
            pltpu.VMEM((m, GLA_DV), BF16),
            pltpu.VMEM((m, D_MODEL), F32),
        ],
        input_output_aliases=aliases,
        compiler_params=pltpu.CompilerParams(
            dimension_semantics=("arbitrary", "arbitrary"), vmem_limit_bytes=VMEM_LIMIT_BYTES),
        name=f"gla_layer{layer}_{'fresh' if fresh else 'cont'}",
    )(*args)


def _run_trunk(x, mods, pool_state, gla_state, pos0, w, *, tb_pool, tb_gla, tl):
    b = x.shape[0]
    n_pool, n_gla = (DEPTH + 1) // 2, DEPTH // 2
    new_pool = new_gla = None
    for li in range(DEPTH):
        shift, scale, gate = (mods[li, j].reshape(b, 1, D_MODEL) for j in range(3))
        j = li // 2
        if li % 2 == 0:
            x, new_pool = _pool_layer(
                x, shift, scale, gate, w["norm_g"][li], w["pool_w_in"], w["pool_w_grp"],
                w["pool_scale"], w["pool_w_out"], pool_state, new_pool,
                layer=j, n_layers=n_pool, tb=tb_pool, tl=tl, pos0=pos0)
        else:
            x, new_gla = _gla_layer(
                x, shift, scale, gate, w["norm_g"][li], w["gla"], w["final_g"] if li == DEPTH - 1 else None,
                gla_state, new_gla, layer=j, n_layers=n_gla, tb=tb_gla, tl=tl)
    return x, new_pool, new_gla


def kernel(x_prompt, x_sample, c_prompt, c_sample, state_pool, state_gla, ada_w, ada_b, norm_g, pool_w_in,
           pool_w_grp, pool_scale, pool_w_out, gla_w_in, gla_w_gate_up, gla_b_gate, gla_norm_g, gla_w_out,
           final_g):
    n_prompt = x_prompt.shape[0]
    n_gla = gla_w_in.shape[0]
    mods = _ada_mods(jnp.concatenate([c_prompt, c_sample], axis=0), ada_w, ada_b)

    o_gl = 2 * GLA_DK + 2 * GLA_DV
    assert gla_w_in.shape[-1] == o_gl + GLA_RANK
    w_gl_pad = jnp.concatenate(
        [gla_w_in[:, :, o_gl:], jnp.zeros((n_gla, D_MODEL, GLA_GL_PAD - GLA_RANK), gla_w_in.dtype)], axis=-1)
    w_up_pad = jnp.concatenate(
        [gla_w_gate_up, jnp.zeros((n_gla, GLA_GL_PAD - GLA_RANK, GLA_DK), gla_w_gate_up.dtype)], axis=1)
    w = {
        "norm_g": norm_g.reshape(DEPTH, 1, D_MODEL),
        "pool_w_in": pool_w_in.astype(BF16),
        "pool_w_grp": pool_w_grp.astype(BF16),
        "pool_scale": pool_scale.reshape(-1, 1, D_INNER),
        "pool_w_out": pool_w_out.astype(BF16),
        "gla": {
            "w_in": gla_w_in.astype(BF16),
            "w_gl": w_gl_pad.astype(BF16),
            "w_up": w_up_pad.astype(BF16),
            "b_gate": gla_b_gate.reshape(-1, 1, GLA_DK),
            "norm_g": gla_norm_g.reshape(-1, 1, GLA_DV),
            "w_out": gla_w_out.astype(BF16),
        },
        "final_g": final_g.reshape(1, D_MODEL),
    }
    y_p, pool_p, gla_p = _run_trunk(x_prompt, mods[:, :, :n_prompt], None, None, 0, w,
                                    tb_pool=1, tb_gla=1, tl=256)
    y_s, pool_s, gla_s = _run_trunk(x_sample, mods[:, :, n_prompt:], state_pool, state_gla, PAST_LEN, w,
                                    tb_pool=32, tb_gla=8, tl=x_sample.shape[1])
    return (y_p, y_s, pool_p, gla_p, pool_s, gla_s)
```

```python
import functools

import jax
import jax.numpy as jnp
from jax import lax
from jax.experimental import pallas as pl
from jax.experimental.pallas import tpu as pltpu

D_MODEL = 1024
DEPTH = 4
PAST_LEN = 16384
D_INNER = 2048
POOL_WINDOWS = (2, 4, 8, 16)
POOL_GROUP = D_INNER // len(POOL_WINDOWS)
POOL_BUF = max(POOL_WINDOWS) - 1
POOL_HALO = POOL_BUF + 1
GLA_HEADS = 4
GLA_DK = 512
GLA_DV = D_INNER
GLA_HK = GLA_DK // GLA_HEADS
GLA_HV = GLA_DV // GLA_HEADS
GLA_RANK = 16
GLA_GATE_NORM = 16.0
GLA_CHUNK = 64
GLA_GL_PAD = 128
GLA_QKVZ = 2 * GLA_DK + 2 * GLA_DV
EPS = 1e-6
F32 = jnp.float32
BF16 = jnp.bfloat16

VMEM_LIMIT_BYTES = 62 * 1024 * 1024

_NT = (((1,), (1,)), ((), ()))
_TN = (((0,), (0,)), ((), ()))


def _silu(v):
    return v / (1.0 + jnp.exp(-v))


def _mod_norm(x, g, shift, scale):
    ms = jnp.mean(x * x, axis=-1, keepdims=True)
    y = x * lax.rsqrt(ms + EPS) * g
    return y * (1.0 + scale) + shift


def _const_spec(shape, lead=(), tail=None):
    idx = tuple(lead) + tuple(tail if tail is not None else (0,) * len(shape))
    return pl.BlockSpec((None,) * len(lead) + tuple(shape), lambda *_: idx, pipeline_mode=pl.Buffered(1))


def _ada_kernel(c_ref, w_ref, b_ref, o_ref):
    cs = _silu(c_ref[...]).astype(BF16)
    o_ref[0, 0] = jnp.dot(cs, w_ref[0].astype(BF16), preferred_element_type=F32) + b_ref[0, 0]


def _ada_mods(c_all, ada_w, ada_b):
    nb = c_all.shape[0]
    b4 = ada_b.reshape(DEPTH, 3, 1, D_MODEL)
    return pl.pallas_call(
        _ada_kernel,
        grid=(DEPTH, 3),
        in_specs=[
            pl.BlockSpec((nb, D_MODEL), lambda li, j: (0, 0)),
            pl.BlockSpec((1, D_MODEL, D_MODEL), lambda li, j: (li, 0, j)),
            pl.BlockSpec((1, 1, 1, D_MODEL), lambda li, j: (li, j, 0, 0)),
        ],
        out_specs=pl.BlockSpec((1, 1, nb, D_MODEL), lambda li, j: (li, j, 0, 0)),
        out_shape=jax.ShapeDtypeStruct((DEPTH, 3, nb, D_MODEL), F32),
        name="ada_mods",
    )(c_all, ada_w, b4)


def _pool_kernel(*refs, tb, tl, n_l, pos0, fresh, has_prev):
    it = iter(refs)
    x_ref, sh_ref, sc_ref, gt_ref, g_ref = (next(it) for _ in range(5))
    win_ref, wgrp_ref, pscale_ref, wout_ref = (next(it) for _ in range(4))
    buf_ref = None if fresh else next(it)
    if has_prev:
        next(it)
    y_ref, nbuf_ref, ext_ref, gated_ref = (next(it) for _ in range(4))

    l = pl.program_id(1)
    m = tb * tl

    @pl.when(l == 0)
    def _():
        ext_ref[:, 0:POOL_HALO, :] = jnp.zeros((tb, POOL_HALO, D_INNER), F32)
        if not fresh:
            ext_ref[:, 1:POOL_HALO, :] = buf_ref[0]

    x = x_ref[...]
    h = _mod_norm(x, g_ref[...], sh_ref[...], sc_ref[...]).reshape(m, D_MODEL).astype(BF16)
    uz = jnp.dot(h, win_ref[...], preferred_element_type=F32)
    ext_ref[:, POOL_HALO:POOL_HALO + tl, :] = uz[:, :D_INNER].reshape(tb, tl, D_INNER)

    for gi, w in enumerate(POOL_WINDOWS):
        c0 = gi * POOL_GROUP
        u_g = ext_ref[:, POOL_HALO:POOL_HALO + tl, c0:c0 + POOL_GROUP]
        s = u_g
        for j in range(1, w):
            s = s + ext_ref[:, POOL_HALO - j:POOL_HALO - j + tl, c0:c0 + POOL_GROUP]
        if pos0 >= w - 1:
            mean = s * (1.0 / w)
        else:
            pos = pos0 + l * tl + lax.broadcasted_iota(jnp.int32, (tb, tl, POOL_GROUP), 1)
            mean = s / jnp.minimum(pos + 1, w).astype(F32)
        diff = (mean - u_g).reshape(m, POOL_GROUP).astype(BF16)
        mixed = jnp.dot(diff, wgrp_ref[gi], preferred_element_type=F32) * pscale_ref[:, c0:c0 + POOL_GROUP]
        z_g = uz[:, D_INNER + c0:D_INNER + c0 + POOL_GROUP]
        gated_ref[:, c0:c0 + POOL_GROUP] = (mixed * _silu(z_g)).astype(BF16)

    y = jnp.dot(gated_ref[...], wout_ref[...], preferred_element_type=F32)
    y_ref[...] = x + gt_ref[...] * y.reshape(tb, tl, D_MODEL)

    @pl.when(l == n_l - 1)
    def _():
        nbuf_ref[0] = ext_ref[:, tl + 1:tl + POOL_HALO, :]

    if n_l > 1:
        @pl.when(l < n_l - 1)
        def _():
            ext_ref[:, 0:POOL_HALO, :] = ext_ref[:, tl:tl + POOL_HALO, :]


def _pool_layer(x, shift, scale, gate, g, w_in, w_grp, pscale, w_out, state, prev_out, *, layer, n_layers,
                tb, tl, pos0):
    b, L, _ = x.shape
    n_l = L // tl
    fresh = state is None
    has_prev = prev_out is not None
    seq = lambda bi, li: (bi, li, 0)
    per_b = lambda bi, li: (bi, 0, 0)
    in_specs = [
        pl.BlockSpec((tb, tl, D_MODEL), seq),
        pl.BlockSpec((tb, 1, D_MODEL), per_b),
        pl.BlockSpec((tb, 1, D_MODEL), per_b),
        pl.BlockSpec((tb, 1, D_MODEL), per_b),
        _const_spec((1, D_MODEL)),
        _const_spec((D_MODEL, 2 * D_INNER), lead=(layer,)),
        _const_spec((len(POOL_WINDOWS), POOL_GROUP, POOL_GROUP), lead=(layer,)),
        _const_spec((1, D_INNER), lead=(layer,)),
        _const_spec((D_INNER, D_MODEL), lead=(layer,)),
    ]
    args = [x, shift, scale, gate, g, w_in, w_grp, pscale, w_out]
    if not fresh:
        in_specs.append(pl.BlockSpec((1, tb, POOL_BUF, D_INNER), lambda bi, li: (layer, bi, 0, 0)))
        args.append(state)
    aliases = {}
    if has_prev:
        in_specs.append(pl.BlockSpec(memory_space=pl.ANY))
        aliases = {len(args): 1}
        args.append(prev_out)
    kern = functools.partial(_pool_kernel, tb=tb, tl=tl, n_l=n_l, pos0=pos0, fresh=fresh, has_prev=has_prev)
    return pl.pallas_call(
        kern,
        grid=(b // tb, n_l),
        in_specs=in_specs,
        out_specs=[
            pl.BlockSpec((tb, tl, D_MODEL), seq),
            pl.BlockSpec((1, tb, POOL_BUF, D_INNER), lambda bi, li: (layer, bi, 0, 0)),
        ],
        out_shape=[
            jax.ShapeDtypeStruct(x.shape, F32),
            jax.ShapeDtypeStruct((n_layers, b, POOL_BUF, D_INNER), F32),
        ],
        scratch_shapes=[
            pltpu.VMEM((tb, POOL_HALO + tl, D_INNER), F32),
            pltpu.VMEM((tb * tl, D_INNER), BF16),
        ],
        input_output_aliases=aliases,
        compiler_params=pltpu.CompilerParams(
            dimension_semantics=("arbitrary", "arbitrary"), vmem_limit_bytes=VMEM_LIMIT_BYTES),
        name=f"pool_layer{layer}_{'fresh' if fresh else 'cont'}",
    )(*args)


def _gla_kernel(*refs, tb, tl, n_l, fresh, has_prev, final):
    it = iter(refs)
    x_ref, sh_ref, sc_ref, gt_ref, g_ref = (next(it) for _ in range(5))
    win_ref, wgl_ref, wup_ref, bg_ref, ng_ref, wout_ref = (next(it) for _ in range(6))
    fg_ref = next(it) if final else None
    s0_ref = None if fresh else next(it)
    if has_prev:
        next(it)
    y_ref, s_ref, gated_ref = (next(it) for _ in range(3))

    l = pl.program_id(1)
    m = tb * tl

    @pl.when(l == 0)
    def _():
        if fresh:
            s_ref[...] = jnp.zeros(s_ref.shape, F32)
        else:
            s_ref[...] = s0_ref[...]

    x = x_ref[...]
    h = _mod_norm(x, g_ref[...], sh_ref[...], sc_ref[...]).reshape(m, D_MODEL).astype(BF16)
    proj = jnp.dot(h, win_ref[...], preferred_element_type=F32)
    o_z = 2 * GLA_DK + GLA_DV
    gl = jnp.dot(h, wgl_ref[...], preferred_element_type=F32).astype(BF16)
    gate_pre = jnp.dot(gl, wup_ref[...], preferred_element_type=F32) + bg_ref[...]
    lg = (jnp.minimum(gate_pre, 0.0) - jnp.log(1.0 + jnp.exp(-jnp.abs(gate_pre)))) * (1.0 / GLA_GATE_NORM)

    def finish_head(rows, hd, o_h):
        o_h = o_h * lax.rsqrt(jnp.mean(o_h * o_h, axis=-1, keepdims=True) + EPS)
        hc = slice(hd * GLA_HV, (hd + 1) * GLA_HV)
        z_h = proj[rows, o_z + hd * GLA_HV:o_z + (hd + 1) * GLA_HV]
        gated_ref[rows, hc] = (o_h * ng_ref[:, hc] * _silu(z_h)).astype(BF16)

    if tl >= 2 * GLA_CHUNK:
        _gla_long_chunks(proj, lg, s_ref, finish_head, m=m, tl=tl)
    else:
        _gla_short_chunks(proj, lg, s_ref, finish_head, m=m, tl=tl)

    y = jnp.dot(gated_ref[...], wout_ref[...], preferred_element_type=F32)
    out = x + gt_ref[...] * y.reshape(tb, tl, D_MODEL)
    if final:
        out = out * lax.rsqrt(jnp.mean(out * out, axis=-1, keepdims=True) + EPS) * fg_ref[...]
    y_ref[...] = out


def _prefix_sums(pat, lg_rows):
    hi = lg_rows.astype(BF16)
    lo = (lg_rows - hi.astype(F32)).astype(BF16)
    return jnp.dot(pat, hi, preferred_element_type=F32) + jnp.dot(pat, lo, preferred_element_type=F32)


def _decay_cols(decay_row):
    d_col = jnp.transpose(jnp.broadcast_to(decay_row, (GLA_HK, GLA_HK)))
    return jnp.concatenate([d_col] * (GLA_HV // GLA_HK), axis=1)


def _gla_long_chunks(proj, lg, s_ref, finish_head, *, m, tl):
    hb = GLA_CHUNK
    cr = 2 * hb
    o_v = 2 * GLA_DK
    ri = lax.broadcasted_iota(jnp.int32, (cr, cr), 0)
    ci = lax.broadcasted_iota(jnp.int32, (cr, cr), 1)
    causal = ci <= ri
    tri = jnp.where(causal, 1.0, 0.0).astype(BF16)
    zeros = jnp.zeros((hb, GLA_HK), BF16)
    for c in range(m // cr):
        rows = slice(c * cr, (c + 1) * cr)
        si = (c * cr) // tl
        bcum = _prefix_sums(tri, lg[rows])
        b_a, b_b = bcum[:hb], bcum[hb:]
        ref_a, bnd = bcum[hb // 2 - 1:hb // 2], bcum[hb - 1:hb]
        ref_b, tot = bcum[hb + hb // 2 - 1:hb + hb // 2], bcum[cr - 1:cr]
        q = proj[rows, 0:GLA_DK] * (GLA_HK ** -0.5)
        k = proj[rows, GLA_DK:2 * GLA_DK]
        q_a, q_b, k_a, k_b = q[:hb], q[hb:], k[:hb], k[hb:]
        qi_a = (q_a * jnp.exp(b_a - ref_a)).astype(BF16)
        ki_a = (k_a * jnp.exp(ref_a - b_a)).astype(BF16)
        qi_b = (q_b * jnp.exp(b_b - ref_b)).astype(BF16)
        ki_b = (k_b * jnp.exp(ref_b - b_b)).astype(BF16)
        qx_b = (q_b * jnp.exp(b_b - bnd)).astype(BF16)
        kx_a = (k_a * jnp.exp(bnd - b_a)).astype(BF16)
        q_dec = (q * jnp.exp(bcum)).astype(BF16)
        k_state = (k * jnp.exp(tot - bcum)).astype(BF16)
        decay = jnp.exp(tot)
        for hd in range(GLA_HEADS):
            kc = slice(hd * GLA_HK, (hd + 1) * GLA_HK)
            v_h = proj[rows, o_v + hd * GLA_HV:o_v + (hd + 1) * GLA_HV].astype(BF16)
            s_old = s_ref[0, si, hd]
            att_a = lax.dot_general(qi_a[:, kc], jnp.concatenate([ki_a[:, kc], zeros], axis=0), _NT,
                                    preferred_element_type=F32)
            keys_b = jnp.concatenate([jnp.concatenate([kx_a[:, kc], zeros], axis=1),
                                      jnp.concatenate([zeros, ki_b[:, kc]], axis=1)], axis=0)
            att_b = lax.dot_general(jnp.concatenate([qx_b[:, kc], qi_b[:, kc]], axis=1), keys_b, _NT,
                                    preferred_element_type=F32)
            att = jnp.where(causal, jnp.concatenate([att_a, att_b], axis=0), 0.0).astype(BF16)
            o_h = jnp.dot(jnp.concatenate([att, q_dec[:, kc]], axis=1),
                          jnp.concatenate([v_h, s_old.astype(BF16)], axis=0), preferred_element_type=F32)
            upd = lax.dot_general(k_state[:, kc], v_h, _TN, preferred_element_type=F32)
            s_ref[0, si, hd] = s_old * _decay_cols(decay[:, kc]) + upd
            finish_head(rows, hd, o_h)


def _gla_short_chunks(proj, lg, s_ref, finish_head, *, m, tl):
    seg = tl
    n_seg = GLA_CHUNK // seg
    o_v = 2 * GLA_DK
    ri = lax.broadcasted_iota(jnp.int32, (GLA_CHUNK, GLA_CHUNK), 0)
    ci = lax.broadcasted_iota(jnp.int32, (GLA_CHUNK, GLA_CHUNK), 1)
    sh = seg.bit_length() - 1
    same = (ri >> sh) == (ci >> sh)
    causal = same & (ci <= ri)
    mid = same & ((ci & (seg - 1)) <= (seg // 2 - 1))
    pat = jnp.concatenate([jnp.where(mk, 1.0, 0.0).astype(BF16) for mk in (causal, mid, same)], axis=0)

    for c in range(m // GLA_CHUNK):
        r0 = c * GLA_CHUNK
        rows = slice(r0, r0 + GLA_CHUNK)
        sums = _prefix_sums(pat, lg[rows])
        bcum, bref, btot = sums[:GLA_CHUNK], sums[GLA_CHUNK:2 * GLA_CHUNK], sums[2 * GLA_CHUNK:]
        q = proj[rows, 0:GLA_DK] * (GLA_HK ** -0.5)
        k = proj[rows, GLA_DK:2 * GLA_DK]
        q_intra = (q * jnp.exp(bcum - bref)).astype(BF16)
        k_intra = (k * jnp.exp(bref - bcum)).astype(BF16)
        q_dec = (q * jnp.exp(bcum)).astype(BF16)
        k_state = (k * jnp.exp(btot - bcum)).astype(BF16)
        decay = jnp.exp(btot)
        for hd in range(GLA_HEADS):
            kc = slice(hd * GLA_HK, (hd + 1) * GLA_HK)
            vc0 = o_v + hd * GLA_HV
            v_h = proj[rows, vc0:vc0 + GLA_HV].astype(BF16)
            att = lax.dot_general(q_intra[:, kc], k_intra[:, kc], _NT, preferred_element_type=F32)
            att = jnp.where(causal, att, 0.0).astype(BF16)
            o_h = jnp.dot(att, v_h, preferred_element_type=F32)
            o_inter = []
            for sg in range(n_seg):
                srows = slice(sg * seg, (sg + 1) * seg)
                si = (r0 + sg * seg) // tl
                s_old = s_ref[0, si, hd]
                o_inter.append(jnp.dot(q_dec[srows, kc], s_old.astype(BF16), preferred_element_type=F32))
                upd = lax.dot_general(k_state[srows, kc], v_h[srows], _TN, preferred_element_type=F32)
                s_ref[0, si, hd] = s_old * _decay_cols(decay[sg * seg:sg * seg + 1, kc]) + upd
            finish_head(rows, hd, o_h + jnp.concatenate(o_inter, axis=0))


def _gla_layer(x, shift, scale, gate, g, gw, final_g, state, prev_out, *, layer, n_layers, tb, tl):
    b, L, _ = x.shape
    n_l = L // tl
    fresh = state is None
    has_prev = prev_out is not None
    final = final_g is not None
    m = tb * tl
    seq = lambda bi, li: (bi, li, 0)
    per_b = lambda bi, li: (bi, 0, 0)
    st_block = (1, tb, GLA_HEADS, GLA_HK, GLA_HV)
    st_map = lambda bi, li: (layer, bi, 0, 0, 0)
    in_specs = [
        pl.BlockSpec((tb, tl, D_MODEL), seq),
        pl.BlockSpec((tb, 1, D_MODEL), per_b),
        pl.BlockSpec((tb, 1, D_MODEL), per_b),
        pl.BlockSpec((tb, 1, D_MODEL), per_b),
        _const_spec((1, D_MODEL)),
        _const_spec((D_MODEL, GLA_QKVZ), lead=(layer,)),
        _const_spec((D_MODEL, GLA_GL_PAD), lead=(layer,)),
        _const_spec((GLA_GL_PAD, GLA_DK), lead=(layer,)),
        _const_spec((1, GLA_DK), lead=(layer,)),
        _const_spec((1, GLA_DV), lead=(layer,)),
        _const_spec((GLA_DV, D_MODEL), lead=(layer,)),
    ]
    args = [x, shift, scale, gate, g, gw["w_in"], gw["w_gl"], gw["w_up"], gw["b_gate"], gw["norm_g"],
            gw["w_out"]]
    if final:
        in_specs.append(_const_spec((1, D_MODEL)))
        args.append(final_g)
    if not fresh:
        in_specs.append(pl.BlockSpec(st_block, st_map))
        args.append(state)
    aliases = {}
    if has_prev:
        in_specs.append(pl.BlockSpec(memory_space=pl.ANY))
        aliases = {len(args): 1}
        args.append(prev_out)
    kern = functools.partial(_gla_kernel, tb=tb, tl=tl, n_l=n_l, fresh=fresh, has_prev=has_prev, final=final)
    return pl.pallas_call(
        kern,
        grid=(b // tb, n_l),
        in_specs=in_specs,
        out_specs=[pl.BlockSpec((tb, tl, D_MODEL), seq), pl.BlockSpec(st_block, st_map)],
        out_shape=[
            jax.ShapeDtypeStruct(x.shape, F32),
            jax.ShapeDtypeStruct((n_layers, b, GLA_HEADS, GLA_HK, GLA_HV), F32),
        ],
        scratch_shapes=[pltpu.VMEM((m, GLA_DV), BF16)],
        input_output_aliases=aliases,
        compiler_params=pltpu.CompilerParams(
            dimension_semantics=("arbitrary", "arbitrary"), vmem_limit_bytes=VMEM_LIMIT_BYTES),
        name=f"gla_layer{layer}_{'fresh' if fresh else 'cont'}",
    )(*args)


def _run_trunk(x, mods, pool_state, gla_state, pos0, w, *, tb_pool, tl_pool, tb_gla, tl_gla):
    b = x.shape[0]
    n_pool, n_gla = (DEPTH + 1) // 2, DEPTH // 2
    new_pool = new_gla = None
    for li in range(DEPTH):
        shift, scale, gate = (mods[li, j].reshape(b, 1, D_MODEL) for j in range(3))
        j = li // 2
        if li % 2 == 0:
            x, new_pool = _pool_layer(
                x, shift, scale, gate, w["norm_g"][li], w["pool_w_in"], w["pool_w_grp"],
                w["pool_scale"], w["pool_w_out"], pool_state, new_pool,
                layer=j, n_layers=n_pool, tb=tb_pool, tl=tl_pool, pos0=pos0)
        else:
            x, new_gla = _gla_layer(
                x, shift, scale, gate, w["norm_g"][li], w["gla"], w["final_g"] if li == DEPTH - 1 else None,
                gla_state, new_gla, layer=j, n_layers=n_gla, tb=tb_gla, tl=tl_gla)
    return x, new_pool, new_gla


def kernel(x_prompt, x_sample, c_prompt, c_sample, state_pool, state_gla, ada_w, ada_b, norm_g, pool_w_in,
           pool_w_grp, pool_scale, pool_w_out, gla_w_in, gla_w_gate_up, gla_b_gate, gla_norm_g, gla_w_out,
           final_g):
    n_prompt = x_prompt.shape[0]
    n_gla = gla_w_in.shape[0]
    mods = _ada_mods(jnp.concatenate([c_prompt, c_sample], axis=0), ada_w, ada_b)

    assert gla_w_in.shape[-1] == GLA_QKVZ + GLA_RANK
    w_gl_pad = jnp.concatenate(
        [gla_w_in[:, :, GLA_QKVZ:], jnp.zeros((n_gla, D_MODEL, GLA_GL_PAD - GLA_RANK), gla_w_in.dtype)], axis=-1)
    w_up_pad = jnp.concatenate(
        [gla_w_gate_up, jnp.zeros((n_gla, GLA_GL_PAD - GLA_RANK, GLA_DK), gla_w_gate_up.dtype)], axis=1)
    w = {
        "norm_g": norm_g.reshape(DEPTH, 1, D_MODEL),
        "pool_w_in": pool_w_in.astype(BF16),
        "pool_w_grp": pool_w_grp.astype(BF16),
        "pool_scale": pool_scale.reshape(-1, 1, D_INNER),
        "pool_w_out": pool_w_out.astype(BF16),
        "gla": {
            "w_in": gla_w_in.astype(BF16),
            "w_gl": w_gl_pad.astype(BF16),
            "w_up": w_up_pad.astype(BF16),
            "b_gate": gla_b_gate.reshape(-1, 1, GLA_DK),
            "norm_g": gla_norm_g.reshape(-1, 1, GLA_DV),
            "w_out": gla_w_out.astype(BF16),
        },
        "final_g": final_g.reshape(1, D_MODEL),
    }
    y_p, pool_p, gla_p = _run_trunk(x_prompt, mods[:, :, :n_prompt], None, None, 0, w,
                                    tb_pool=1, tl_pool=512, tb_gla=1, tl_gla=256)
    y_s, pool_s, gla_s = _run_trunk(x_sample, mods[:, :, n_prompt:], state_pool, state_gla, PAST_LEN, w,
                                    tb_pool=32, tl_pool=x_sample.shape[1], tb_gla=8, tl_gla=x_sample.shape[1])
    return (y_p, y_s, pool_p, gla_p, pool_s, gla_s)
```

```python
import functools

import jax
import jax.numpy as jnp
from jax import lax
from jax.experimental import pallas as pl
from jax.experimental.pallas import tpu as pltpu

D_MODEL = 1024
DEPTH = 4
PAST_LEN = 16384
D_INNER = 2048
POOL_WINDOWS = (2, 4, 8, 16)
POOL_GROUP = D_INNER // len(POOL_WINDOWS)
POOL_BUF = max(POOL_WINDOWS) - 1
POOL_HALO = POOL_BUF + 1
GLA_HEADS = 4
GLA_DK = 512
GLA_DV = D_INNER
GLA_HK = GLA_DK // GLA_HEADS
GLA_HV = GLA_DV // GLA_HEADS
GLA_RANK = 16
GLA_GATE_NORM = 16.0
GLA_CHUNK = 64
GLA_GL_PAD = 128
GLA_QKVZ = 2 * GLA_DK + 2 * GLA_DV
EPS = 1e-6
F32 = jnp.float32
BF16 = jnp.bfloat16

VMEM_LIMIT_BYTES = 62 * 1024 * 1024

_NT = (((1,), (1,)), ((), ()))
_TN = (((0,), (0,)), ((), ()))


def _silu(v):
    return v / (1.0 + jnp.exp(-v))


def _mod_norm(x, g, shift, scale):
    ms = jnp.mean(x * x, axis=-1, keepdims=True)
    y = x * lax.rsqrt(ms + EPS) * g
    return y * (1.0 + scale) + shift


def _const_spec(shape, lead=(), tail=None):
    idx = tuple(lead) + tuple(tail if tail is not None else (0,) * len(shape))
    return pl.BlockSpec((None,) * len(lead) + tuple(shape), lambda *_: idx, pipeline_mode=pl.Buffered(1))


def _ada_kernel(c_ref, w_ref, b_ref, o_ref):
    cs = _silu(c_ref[...]).astype(BF16)
    o_ref[0, 0] = jnp.dot(cs, w_ref[0].astype(BF16), preferred_element_type=F32) + b_ref[0, 0]


def _ada_mods(c_all, ada_w, ada_b):
    nb = c_all.shape[0]
    b4 = ada_b.reshape(DEPTH, 3, 1, D_MODEL)
    return pl.pallas_call(
        _ada_kernel,
        grid=(DEPTH, 3),
        in_specs=[
            pl.BlockSpec((nb, D_MODEL), lambda li, j: (0, 0)),
            pl.BlockSpec((1, D_MODEL, D_MODEL), lambda li, j: (li, 0, j)),
            pl.BlockSpec((1, 1, 1, D_MODEL), lambda li, j: (li, j, 0, 0)),
        ],
        out_specs=pl.BlockSpec((1, 1, nb, D_MODEL), lambda li, j: (li, j, 0, 0)),
        out_shape=jax.ShapeDtypeStruct((DEPTH, 3, nb, D_MODEL), F32),
        name="ada_mods",
    )(c_all, ada_w, b4)


def _pool_kernel(*refs, tb, tl, n_l, pos0, fresh, has_prev):
    it = iter(refs)
    x_ref, sh_ref, sc_ref, gt_ref, g_ref = (next(it) for _ in range(5))
    win_ref, wgrp_ref, pscale_ref, wout_ref = (next(it) for _ in range(4))
    buf_ref = None if fresh else next(it)
    if has_prev:
        next(it)
    y_ref, nbuf_ref, ext_ref, gated_ref = (next(it) for _ in range(4))

    l = pl.program_id(1)
    m = tb * tl

    @pl.when(l == 0)
    def _():
        ext_ref[:, 0:POOL_HALO, :] = jnp.zeros((tb, POOL_HALO, D_INNER), F32)
        if not fresh:
            ext_ref[:, 1:POOL_HALO, :] = buf_ref[0]

    x = x_ref[...]
    h = _mod_norm(x, g_ref[...], sh_ref[...], sc_ref[...]).reshape(m, D_MODEL).astype(BF16)
    uz = jnp.dot(h, win_ref[...], preferred_element_type=F32)
    ext_ref[:, POOL_HALO:POOL_HALO + tl, :] = uz[:, :D_INNER].reshape(tb, tl, D_INNER)

    for gi, w in enumerate(POOL_WINDOWS):
        c0 = gi * POOL_GROUP
        u_g = ext_ref[:, POOL_HALO:POOL_HALO + tl, c0:c0 + POOL_GROUP]
        s = u_g
        for j in range(1, w):
            s = s + ext_ref[:, POOL_HALO - j:POOL_HALO - j + tl, c0:c0 + POOL_GROUP]
        if pos0 >= w - 1:
            mean = s * (1.0 / w)
        else:
            pos = pos0 + l * tl + lax.broadcasted_iota(jnp.int32, (tb, tl, POOL_GROUP), 1)
            mean = s / jnp.minimum(pos + 1, w).astype(F32)
        diff = (mean - u_g).reshape(m, POOL_GROUP).astype(BF16)
        mixed = jnp.dot(diff, wgrp_ref[gi], preferred_element_type=F32) * pscale_ref[:, c0:c0 + POOL_GROUP]
        z_g = uz[:, D_INNER + c0:D_INNER + c0 + POOL_GROUP]
        gated_ref[:, c0:c0 + POOL_GROUP] = (mixed * _silu(z_g)).astype(BF16)

    y = jnp.dot(gated_ref[...], wout_ref[...], preferred_element_type=F32)
    y_ref[...] = x + gt_ref[...] * y.reshape(tb, tl, D_MODEL)

    @pl.when(l == n_l - 1)
    def _():
        nbuf_ref[0] = ext_ref[:, tl + 1:tl + POOL_HALO, :]

    if n_l > 1:
        @pl.when(l < n_l - 1)
        def _():
            ext_ref[:, 0:POOL_HALO, :] = ext_ref[:, tl:tl + POOL_HALO, :]


def _pool_layer(x, shift, scale, gate, g, w_in, w_grp, pscale, w_out, state, prev_out, *, layer, n_layers,
                tb, tl, pos0):
    b, L, _ = x.shape
    n_l = L // tl
    fresh = state is None
    has_prev = prev_out is not None
    seq = lambda bi, li: (bi, li, 0)
    per_b = lambda bi, li: (bi, 0, 0)
    in_specs = [
        pl.BlockSpec((tb, tl, D_MODEL), seq),
        pl.BlockSpec((tb, 1, D_MODEL), per_b),
        pl.BlockSpec((tb, 1, D_MODEL), per_b),
        pl.BlockSpec((tb, 1, D_MODEL), per_b),
        _const_spec((1, D_MODEL)),
        _const_spec((D_MODEL, 2 * D_INNER), lead=(layer,)),
        _const_spec((len(POOL_WINDOWS), POOL_GROUP, POOL_GROUP), lead=(layer,)),
        _const_spec((1, D_INNER), lead=(layer,)),
        _const_spec((D_INNER, D_MODEL), lead=(layer,)),
    ]
    args = [x, shift, scale, gate, g, w_in, w_grp, pscale, w_out]
    if not fresh:
        in_specs.append(pl.BlockSpec((1, tb, POOL_BUF, D_INNER), lambda bi, li: (layer, bi, 0, 0)))
        args.append(state)
    aliases = {}
    if has_prev:
        in_specs.append(pl.BlockSpec(memory_space=pl.ANY))
        aliases = {len(args): 1}
        args.append(prev_out)
    kern = functools.partial(_pool_kernel, tb=tb, tl=tl, n_l=n_l, pos0=pos0, fresh=fresh, has_prev=has_prev)
    return pl.pallas_call(
        kern,
        grid=(b // tb, n_l),
        in_specs=in_specs,
        out_specs=[
            pl.BlockSpec((tb, tl, D_MODEL), seq),
            pl.BlockSpec((1, tb, POOL_BUF, D_INNER), lambda bi, li: (layer, bi, 0, 0)),
        ],
        out_shape=[
            jax.ShapeDtypeStruct(x.shape, F32),
            jax.ShapeDtypeStruct((n_layers, b, POOL_BUF, D_INNER), F32),
        ],
        scratch_shapes=[
            pltpu.VMEM((tb, POOL_HALO + tl, D_INNER), F32),
            pltpu.VMEM((tb * tl, D_INNER), BF16),
        ],
        input_output_aliases=aliases,
        compiler_params=pltpu.CompilerParams(
            dimension_semantics=("arbitrary", "arbitrary"), vmem_limit_bytes=VMEM_LIMIT_BYTES),
        name=f"pool_layer{layer}_{'fresh' if fresh else 'cont'}",
    )(*args)


def _gla_kernel(*refs, tb, tl, n_l, fresh, has_prev, final):
    it = iter(refs)
    x_ref, sh_ref, sc_ref, gt_ref, g_ref = (next(it) for _ in range(5))
    win_ref, wgl_ref, wup_ref, bg_ref, ng_ref, wout_ref = (next(it) for _ in range(6))
    fg_ref = next(it) if final else None
    s0_ref = None if fresh else next(it)
    if has_prev:
        next(it)
    y_ref, s_ref, gated_ref = (next(it) for _ in range(3))

    l = pl.program_id(1)
    m = tb * tl

    @pl.when(l == 0)
    def _():
        if fresh:
            s_ref[...] = jnp.zeros(s_ref.shape, F32)
        else:
            s_ref[...] = s0_ref[...]

    x = x_ref[...]
    h = _mod_norm(x, g_ref[...], sh_ref[...], sc_ref[...]).reshape(m, D_MODEL).astype(BF16)
    proj = jnp.dot(h, win_ref[...], preferred_element_type=F32)
    o_z = 2 * GLA_DK + GLA_DV
    gl = jnp.dot(h, wgl_ref[...], preferred_element_type=F32).astype(BF16)
    gate_pre = jnp.dot(gl, wup_ref[...], preferred_element_type=F32) + bg_ref[...]
    lg = (jnp.minimum(gate_pre, 0.0) - jnp.log(1.0 + jnp.exp(-jnp.abs(gate_pre)))) * (1.0 / GLA_GATE_NORM)

    def finish_head(rows, hd, o_h):
        o_h = o_h * lax.rsqrt(jnp.mean(o_h * o_h, axis=-1, keepdims=True) + EPS)
        hc = slice(hd * GLA_HV, (hd + 1) * GLA_HV)
        z_h = proj[rows, o_z + hd * GLA_HV:o_z + (hd + 1) * GLA_HV]
        gated_ref[rows, hc] = (o_h * ng_ref[:, hc] * _silu(z_h)).astype(BF16)

    if tl >= 2 * GLA_CHUNK:
        _gla_long_chunks(proj, lg, s_ref, finish_head, m=m, tl=tl)
    else:
        _gla_short_chunks(proj, lg, s_ref, finish_head, m=m, tl=tl)

    y = jnp.dot(gated_ref[...], wout_ref[...], preferred_element_type=F32)
    out = x + gt_ref[...] * y.reshape(tb, tl, D_MODEL)
    if final:
        out = out * lax.rsqrt(jnp.mean(out * out, axis=-1, keepdims=True) + EPS) * fg_ref[...]
    y_ref[...] = out


def _prefix_sums(pat, lg_rows):
    hi = lg_rows.astype(BF16)
    lo = (lg_rows - hi.astype(F32)).astype(BF16)
    return jnp.dot(pat, hi, preferred_element_type=F32) + jnp.dot(pat, lo, preferred_element_type=F32)


def _decay_cols(decay_row):
    d_col = jnp.transpose(jnp.broadcast_to(decay_row, (GLA_HK, GLA_HK)))
    return jnp.concatenate([d_col] * (GLA_HV // GLA_HK), axis=1)


def _gla_long_chunks(proj, lg, s_ref, finish_head, *, m, tl):
    hb = GLA_CHUNK
    cr = 2 * hb
    o_v = 2 * GLA_DK
    ri = lax.broadcasted_iota(jnp.int32, (cr, cr), 0)
    ci = lax.broadcasted_iota(jnp.int32, (cr, cr), 1)
    causal = ci <= ri
    tri = jnp.where(causal, 1.0, 0.0).astype(BF16)
    zeros = jnp.zeros((hb, GLA_HK), BF16)
    heads = range(GLA_HEADS)
    kcs = [slice(hd * GLA_HK, (hd + 1) * GLA_HK) for hd in heads]

    def prepare(c):
        rows = slice(c * cr, (c + 1) * cr)
        bcum = _prefix_sums(tri, lg[rows])
        b_a, b_b = bcum[:hb], bcum[hb:]
        ref_a, bnd = bcum[hb // 2 - 1:hb // 2], bcum[hb - 1:hb]
        ref_b, tot = bcum[hb + hb // 2 - 1:hb + hb // 2], bcum[cr - 1:cr]
        q = proj[rows, 0:GLA_DK] * (GLA_HK ** -0.5)
        k = proj[rows, GLA_DK:2 * GLA_DK]
        q_a, q_b, k_a, k_b = q[:hb], q[hb:], k[:hb], k[hb:]
        qi_a = (q_a * jnp.exp(b_a - ref_a)).astype(BF16)
        ki_a = (k_a * jnp.exp(ref_a - b_a)).astype(BF16)
        qi_b = (q_b * jnp.exp(b_b - ref_b)).astype(BF16)
        ki_b = (k_b * jnp.exp(ref_b - b_b)).astype(BF16)
        qx_b = (q_b * jnp.exp(b_b - bnd)).astype(BF16)
        kx_a = (k_a * jnp.exp(bnd - b_a)).astype(BF16)
        q_dec = (q * jnp.exp(bcum)).astype(BF16)
        k_state = (k * jnp.exp(tot - bcum)).astype(BF16)
        decay = jnp.exp(tot)
        return qi_a, ki_a, qi_b, ki_b, qx_b, kx_a, q_dec, k_state, decay

    prepared = [prepare(c) for c in range(m // cr)]
    for c, (qi_a, ki_a, qi_b, ki_b, qx_b, kx_a, q_dec, k_state, decay) in enumerate(prepared):
        rows = slice(c * cr, (c + 1) * cr)
        si = (c * cr) // tl
        v = [proj[rows, o_v + hd * GLA_HV:o_v + (hd + 1) * GLA_HV].astype(BF16) for hd in heads]
        s_old = [s_ref[0, si, hd] for hd in heads]
        att_a = [lax.dot_general(qi_a[:, kc], jnp.concatenate([ki_a[:, kc], zeros], axis=0), _NT,
                                 preferred_element_type=F32) for kc in kcs]
        att_b = [lax.dot_general(
            jnp.concatenate([qx_b[:, kc], qi_b[:, kc]], axis=1),
            jnp.concatenate([jnp.concatenate([kx_a[:, kc], zeros], axis=1),
                             jnp.concatenate([zeros, ki_b[:, kc]], axis=1)], axis=0),
            _NT, preferred_element_type=F32) for kc in kcs]
        upd = [lax.dot_general(k_state[:, kc], v[hd], _TN, preferred_element_type=F32)
               for hd, kc in enumerate(kcs)]
        att = [jnp.where(causal, jnp.concatenate([att_a[hd], att_b[hd]], axis=0), 0.0).astype(BF16)
               for hd in heads]
        o = [jnp.dot(jnp.concatenate([att[hd], q_dec[:, kc]], axis=1),
                     jnp.concatenate([v[hd], s_old[hd].astype(BF16)], axis=0), preferred_element_type=F32)
             for hd, kc in enumerate(kcs)]
        for hd, kc in enumerate(kcs):
            s_ref[0, si, hd] = s_old[hd] * _decay_cols(decay[:, kc]) + upd[hd]
        for hd in heads:
            finish_head(rows, hd, o[hd])


def _gla_short_chunks(proj, lg, s_ref, finish_head, *, m, tl):
    seg = tl
    n_seg = GLA_CHUNK // seg
    o_v = 2 * GLA_DK
    ri = lax.broadcasted_iota(jnp.int32, (GLA_CHUNK, GLA_CHUNK), 0)
    ci = lax.broadcasted_iota(jnp.int32, (GLA_CHUNK, GLA_CHUNK), 1)
    sh = seg.bit_length() - 1
    same = (ri >> sh) == (ci >> sh)
    causal = same & (ci <= ri)
    mid = same & ((ci & (seg - 1)) <= (seg // 2 - 1))
    pat = jnp.concatenate([jnp.where(mk, 1.0, 0.0).astype(BF16) for mk in (causal, mid, same)], axis=0)

    for c in range(m // GLA_CHUNK):
        r0 = c * GLA_CHUNK
        rows = slice(r0, r0 + GLA_CHUNK)
        sums = _prefix_sums(pat, lg[rows])
        bcum, bref, btot = sums[:GLA_CHUNK], sums[GLA_CHUNK:2 * GLA_CHUNK], sums[2 * GLA_CHUNK:]
        q = proj[rows, 0:GLA_DK] * (GLA_HK ** -0.5)
        k = proj[rows, GLA_DK:2 * GLA_DK]
        q_intra = (q * jnp.exp(bcum - bref)).astype(BF16)
        k_intra = (k * jnp.exp(bref - bcum)).astype(BF16)
        q_dec = (q * jnp.exp(bcum)).astype(BF16)
        k_state = (k * jnp.exp(btot - bcum)).astype(BF16)
        decay = jnp.exp(btot)
        for hd in range(GLA_HEADS):
            kc = slice(hd * GLA_HK, (hd + 1) * GLA_HK)
            vc0 = o_v + hd * GLA_HV
            v_h = proj[rows, vc0:vc0 + GLA_HV].astype(BF16)
            att = lax.dot_general(q_intra[:, kc], k_intra[:, kc], _NT, preferred_element_type=F32)
            att = jnp.where(causal, att, 0.0).astype(BF16)
            o_h = jnp.dot(att, v_h, preferred_element_type=F32)
            o_inter = []
            for sg in range(n_seg):
                srows = slice(sg * seg, (sg + 1) * seg)
                si = (r0 + sg * seg) // tl
                s_old = s_ref[0, si, hd]
                o_inter.append(jnp.dot(q_dec[srows, kc], s_old.astype(BF16), preferred_element_type=F32))
                upd = lax.dot_general(k_state[srows, kc], v_h[srows], _TN, preferred_element_type=F32)
                s_ref[0, si, hd] = s_old * _decay_cols(decay[sg * seg:sg * seg + 1, kc]) + upd
            finish_head(rows, hd, o_h + jnp.concatenate(o_inter, axis=0))


def _gla_layer(x, shift, scale, gate, g, gw, final_g, state, prev_out, *, layer, n_layers, tb, tl):
    b, L, _ = x.shape
    n_l = L // tl
    fresh = state is None
    has_prev = prev_out is not None
    final = final_g is not None
    m = tb * tl
    seq = lambda bi, li: (bi, li, 0)
    per_b = lambda bi, li: (bi, 0, 0)
    st_block = (1, tb, GLA_HEADS, GLA_HK, GLA_HV)
    st_map = lambda bi, li: (layer, bi, 0, 0, 0)
    in_specs = [
        pl.BlockSpec((tb, tl, D_MODEL), seq),
        pl.BlockSpec((tb, 1, D_MODEL), per_b),
        pl.BlockSpec((tb, 1, D_MODEL), per_b),
        pl.BlockSpec((tb, 1, D_MODEL), per_b),
        _const_spec((1, D_MODEL)),
        _const_spec((D_MODEL, GLA_QKVZ), lead=(layer,)),
        _const_spec((D_MODEL, GLA_GL_PAD), lead=(layer,)),
        _const_spec((GLA_GL_PAD, GLA_DK), lead=(layer,)),
        _const_spec((1, GLA_DK), lead=(layer,)),
        _const_spec((1, GLA_DV), lead=(layer,)),
        _const_spec((GLA_DV, D_MODEL), lead=(layer,)),
    ]
    args = [x, shift, scale, gate, g, gw["w_in"], gw["w_gl"], gw["w_up"], gw["b_gate"], gw["norm_g"],
            gw["w_out"]]
    if final:
        in_specs.append(_const_spec((1, D_MODEL)))
        args.append(final_g)
    if not fresh:
        in_specs.append(pl.BlockSpec(st_block, st_map))
        args.append(state)
    aliases = {}
    if has_prev:
        in_specs.append(pl.BlockSpec(memory_space=pl.ANY))
        aliases = {len(args): 1}
        args.append(prev_out)
    kern = functools.partial(_gla_kernel, tb=tb, tl=tl, n_l=n_l, fresh=fresh, has_prev=has_prev, final=final)
    return pl.pallas_call(
        kern,
        grid=(b // tb, n_l),
        in_specs=in_specs,
        out_specs=[pl.BlockSpec((tb, tl, D_MODEL), seq), pl.BlockSpec(st_block, st_map)],
        out_shape=[
            jax.ShapeDtypeStruct(x.shape, F32),
            jax.ShapeDtypeStruct((n_layers, b, GLA_HEADS, GLA_HK, GLA_HV), F32),
        ],
        scratch_shapes=[pltpu.VMEM((m, GLA_DV), BF16)],
        input_output_aliases=aliases,
        compiler_params=pltpu.CompilerParams(
            dimension_semantics=("arbitrary", "arbitrary"), vmem_limit_bytes=VMEM_LIMIT_BYTES),
        name=f"gla_layer{layer}_{'fresh' if fresh else 'cont'}",
    )(*args)


def _run_trunk(x, mods, pool_state, gla_state, pos0, w, *, tb_pool, tl_pool, tb_gla, tl_gla):
    b = x.shape[0]
    n_pool, n_gla = (DEPTH + 1) // 2, DEPTH // 2
    new_pool = new_gla = None
    for li in range(DEPTH):
        shift, scale, gate = (mods[li, j].reshape(b, 1, D_MODEL) for j in range(3))
        j = li // 2
        if li % 2 == 0:
            x, new_pool = _pool_layer(
                x, shift, scale, gate, w["norm_g"][li], w["pool_w_in"], w["pool_w_grp"],
                w["pool_scale"], w["pool_w_out"], pool_state, new_pool,
                layer=j, n_layers=n_pool, tb=tb_pool, tl=tl_pool, pos0=pos0)
        else:
            x, new_gla = _gla_layer(
                x, shift, scale, gate, w["norm_g"][li], w["gla"], w["final_g"] if li == DEPTH - 1 else None,
                gla_state, new_gla, layer=j, n_layers=n_gla, tb=tb_gla, tl=tl_gla)
    return x, new_pool, new_gla


def kernel(x_prompt, x_sample, c_prompt, c_sample, state_pool, state_gla, ada_w, ada_b, norm_g, pool_w_in,
           pool_w_grp, pool_scale, pool_w_out, gla_w_in, gla_w_gate_up, gla_b_gate, gla_norm_g, gla_w_out,
           final_g):
    n_prompt = x_prompt.shape[0]
    n_gla = gla_w_in.shape[0]
    mods = _ada_mods(jnp.concatenate([c_prompt, c_sample], axis=0), ada_w, ada_b)

    assert gla_w_in.shape[-1] == GLA_QKVZ + GLA_RANK
    w_gl_pad = jnp.concatenate(
        [gla_w_in[:, :, GLA_QKVZ:], jnp.zeros((n_gla, D_MODEL, GLA_GL_PAD - GLA_RANK), gla_w_in.dtype)], axis=-1)
    w_up_pad = jnp.concatenate(
        [gla_w_gate_up, jnp.zeros((n_gla, GLA_GL_PAD - GLA_RANK, GLA_DK), gla_w_gate_up.dtype)], axis=1)
    w = {
        "norm_g": norm_g.reshape(DEPTH, 1, D_MODEL),
        "pool_w_in": pool_w_in.astype(BF16),
        "pool_w_grp": pool_w_grp.astype(BF16),
        "pool_scale": pool_scale.reshape(-1, 1, D_INNER),
        "pool_w_out": pool_w_out.astype(BF16),
        "gla": {
            "w_in": gla_w_in.astype(BF16),
            "w_gl": w_gl_pad.astype(BF16),
            "w_up": w_up_pad.astype(BF16),
            "b_gate": gla_b_gate.reshape(-1, 1, GLA_DK),
            "norm_g": gla_norm_g.reshape(-1, 1, GLA_DV),
            "w_out": gla_w_out.astype(BF16),
        },
        "final_g": final_g.reshape(1, D_MODEL),
    }
    y_p, pool_p, gla_p = _run_trunk(x_prompt, mods[:, :, :n_prompt], None, None, 0, w,
                                    tb_pool=1, tl_pool=512, tb_gla=1, tl_gla=256)
    y_s, pool_s, gla_s = _run_trunk(x_sample, mods[:, :, n_prompt:], state_pool, state_gla, PAST_LEN, w,
                                    tb_pool=32, tl_pool=x_sample.shape[1], tb_gla=8, tl_gla=x_sample.shape[1])
    return (y_p, y_s, pool_p, gla_p, pool_s, gla_s)
```

```python
import functools

import jax
import jax.numpy as jnp
from jax import lax
from jax.experimental import pallas as pl
from jax.experimental.pallas import tpu as pltpu

D_MODEL = 1024
DEPTH = 4
PAST_LEN = 16384
D_INNER = 2048
POOL_WINDOWS = (2, 4, 8, 16)
POOL_GROUP = D_INNER // len(POOL_WINDOWS)
POOL_BUF = max(POOL_WINDOWS) - 1
POOL_HALO = POOL_BUF + 1
GLA_HEADS = 4
GLA_DK = 512
GLA_DV = D_INNER
GLA_HK = GLA_DK // GLA_HEADS
GLA_HV = GLA_DV // GLA_HEADS
GLA_RANK = 16
GLA_GATE_NORM = 16.0
GLA_CHUNK = 64
GLA_GL_PAD = 128
GLA_QKVZ = 2 * GLA_DK + 2 * GLA_DV
EPS = 1e-6
F32 = jnp.float32
BF16 = jnp.bfloat16

VMEM_LIMIT_BYTES = 62 * 1024 * 1024

_NT = (((1,), (1,)), ((), ()))
_TN = (((0,), (0,)), ((), ()))


def _silu(v):
    return v / (1.0 + jnp.exp(-v))


def _mod_norm(x, g, shift, scale):
    ms = jnp.mean(x * x, axis=-1, keepdims=True)
    y = x * lax.rsqrt(ms + EPS) * g
    return y * (1.0 + scale) + shift


def _const_spec(shape, lead=(), tail=None):
    idx = tuple(lead) + tuple(tail if tail is not None else (0,) * len(shape))
    return pl.BlockSpec((None,) * len(lead) + tuple(shape), lambda *_: idx, pipeline_mode=pl.Buffered(1))


def _ada_kernel(c_ref, w_ref, b_ref, o_ref):
    cs = _silu(c_ref[...]).astype(BF16)
    o_ref[0, 0] = jnp.dot(cs, w_ref[0].astype(BF16), preferred_element_type=F32) + b_ref[0, 0]


def _ada_mods(c_all, ada_w, ada_b):
    nb = c_all.shape[0]
    b4 = ada_b.reshape(DEPTH, 3, 1, D_MODEL)
    return pl.pallas_call(
        _ada_kernel,
        grid=(DEPTH, 3),
        in_specs=[
            pl.BlockSpec((nb, D_MODEL), lambda li, j: (0, 0)),
            pl.BlockSpec((1, D_MODEL, D_MODEL), lambda li, j: (li, 0, j)),
            pl.BlockSpec((1, 1, 1, D_MODEL), lambda li, j: (li, j, 0, 0)),
        ],
        out_specs=pl.BlockSpec((1, 1, nb, D_MODEL), lambda li, j: (li, j, 0, 0)),
        out_shape=jax.ShapeDtypeStruct((DEPTH, 3, nb, D_MODEL), F32),
        name="ada_mods",
    )(c_all, ada_w, b4)


def _pool_kernel(*refs, tb, tl, n_l, pos0, fresh, has_prev):
    it = iter(refs)
    x_ref, sh_ref, sc_ref, gt_ref, g_ref = (next(it) for _ in range(5))
    win_ref, wgrp_ref, pscale_ref, wout_ref = (next(it) for _ in range(4))
    buf_ref = None if fresh else next(it)
    if has_prev:
        next(it)
    y_ref, nbuf_ref, ext_ref, gated_ref = (next(it) for _ in range(4))

    l = pl.program_id(1)
    m = tb * tl

    @pl.when(l == 0)
    def _():
        ext_ref[:, 0:POOL_HALO, :] = jnp.zeros((tb, POOL_HALO, D_INNER), F32)
        if not fresh:
            ext_ref[:, 1:POOL_HALO, :] = buf_ref[0]

    x = x_ref[...]
    h = _mod_norm(x, g_ref[...], sh_ref[...], sc_ref[...]).reshape(m, D_MODEL).astype(BF16)
    uz = jnp.dot(h, win_ref[...], preferred_element_type=F32)
    ext_ref[:, POOL_HALO:POOL_HALO + tl, :] = uz[:, :D_INNER].reshape(tb, tl, D_INNER)

    diffs = []
    for gi, w in enumerate(POOL_WINDOWS):
        c0 = gi * POOL_GROUP
        u_g = ext_ref[:, POOL_HALO:POOL_HALO + tl, c0:c0 + POOL_GROUP]
        s = u_g
        for j in range(1, w):
            s = s + ext_ref[:, POOL_HALO - j:POOL_HALO - j + tl, c0:c0 + POOL_GROUP]
        if pos0 >= w - 1:
            mean = s * (1.0 / w)
        else:
            pos = pos0 + l * tl + lax.broadcasted_iota(jnp.int32, (tb, tl, POOL_GROUP), 1)
            mean = s / jnp.minimum(pos + 1, w).astype(F32)
        diffs.append((mean - u_g).reshape(m, POOL_GROUP).astype(BF16))
    mixed = [jnp.dot(diffs[gi], wgrp_ref[gi], preferred_element_type=F32) for gi in range(len(POOL_WINDOWS))]
    for gi in range(len(POOL_WINDOWS)):
        c0 = gi * POOL_GROUP
        z_g = uz[:, D_INNER + c0:D_INNER + c0 + POOL_GROUP]
        gated_ref[:, c0:c0 + POOL_GROUP] = (
            mixed[gi] * pscale_ref[:, c0:c0 + POOL_GROUP] * _silu(z_g)).astype(BF16)

    y = jnp.dot(gated_ref[...], wout_ref[...], preferred_element_type=F32)
    y_ref[...] = x + gt_ref[...] * y.reshape(tb, tl, D_MODEL)

    @pl.when(l == n_l - 1)
    def _():
        nbuf_ref[0] = ext_ref[:, tl + 1:tl + POOL_HALO, :]

    if n_l > 1:
        @pl.when(l < n_l - 1)
        def _():
            ext_ref[:, 0:POOL_HALO, :] = ext_ref[:, tl:tl + POOL_HALO, :]


def _mod_specs(tb, li, row0):
    assert row0 % tb == 0
    return [pl.BlockSpec((None, tb, 1, D_MODEL), lambda bi, l, j=j: (li * 3 + j, row0 // tb + bi, 0, 0))
            for j in range(3)]


def _pool_layer(x, mods, li, row0, norm_g, w_in, w_grp, pscale, w_out, state, prev_out, *, layer, n_layers,
                tb, tl, pos0):
    b, L, _ = x.shape
    n_l = L // tl
    fresh = state is None
    has_prev = prev_out is not None
    seq = lambda bi, l: (bi, l, 0)
    in_specs = [
        pl.BlockSpec((tb, tl, D_MODEL), seq),
        *_mod_specs(tb, li, row0),
        _const_spec((1, D_MODEL), lead=(li,)),
        _const_spec((D_MODEL, 2 * D_INNER), lead=(layer,)),
        _const_spec((len(POOL_WINDOWS), POOL_GROUP, POOL_GROUP), lead=(layer,)),
        _const_spec((1, D_INNER), lead=(layer,)),
        _const_spec((D_INNER, D_MODEL), lead=(layer,)),
    ]
    args = [x, mods, mods, mods, norm_g, w_in, w_grp, pscale, w_out]
    if not fresh:
        in_specs.append(pl.BlockSpec((1, tb, POOL_BUF, D_INNER), lambda bi, li: (layer, bi, 0, 0)))
        args.append(state)
    aliases = {}
    if has_prev:
        in_specs.append(pl.BlockSpec(memory_space=pl.ANY))
        aliases = {len(args): 1}
        args.append(prev_out)
    kern = functools.partial(_pool_kernel, tb=tb, tl=tl, n_l=n_l, pos0=pos0, fresh=fresh, has_prev=has_prev)
    return pl.pallas_call(
        kern,
        grid=(b // tb, n_l),
        in_specs=in_specs,
        out_specs=[
            pl.BlockSpec((tb, tl, D_MODEL), seq),
            pl.BlockSpec((1, tb, POOL_BUF, D_INNER), lambda bi, li: (layer, bi, 0, 0)),
        ],
        out_shape=[
            jax.ShapeDtypeStruct(x.shape, F32),
            jax.ShapeDtypeStruct((n_layers, b, POOL_BUF, D_INNER), F32),
        ],
        scratch_shapes=[
            pltpu.VMEM((tb, POOL_HALO + tl, D_INNER), F32),
            pltpu.VMEM((tb * tl, D_INNER), BF16),
        ],
        input_output_aliases=aliases,
        compiler_params=pltpu.CompilerParams(
            dimension_semantics=("arbitrary", "arbitrary"), vmem_limit_bytes=VMEM_LIMIT_BYTES),
        name=f"pool_layer{layer}_{'fresh' if fresh else 'cont'}",
    )(*args)


def _gla_kernel(*refs, tb, tl, n_l, fresh, has_prev, final):
    it = iter(refs)
    x_ref, sh_ref, sc_ref, gt_ref, g_ref = (next(it) for _ in range(5))
    win_ref, wgl_ref, wup_ref, bg_ref, ng_ref, wout_ref = (next(it) for _ in range(6))
    fg_ref = next(it) if final else None
    s0_ref = None if fresh else next(it)
    if has_prev:
        next(it)
    y_ref, s_ref, gated_ref = (next(it) for _ in range(3))

    l = pl.program_id(1)
    m = tb * tl

    @pl.when(l == 0)
    def _():
        if fresh:
            s_ref[...] = jnp.zeros(s_ref.shape, F32)
        else:
            s_ref[...] = s0_ref[...]

    x = x_ref[...]
    h = _mod_norm(x, g_ref[...], sh_ref[...], sc_ref[...]).reshape(m, D_MODEL).astype(BF16)
    qk = jnp.dot(h, win_ref[:, :2 * GLA_DK], preferred_element_type=F32)
    gl = jnp.dot(h, wgl_ref[...], preferred_element_type=F32).astype(BF16)
    gate_pre = jnp.dot(gl, wup_ref[...], preferred_element_type=F32) + bg_ref[...]
    lg = (jnp.minimum(gate_pre, 0.0) - jnp.log(1.0 + jnp.exp(-jnp.abs(gate_pre)))) * (1.0 / GLA_GATE_NORM)

    def project_vz():
        return jnp.dot(h, win_ref[:, 2 * GLA_DK:], preferred_element_type=F32)

    def finish_head(rows, hd, o_h, vz):
        o_h = o_h * lax.rsqrt(jnp.mean(o_h * o_h, axis=-1, keepdims=True) + EPS)
        hc = slice(hd * GLA_HV, (hd + 1) * GLA_HV)
        z_h = vz[rows, GLA_DV + hd * GLA_HV:GLA_DV + (hd + 1) * GLA_HV]
        gated_ref[rows, hc] = (o_h * ng_ref[:, hc] * _silu(z_h)).astype(BF16)

    def project_out(rows):
        y = jnp.dot(gated_ref[rows, :], wout_ref[...], preferred_element_type=F32)
        if tb == 1:
            out = x_ref[0, rows, :] + gt_ref[0] * y
        else:
            assert rows == slice(0, m)
            out = x + gt_ref[...] * y.reshape(tb, tl, D_MODEL)
        if final:
            out = out * lax.rsqrt(jnp.mean(out * out, axis=-1, keepdims=True) + EPS) * fg_ref[...]
        if tb == 1:
            y_ref[0, rows, :] = out
        else:
            y_ref[...] = out

    if tl >= 2 * GLA_CHUNK:
        _gla_long_chunks(qk, lg, project_vz, s_ref, finish_head, project_out, m=m, tl=tl)
    else:
        _gla_short_chunks(qk, lg, project_vz, s_ref, finish_head, project_out, m=m, tl=tl)


def _prefix_sums(pat, lg_rows):
    hi = lg_rows.astype(BF16)
    lo = (lg_rows - hi.astype(F32)).astype(BF16)
    return jnp.dot(pat, hi, preferred_element_type=F32) + jnp.dot(pat, lo, preferred_element_type=F32)


def _decay_cols(decay_row):
    d_col = jnp.transpose(jnp.broadcast_to(decay_row, (GLA_HK, GLA_HK)))
    return jnp.concatenate([d_col] * (GLA_HV // GLA_HK), axis=1)


def _gla_long_chunks(qk, lg, project_vz, s_ref, finish_head, project_out, *, m, tl):
    hb = GLA_CHUNK
    cr = 2 * hb
    ri = lax.broadcasted_iota(jnp.int32, (cr, cr), 0)
    ci = lax.broadcasted_iota(jnp.int32, (cr, cr), 1)
    causal = ci <= ri
    tri = jnp.where(causal, 1.0, 0.0).astype(BF16)
    zeros = jnp.zeros((hb, GLA_HK), BF16)
    heads = range(GLA_HEADS)
    kcs = [slice(hd * GLA_HK, (hd + 1) * GLA_HK) for hd in heads]

    def prepare(c):
        rows = slice(c * cr, (c + 1) * cr)
        bcum = _prefix_sums(tri, lg[rows])
        b_a, b_b = bcum[:hb], bcum[hb:]
        ref_a, bnd = bcum[hb // 2 - 1:hb // 2], bcum[hb - 1:hb]
        ref_b, tot = bcum[hb + hb // 2 - 1:hb + hb // 2], bcum[cr - 1:cr]
        q = qk[rows, 0:GLA_DK] * (GLA_HK ** -0.5)
        k = qk[rows, GLA_DK:2 * GLA_DK]
        q_a, q_b, k_a, k_b = q[:hb], q[hb:], k[:hb], k[hb:]
        qi_a = (q_a * jnp.exp(b_a - ref_a)).astype(BF16)
        ki_a = (k_a * jnp.exp(ref_a - b_a)).astype(BF16)
        qi_b = (q_b * jnp.exp(b_b - ref_b)).astype(BF16)
        ki_b = (k_b * jnp.exp(ref_b - b_b)).astype(BF16)
        qx_b = (q_b * jnp.exp(b_b - bnd)).astype(BF16)
        kx_a = (k_a * jnp.exp(bnd - b_a)).astype(BF16)
        q_dec = (q * jnp.exp(bcum)).astype(BF16)
        k_state = (k * jnp.exp(tot - bcum)).astype(BF16)
        decay = jnp.exp(tot)
        return qi_a, ki_a, qi_b, ki_b, qx_b, kx_a, q_dec, k_state, decay

    vz = project_vz()
    prepared = [prepare(c) for c in range(m // cr)]
    for c, (qi_a, ki_a, qi_b, ki_b, qx_b, kx_a, q_dec, k_state, decay) in enumerate(prepared):
        rows = slice(c * cr, (c + 1) * cr)
        si = (c * cr) // tl
        v = [vz[rows, hd * GLA_HV:(hd + 1) * GLA_HV].astype(BF16) for hd in heads]
        s_old = [s_ref[0, si, hd] for hd in heads]
        att_a = [lax.dot_general(qi_a[:, kc], jnp.concatenate([ki_a[:, kc], zeros], axis=0), _NT,
                                 preferred_element_type=F32) for kc in kcs]
        att_b = [lax.dot_general(
            jnp.concatenate([qx_b[:, kc], qi_b[:, kc]], axis=1),
            jnp.concatenate([jnp.concatenate([kx_a[:, kc], zeros], axis=1),
                             jnp.concatenate([zeros, ki_b[:, kc]], axis=1)], axis=0),
            _NT, preferred_element_type=F32) for kc in kcs]
        upd = [lax.dot_general(k_state[:, kc], v[hd], _TN, preferred_element_type=F32)
               for hd, kc in enumerate(kcs)]
        att = [jnp.where(causal, jnp.concatenate([att_a[hd], att_b[hd]], axis=0), 0.0).astype(BF16)
               for hd in heads]
        o = [jnp.dot(jnp.concatenate([att[hd], q_dec[:, kc]], axis=1),
                     jnp.concatenate([v[hd], s_old[hd].astype(BF16)], axis=0), preferred_element_type=F32)
             for hd, kc in enumerate(kcs)]
        for hd, kc in enumerate(kcs):
            s_ref[0, si, hd] = s_old[hd] * _decay_cols(decay[:, kc]) + upd[hd]
        for hd in heads:
            finish_head(rows, hd, o[hd], vz)
    project_out(slice(0, m))


def _gla_short_chunks(qk, lg, project_vz, s_ref, finish_head, project_out, *, m, tl):
    seg = tl
    n_seg = GLA_CHUNK // seg
    vz = project_vz()
    ri = lax.broadcasted_iota(jnp.int32, (GLA_CHUNK, GLA_CHUNK), 0)
    ci = lax.broadcasted_iota(jnp.int32, (GLA_CHUNK, GLA_CHUNK), 1)
    sh = seg.bit_length() - 1
    same = (ri >> sh) == (ci >> sh)
    causal = same & (ci <= ri)
    mid = same & ((ci & (seg - 1)) <= (seg // 2 - 1))
    pat = jnp.concatenate([jnp.where(mk, 1.0, 0.0).astype(BF16) for mk in (causal, mid, same)], axis=0)

    for c in range(m // GLA_CHUNK):
        r0 = c * GLA_CHUNK
        rows = slice(r0, r0 + GLA_CHUNK)
        sums = _prefix_sums(pat, lg[rows])
        bcum, bref, btot = sums[:GLA_CHUNK], sums[GLA_CHUNK:2 * GLA_CHUNK], sums[2 * GLA_CHUNK:]
        q = qk[rows, 0:GLA_DK] * (GLA_HK ** -0.5)
        k = qk[rows, GLA_DK:2 * GLA_DK]
        q_intra = (q * jnp.exp(bcum - bref)).astype(BF16)
        k_intra = (k * jnp.exp(bref - bcum)).astype(BF16)
        q_dec = (q * jnp.exp(bcum)).astype(BF16)
        k_state = (k * jnp.exp(btot - bcum)).astype(BF16)
        decay = jnp.exp(btot)
        for hd in range(GLA_HEADS):
            kc = slice(hd * GLA_HK, (hd + 1) * GLA_HK)
            v_h = vz[rows, hd * GLA_HV:(hd + 1) * GLA_HV].astype(BF16)
            att = lax.dot_general(q_intra[:, kc], k_intra[:, kc], _NT, preferred_element_type=F32)
            att = jnp.where(causal, att, 0.0).astype(BF16)
            o_h = jnp.dot(att, v_h, preferred_element_type=F32)
            o_inter = []
            for sg in range(n_seg):
                srows = slice(sg * seg, (sg + 1) * seg)
                si = (r0 + sg * seg) // tl
                s_old = s_ref[0, si, hd]
                o_inter.append(jnp.dot(q_dec[srows, kc], s_old.astype(BF16), preferred_element_type=F32))
                upd = lax.dot_general(k_state[srows, kc], v_h[srows], _TN, preferred_element_type=F32)
                s_ref[0, si, hd] = s_old * _decay_cols(decay[sg * seg:sg * seg + 1, kc]) + upd
            finish_head(rows, hd, o_h + jnp.concatenate(o_inter, axis=0), vz)
    project_out(slice(0, m))


def _gla_layer(x, mods, li, row0, norm_g, gw, final_g, state, prev_out, *, layer, n_layers, tb, tl):
    b, L, _ = x.shape
    n_l = L // tl
    fresh = state is None
    has_prev = prev_out is not None
    final = final_g is not None
    m = tb * tl
    seq = lambda bi, l: (bi, l, 0)
    st_block = (1, tb, GLA_HEADS, GLA_HK, GLA_HV)
    st_map = lambda bi, l: (layer, bi, 0, 0, 0)
    in_specs = [
        pl.BlockSpec((tb, tl, D_MODEL), seq),
        *_mod_specs(tb, li, row0),
        _const_spec((1, D_MODEL), lead=(li,)),
        _const_spec((D_MODEL, GLA_QKVZ), lead=(layer,)),
        _const_spec((D_MODEL, GLA_GL_PAD), lead=(layer,)),
        _const_spec((GLA_GL_PAD, GLA_DK), lead=(layer,)),
        _const_spec((1, GLA_DK), lead=(layer,)),
        _const_spec((1, GLA_DV), lead=(layer,)),
        _const_spec((GLA_DV, D_MODEL), lead=(layer,)),
    ]
    args = [x, mods, mods, mods, norm_g, gw["w_in"], gw["w_gl"], gw["w_up"], gw["b_gate"], gw["norm_g"],
            gw["w_out"]]
    if final:
        in_specs.append(_const_spec((1, D_MODEL)))
        args.append(final_g)
    if not fresh:
        in_specs.append(pl.BlockSpec(st_block, st_map))
        args.append(state)
    aliases = {}
    if has_prev:
        in_specs.append(pl.BlockSpec(memory_space=pl.ANY))
        aliases = {len(args): 1}
        args.append(prev_out)
    kern = functools.partial(_gla_kernel, tb=tb, tl=tl, n_l=n_l, fresh=fresh, has_prev=has_prev, final=final)
    return pl.pallas_call(
        kern,
        grid=(b // tb, n_l),
        in_specs=in_specs,
        out_specs=[pl.BlockSpec((tb, tl, D_MODEL), seq), pl.BlockSpec(st_block, st_map)],
        out_shape=[
            jax.ShapeDtypeStruct(x.shape, F32),
            jax.ShapeDtypeStruct((n_layers, b, GLA_HEADS, GLA_HK, GLA_HV), F32),
        ],
        scratch_shapes=[pltpu.VMEM((m, GLA_DV), BF16)],
        input_output_aliases=aliases,
        compiler_params=pltpu.CompilerParams(
            dimension_semantics=("arbitrary", "arbitrary"), vmem_limit_bytes=VMEM_LIMIT_BYTES),
        name=f"gla_layer{layer}_{'fresh' if fresh else 'cont'}",
    )(*args)


def _run_trunk(x, mods, row0, pool_state, gla_state, pos0, w, *, tb_pool, tl_pool, tb_gla, tl_gla):
    n_pool, n_gla = (DEPTH + 1) // 2, DEPTH // 2
    new_pool = new_gla = None
    for li in range(DEPTH):
        j = li // 2
        if li % 2 == 0:
            x, new_pool = _pool_layer(
                x, mods, li, row0, w["norm_g"], w["pool_w_in"], w["pool_w_grp"],
                w["pool_scale"], w["pool_w_out"], pool_state, new_pool,
                layer=j, n_layers=n_pool, tb=tb_pool, tl=tl_pool, pos0=pos0)
        else:
            x, new_gla = _gla_layer(
                x, mods, li, row0, w["norm_g"], w["gla"], w["final_g"] if li == DEPTH - 1 else None,
                gla_state, new_gla, layer=j, n_layers=n_gla, tb=tb_gla, tl=tl_gla)
    return x, new_pool, new_gla


def kernel(x_prompt, x_sample, c_prompt, c_sample, state_pool, state_gla, ada_w, ada_b, norm_g, pool_w_in,
           pool_w_grp, pool_scale, pool_w_out, gla_w_in, gla_w_gate_up, gla_b_gate, gla_norm_g, gla_w_out,
           final_g):
    n_sample = x_sample.shape[0]
    n_gla = gla_w_in.shape[0]
    c_all = jnp.concatenate([c_sample, c_prompt], axis=0)
    mods = _ada_mods(c_all, ada_w, ada_b).reshape(DEPTH * 3, c_all.shape[0], 1, D_MODEL)

    assert gla_w_in.shape[-1] == GLA_QKVZ + GLA_RANK
    w_gl_pad = jnp.concatenate(
        [gla_w_in[:, :, GLA_QKVZ:], jnp.zeros((n_gla, D_MODEL, GLA_GL_PAD - GLA_RANK), gla_w_in.dtype)], axis=-1)
    w_up_pad = jnp.concatenate(
        [gla_w_gate_up, jnp.zeros((n_gla, GLA_GL_PAD - GLA_RANK, GLA_DK), gla_w_gate_up.dtype)], axis=1)
    w = {
        "norm_g": norm_g.reshape(DEPTH, 1, D_MODEL),
        "pool_w_in": pool_w_in.astype(BF16),
        "pool_w_grp": pool_w_grp.astype(BF16),
        "pool_scale": pool_scale.reshape(-1, 1, D_INNER),
        "pool_w_out": pool_w_out.astype(BF16),
        "gla": {
            "w_in": gla_w_in.astype(BF16),
            "w_gl": w_gl_pad.astype(BF16),
            "w_up": w_up_pad.astype(BF16),
            "b_gate": gla_b_gate.reshape(-1, 1, GLA_DK),
            "norm_g": gla_norm_g.reshape(-1, 1, GLA_DV),
            "w_out": gla_w_out.astype(BF16),
        },
        "final_g": final_g.reshape(1, D_MODEL),
    }
    y_p, pool_p, gla_p = _run_trunk(x_prompt, mods, n_sample, None, None, 0, w,
                                    tb_pool=1, tl_pool=512, tb_gla=1, tl_gla=512)
    y_s, pool_s, gla_s = _run_trunk(x_sample, mods, 0, state_pool, state_gla, PAST_LEN, w,
                                    tb_pool=32, tl_pool=x_sample.shape[1], tb_gla=8, tl_gla=x_sample.shape[1])
    return (y_p, y_s, pool_p, gla_p, pool_s, gla_s)
```

```python
import functools

import jax
import jax.numpy as jnp
from jax import lax
from jax.experimental import pallas as pl
from jax.experimental.pallas import tpu as pltpu

D_MODEL = 1024
DEPTH = 4
PAST_LEN = 16384
D_INNER = 2048
POOL_WINDOWS = (2, 4, 8, 16)
POOL_GROUP = D_INNER // len(POOL_WINDOWS)
POOL_BUF = max(POOL_WINDOWS) - 1
POOL_HALO = POOL_BUF + 1
GLA_HEADS = 4
GLA_DK = 512
GLA_DV = D_INNER
GLA_HK = GLA_DK // GLA_HEADS
GLA_HV = GLA_DV // GLA_HEADS
GLA_RANK = 16
GLA_GATE_NORM = 16.0
GLA_CHUNK = 64
GLA_GL_PAD = 128
GLA_QKVZ = 2 * GLA_DK + 2 * GLA_DV
EPS = 1e-6
F32 = jnp.float32
BF16 = jnp.bfloat16

VMEM_LIMIT_BYTES = 62 * 1024 * 1024

_NT = (((1,), (1,)), ((), ()))
_TN = (((0,), (0,)), ((), ()))


def _silu(v):
    return v / (1.0 + jnp.exp(-v))


def _mod_norm(x, g, shift, scale):
    ms = jnp.mean(x * x, axis=-1, keepdims=True)
    y = x * lax.rsqrt(ms + EPS) * g
    return y * (1.0 + scale) + shift


def _const_spec(shape, lead=(), tail=None):
    idx = tuple(lead) + tuple(tail if tail is not None else (0,) * len(shape))
    return pl.BlockSpec((None,) * len(lead) + tuple(shape), lambda *_: idx, pipeline_mode=pl.Buffered(1))


def _ada_kernel(c_ref, w_ref, b_ref, o_ref):
    cs = _silu(c_ref[...]).astype(BF16)
    o_ref[0, 0] = jnp.dot(cs, w_ref[0].astype(BF16), preferred_element_type=F32) + b_ref[0, 0]


def _ada_mods(c_all, ada_w, ada_b):
    nb = c_all.shape[0]
    b4 = ada_b.reshape(DEPTH, 3, 1, D_MODEL)
    return pl.pallas_call(
        _ada_kernel,
        grid=(DEPTH, 3),
        in_specs=[
            pl.BlockSpec((nb, D_MODEL), lambda li, j: (0, 0)),
            pl.BlockSpec((1, D_MODEL, D_MODEL), lambda li, j: (li, 0, j)),
            pl.BlockSpec((1, 1, 1, D_MODEL), lambda li, j: (li, j, 0, 0)),
        ],
        out_specs=pl.BlockSpec((1, 1, nb, D_MODEL), lambda li, j: (li, j, 0, 0)),
        out_shape=jax.ShapeDtypeStruct((DEPTH, 3, nb, D_MODEL), F32),
        name="ada_mods",
    )(c_all, ada_w, b4)


def _pool_mix(ext_ref, uz, wgrp_ref, pscale_ref, gated_ref, *, halo, time_axis, rows, pos):
    m = uz.shape[0]

    def window(j, c0):
        idx = [slice(None)] * 3
        idx[time_axis] = slice(halo - j, halo - j + rows)
        idx[2] = slice(c0, c0 + POOL_GROUP)
        return ext_ref[tuple(idx)]

    diffs = []
    for gi, w in enumerate(POOL_WINDOWS):
        c0 = gi * POOL_GROUP
        u_g = window(0, c0)
        s = u_g
        for j in range(1, w):
            s = s + window(j, c0)
        mean = s * (1.0 / w) if pos is None else s / jnp.minimum(pos + 1, w).astype(F32)
        diffs.append((mean - u_g).reshape(m, POOL_GROUP).astype(BF16))
    mixed = [jnp.dot(diffs[gi], wgrp_ref[gi], preferred_element_type=F32) for gi in range(len(POOL_WINDOWS))]
    for gi in range(len(POOL_WINDOWS)):
        c0 = gi * POOL_GROUP
        z_g = uz[:, D_INNER + c0:D_INNER + c0 + POOL_GROUP]
        gated_ref[:, c0:c0 + POOL_GROUP] = (
            mixed[gi] * pscale_ref[:, c0:c0 + POOL_GROUP] * _silu(z_g)).astype(BF16)


def _pool_kernel(*refs, tl, n_l, row0, has_prev):
    it = iter(refs)
    x_ref = next(it)
    mod_refs = [next(it) for _ in range(3)]
    g_ref, win_ref, wgrp_ref, pscale_ref, wout_ref = (next(it) for _ in range(5))
    if has_prev:
        next(it)
    y_ref, nbuf_ref, ext_ref, gated_ref = (next(it) for _ in range(4))

    l = pl.program_id(1)

    @pl.when(l == 0)
    def _():
        ext_ref[:, 0:POOL_HALO, :] = jnp.zeros((1, POOL_HALO, D_INNER), F32)

    shift, scale, gate = _load_mods(mod_refs, 1, row0)
    x = x_ref[...]
    h = _mod_norm(x, g_ref[...], shift, scale).reshape(tl, D_MODEL).astype(BF16)
    uz = jnp.dot(h, win_ref[...], preferred_element_type=F32)
    ext_ref[:, POOL_HALO:POOL_HALO + tl, :] = uz[:, :D_INNER].reshape(1, tl, D_INNER)
    pos = l * tl + lax.broadcasted_iota(jnp.int32, (1, tl, POOL_GROUP), 1)
    _pool_mix(ext_ref, uz, wgrp_ref, pscale_ref, gated_ref, halo=POOL_HALO, time_axis=1, rows=tl, pos=pos)
    y = jnp.dot(gated_ref[...], wout_ref[...], preferred_element_type=F32)
    y_ref[...] = x + gate * y.reshape(1, tl, D_MODEL)

    @pl.when(l == n_l - 1)
    def _():
        nbuf_ref[0] = ext_ref[:, tl + 1:tl + POOL_HALO, :]

    if n_l > 1:
        @pl.when(l < n_l - 1)
        def _():
            ext_ref[:, 0:POOL_HALO, :] = ext_ref[:, tl:tl + POOL_HALO, :]


def _pool_cont_kernel(*refs, tb, tl, pos0, has_prev):
    it = iter(refs)
    x_ref = next(it)
    mod_refs = [next(it) for _ in range(3)]
    g_ref, win_ref, wgrp_ref, pscale_ref, wout_ref, buf_ref = (next(it) for _ in range(6))
    if has_prev:
        next(it)
    y_ref, nbuf_ref, ext_ref, gated_ref = (next(it) for _ in range(4))
    m = tb * tl

    shift, scale, gate = _load_mods(mod_refs, tb, 0)
    x = x_ref[...]
    h = pltpu.einshape("btd->tbd", _mod_norm(x, g_ref[...], shift, scale))
    uz = jnp.dot(h.reshape(m, D_MODEL).astype(BF16), win_ref[...], preferred_element_type=F32)
    ext_ref[0:POOL_BUF] = buf_ref[...]
    ext_ref[POOL_BUF:POOL_BUF + tl] = uz[:, :D_INNER].reshape(tl, tb, D_INNER)
    pos = None
    if pos0 < POOL_BUF:
        pos = pos0 + lax.broadcasted_iota(jnp.int32, (tl, tb, POOL_GROUP), 0)
    _pool_mix(ext_ref, uz, wgrp_ref, pscale_ref, gated_ref, halo=POOL_BUF, time_axis=0, rows=tl, pos=pos)
    y = jnp.dot(gated_ref[...], wout_ref[...], preferred_element_type=F32)
    y_ref[...] = x + gate * pltpu.einshape("tbd->btd", y.reshape(tl, tb, D_MODEL))
    nbuf_ref[...] = ext_ref[tl:tl + POOL_BUF]


MOD_ROWS = 8


def _mod_specs(tb, li, row0):
    rows = max(tb, MOD_ROWS)
    assert row0 % rows == 0 and rows % tb == 0
    return [pl.BlockSpec((None, rows, D_MODEL), lambda bi, *_, j=j: (li * 3 + j, (row0 + bi * tb) // rows, 0))
            for j in range(3)]


def _load_mods(refs, tb, row0):
    if tb >= MOD_ROWS:
        return [ref[...][:, None, :] for ref in refs]
    assert tb == 1
    r = (row0 + pl.program_id(0)) % MOD_ROWS
    return [ref[pl.ds(r, 1), :] for ref in refs]


def _pool_layer(x, mods, li, row0, norm_g, w_in, w_grp, pscale, w_out, state, prev_out, *, layer, n_layers,
                tb, tl, pos0):
    b, L, _ = x.shape
    fresh = state is None
    has_prev = prev_out is not None
    in_specs = [
        None,
        *_mod_specs(tb, li, row0),
        _const_spec((1, D_MODEL), lead=(li,)),
        _const_spec((D_MODEL, 2 * D_INNER), lead=(layer,)),
        _const_spec((len(POOL_WINDOWS), POOL_GROUP, POOL_GROUP), lead=(layer,)),
        _const_spec((1, D_INNER), lead=(layer,)),
        _const_spec((D_INNER, D_MODEL), lead=(layer,)),
    ]
    args = [x, mods, mods, mods, norm_g, w_in, w_grp, pscale, w_out]
    if fresh:
        assert tb == 1 and pos0 == 0
        n_l = L // tl
        grid = (b, n_l)
        x_spec = pl.BlockSpec((1, tl, D_MODEL), lambda bi, l: (bi, l, 0))
        buf_shape = (n_layers, b, POOL_BUF, D_INNER)
        buf_spec = pl.BlockSpec((1, 1, POOL_BUF, D_INNER), lambda bi, l: (layer, bi, 0, 0))
        ext_shape = (1, POOL_HALO + tl, D_INNER)
        kern = functools.partial(_pool_kernel, tl=tl, n_l=n_l, row0=row0, has_prev=has_prev)
        semantics = ("arbitrary", "arbitrary")
    else:
        assert tl == L and row0 == 0
        grid = (b // tb,)
        x_spec = pl.BlockSpec((tb, tl, D_MODEL), lambda bi: (bi, 0, 0))
        buf_shape = (n_layers, POOL_BUF, b, D_INNER)
        buf_spec = pl.BlockSpec((None, POOL_BUF, tb, D_INNER), lambda bi: (layer, 0, bi, 0))
        ext_shape = (POOL_BUF + tl, tb, D_INNER)
        in_specs.append(buf_spec)
        args.append(state)
        kern = functools.partial(_pool_cont_kernel, tb=tb, tl=tl, pos0=pos0, has_prev=has_prev)
        semantics = ("arbitrary",)
    in_specs[0] = x_spec
    aliases = {}
    if has_prev:
        in_specs.append(pl.BlockSpec(memory_space=pl.ANY))
        aliases = {len(args): 1}
        args.append(prev_out)
    return pl.pallas_call(
        kern,
        grid=grid,
        in_specs=in_specs,
        out_specs=[x_spec, buf_spec],
        out_shape=[jax.ShapeDtypeStruct(x.shape, F32), jax.ShapeDtypeStruct(buf_shape, F32)],
        scratch_shapes=[pltpu.VMEM(ext_shape, F32), pltpu.VMEM((tb * tl, D_INNER), BF16)],
        input_output_aliases=aliases,
        compiler_params=pltpu.CompilerParams(dimension_semantics=semantics, vmem_limit_bytes=VMEM_LIMIT_BYTES),
        name=f"pool_layer{layer}_{'fresh' if fresh else 'cont'}",
    )(*args)


def _gla_kernel(*refs, tb, tl, n_l, row0, fresh, has_prev, final):
    it = iter(refs)
    x_ref = next(it)
    mod_refs = [next(it) for _ in range(3)]
    g_ref, win_ref, wgl_ref, wup_ref, bg_ref, ng_ref, wout_ref = (next(it) for _ in range(7))
    fg_ref = next(it) if final else None
    s0_ref = None if fresh else next(it)
    if has_prev:
        next(it)
    y_ref, s_ref, gated_ref = (next(it) for _ in range(3))

    l = pl.program_id(1)
    m = tb * tl

    @pl.when(l == 0)
    def _():
        if fresh:
            s_ref[...] = jnp.zeros(s_ref.shape, F32)
        else:
            s_ref[...] = s0_ref[...]

    shift, scale, gate = _load_mods(mod_refs, tb, row0)
    x = x_ref[...]
    h = _mod_norm(x, g_ref[...], shift, scale).reshape(m, D_MODEL).astype(BF16)
    qk = jnp.dot(h, win_ref[:, :2 * GLA_DK], preferred_element_type=F32)
    gl = jnp.dot(h, wgl_ref[...], preferred_element_type=F32).astype(BF16)
    gate_pre = jnp.dot(gl, wup_ref[...], preferred_element_type=F32) + bg_ref[...]
    lg = (jnp.minimum(gate_pre, 0.0) - jnp.log(1.0 + jnp.exp(-jnp.abs(gate_pre)))) * (1.0 / GLA_GATE_NORM)

    def project_vz():
        return jnp.dot(h, win_ref[:, 2 * GLA_DK:], preferred_element_type=F32)

    def finish_head(rows, hd, o_h, vz):
        o_h = o_h * lax.rsqrt(jnp.mean(o_h * o_h, axis=-1, keepdims=True) + EPS)
        hc = slice(hd * GLA_HV, (hd + 1) * GLA_HV)
        z_h = vz[rows, GLA_DV + hd * GLA_HV:GLA_DV + (hd + 1) * GLA_HV]
        gated_ref[rows, hc] = (o_h * ng_ref[:, hc] * _silu(z_h)).astype(BF16)

    def project_out(rows):
        y = jnp.dot(gated_ref[rows, :], wout_ref[...], preferred_element_type=F32)
        if tb == 1:
            out = x_ref[0, rows, :] + gate * y
        else:
            assert rows == slice(0, m)
            out = x + gate * y.reshape(tb, tl, D_MODEL)
        if final:
            out = out * lax.rsqrt(jnp.mean(out * out, axis=-1, keepdims=True) + EPS) * fg_ref[...]
        if tb == 1:
            y_ref[0, rows, :] = out
        else:
            y_ref[...] = out

    if tl >= 2 * GLA_CHUNK:
        _gla_long_chunks(qk, lg, project_vz, s_ref, finish_head, project_out, m=m, tl=tl)
    else:
        _gla_short_chunks(qk, lg, project_vz, s_ref, finish_head, project_out, m=m, tl=tl)


def _prefix_sums(pat, lg_rows):
    hi = lg_rows.astype(BF16)
    lo = (lg_rows - hi.astype(F32)).astype(BF16)
    return jnp.dot(pat, hi, preferred_element_type=F32) + jnp.dot(pat, lo, preferred_element_type=F32)


def _decay_cols(decay_row):
    d_col = jnp.transpose(jnp.broadcast_to(decay_row, (GLA_HK, GLA_HK)))
    return jnp.concatenate([d_col] * (GLA_HV // GLA_HK), axis=1)


def _gla_long_chunks(qk, lg, project_vz, s_ref, finish_head, project_out, *, m, tl):
    hb = GLA_CHUNK
    cr = 2 * hb
    ri = lax.broadcasted_iota(jnp.int32, (cr, cr), 0)
    ci = lax.broadcasted_iota(jnp.int32, (cr, cr), 1)
    causal = ci <= ri
    tri = jnp.where(causal, 1.0, 0.0).astype(BF16)
    zeros = jnp.zeros((hb, GLA_HK), BF16)
    heads = range(GLA_HEADS)
    kcs = [slice(hd * GLA_HK, (hd + 1) * GLA_HK) for hd in heads]

    def prepare(c):
        rows = slice(c * cr, (c + 1) * cr)
        bcum = _prefix_sums(tri, lg[rows])
        b_a, b_b = bcum[:hb], bcum[hb:]
        ref_a, bnd = bcum[hb // 2 - 1:hb // 2], bcum[hb - 1:hb]
        ref_b, tot = bcum[hb + hb // 2 - 1:hb + hb // 2], bcum[cr - 1:cr]
        q = qk[rows, 0:GLA_DK] * (GLA_HK ** -0.5)
        k = qk[rows, GLA_DK:2 * GLA_DK]
        q_a, q_b, k_a, k_b = q[:hb], q[hb:], k[:hb], k[hb:]
        qi_a = (q_a * jnp.exp(b_a - ref_a)).astype(BF16)
        ki_a = (k_a * jnp.exp(ref_a - b_a)).astype(BF16)
        qi_b = (q_b * jnp.exp(b_b - ref_b)).astype(BF16)
        ki_b = (k_b * jnp.exp(ref_b - b_b)).astype(BF16)
        qx_b = (q_b * jnp.exp(b_b - bnd)).astype(BF16)
        kx_a = (k_a * jnp.exp(bnd - b_a)).astype(BF16)
        q_dec = (q * jnp.exp(bcum)).astype(BF16)
        k_state = (k * jnp.exp(tot - bcum)).astype(BF16)
        decay = jnp.exp(tot)
        return qi_a, ki_a, qi_b, ki_b, qx_b, kx_a, q_dec, k_state, decay

    vz = project_vz()
    prepared = [prepare(c) for c in range(m // cr)]
    for c, (qi_a, ki_a, qi_b, ki_b, qx_b, kx_a, q_dec, k_state, decay) in enumerate(prepared):
        rows = slice(c * cr, (c + 1) * cr)
        si = (c * cr) // tl
        v = [vz[rows, hd * GLA_HV:(hd + 1) * GLA_HV].astype(BF16) for hd in heads]
        s_old = [s_ref[0, si, hd] for hd in heads]
        att_a = [lax.dot_general(qi_a[:, kc], jnp.concatenate([ki_a[:, kc], zeros], axis=0), _NT,
                                 preferred_element_type=F32) for kc in kcs]
        att_b = [lax.dot_general(
            jnp.concatenate([qx_b[:, kc], qi_b[:, kc]], axis=1),
            jnp.concatenate([jnp.concatenate([kx_a[:, kc], zeros], axis=1),
                             jnp.concatenate([zeros, ki_b[:, kc]], axis=1)], axis=0),
            _NT, preferred_element_type=F32) for kc in kcs]
        upd = [lax.dot_general(k_state[:, kc], v[hd], _TN, preferred_element_type=F32)
               for hd, kc in enumerate(kcs)]
        att = [jnp.where(causal, jnp.concatenate([att_a[hd], att_b[hd]], axis=0), 0.0).astype(BF16)
               for hd in heads]
        o = [jnp.dot(jnp.concatenate([att[hd], q_dec[:, kc]], axis=1),
                     jnp.concatenate([v[hd], s_old[hd].astype(BF16)], axis=0), preferred_element_type=F32)
             for hd, kc in enumerate(kcs)]
        for hd, kc in enumerate(kcs):
            s_ref[0, si, hd] = s_old[hd] * _decay_cols(decay[:, kc]) + upd[hd]
        for hd in heads:
            finish_head(rows, hd, o[hd], vz)
    project_out(slice(0, m))


def _gla_short_chunks(qk, lg, project_vz, s_ref, finish_head, project_out, *, m, tl):
    seg = tl
    n_seg = GLA_CHUNK // seg
    vz = project_vz()
    ri = lax.broadcasted_iota(jnp.int32, (GLA_CHUNK, GLA_CHUNK), 0)
    ci = lax.broadcasted_iota(jnp.int32, (GLA_CHUNK, GLA_CHUNK), 1)
    sh = seg.bit_length() - 1
    same = (ri >> sh) == (ci >> sh)
    causal = same & (ci <= ri)
    mid = same & ((ci & (seg - 1)) <= (seg // 2 - 1))
    pat = jnp.concatenate([jnp.where(mk, 1.0, 0.0).astype(BF16) for mk in (causal, mid, same)], axis=0)

    for c in range(m // GLA_CHUNK):
        r0 = c * GLA_CHUNK
        rows = slice(r0, r0 + GLA_CHUNK)
        sums = _prefix_sums(pat, lg[rows])
        bcum, bref, btot = sums[:GLA_CHUNK], sums[GLA_CHUNK:2 * GLA_CHUNK], sums[2 * GLA_CHUNK:]
        q = qk[rows, 0:GLA_DK] * (GLA_HK ** -0.5)
        k = qk[rows, GLA_DK:2 * GLA_DK]
        q_intra = (q * jnp.exp(bcum - bref)).astype(BF16)
        k_intra = (k * jnp.exp(bref - bcum)).astype(BF16)
        q_dec = (q * jnp.exp(bcum)).astype(BF16)
        k_state = (k * jnp.exp(btot - bcum)).astype(BF16)
        decay = jnp.exp(btot)
        for hd in range(GLA_HEADS):
            kc = slice(hd * GLA_HK, (hd + 1) * GLA_HK)
            v_h = vz[rows, hd * GLA_HV:(hd + 1) * GLA_HV].astype(BF16)
            att = lax.dot_general(q_intra[:, kc], k_intra[:, kc], _NT, preferred_element_type=F32)
            att = jnp.where(causal, att, 0.0).astype(BF16)
            o_h = jnp.dot(att, v_h, preferred_element_type=F32)
            o_inter = []
            for sg in range(n_seg):
                srows = slice(sg * seg, (sg + 1) * seg)
                si = (r0 + sg * seg) // tl
                s_old = s_ref[0, si, hd]
                o_inter.append(jnp.dot(q_dec[srows, kc], s_old.astype(BF16), preferred_element_type=F32))
                upd = lax.dot_general(k_state[srows, kc], v_h[srows], _TN, preferred_element_type=F32)
                s_ref[0, si, hd] = s_old * _decay_cols(decay[sg * seg:sg * seg + 1, kc]) + upd
            finish_head(rows, hd, o_h + jnp.concatenate(o_inter, axis=0), vz)
    project_out(slice(0, m))


def _gla_layer(x, mods, li, row0, norm_g, gw, final_g, state, prev_out, *, layer, n_layers, tb, tl):
    b, L, _ = x.shape
    n_l = L // tl
    fresh = state is None
    has_prev = prev_out is not None
    final = final_g is not None
    m = tb * tl
    seq = lambda bi, l: (bi, l, 0)
    st_block = (1, tb, GLA_HEADS, GLA_HK, GLA_HV)
    st_map = lambda bi, l: (layer, bi, 0, 0, 0)
    in_specs = [
        pl.BlockSpec((tb, tl, D_MODEL), seq),
        *_mod_specs(tb, li, row0),
        _const_spec((1, D_MODEL), lead=(li,)),
        _const_spec((D_MODEL, GLA_QKVZ), lead=(layer,)),
        _const_spec((D_MODEL, GLA_GL_PAD), lead=(layer,)),
        _const_spec((GLA_GL_PAD, GLA_DK), lead=(layer,)),
        _const_spec((1, GLA_DK), lead=(layer,)),
        _const_spec((1, GLA_DV), lead=(layer,)),
        _const_spec((GLA_DV, D_MODEL), lead=(layer,)),
    ]
    args = [x, mods, mods, mods, norm_g, gw["w_in"], gw["w_gl"], gw["w_up"], gw["b_gate"], gw["norm_g"],
            gw["w_out"]]
    if final:
        in_specs.append(_const_spec((1, D_MODEL)))
        args.append(final_g)
    if not fresh:
        in_specs.append(pl.BlockSpec(st_block, st_map))
        args.append(state)
    aliases = {}
    if has_prev:
        in_specs.append(pl.BlockSpec(memory_space=pl.ANY))
        aliases = {len(args): 1}
        args.append(prev_out)
    kern = functools.partial(_gla_kernel, tb=tb, tl=tl, n_l=n_l, row0=row0, fresh=fresh, has_prev=has_prev,
                             final=final)
    return pl.pallas_call(
        kern,
        grid=(b // tb, n_l),
        in_specs=in_specs,
        out_specs=[pl.BlockSpec((tb, tl, D_MODEL), seq), pl.BlockSpec(st_block, st_map)],
        out_shape=[
            jax.ShapeDtypeStruct(x.shape, F32),
            jax.ShapeDtypeStruct((n_layers, b, GLA_HEADS, GLA_HK, GLA_HV), F32),
        ],
        scratch_shapes=[pltpu.VMEM((m, GLA_DV), BF16)],
        input_output_aliases=aliases,
        compiler_params=pltpu.CompilerParams(
            dimension_semantics=("arbitrary", "arbitrary"), vmem_limit_bytes=VMEM_LIMIT_BYTES),
        name=f"gla_layer{layer}_{'fresh' if fresh else 'cont'}",
    )(*args)


def _run_trunk(x, mods, row0, pool_state, gla_state, pos0, w, *, tb_pool, tl_pool, tb_gla, tl_gla):
    n_pool, n_gla = (DEPTH + 1) // 2, DEPTH // 2
    new_pool = new_gla = None
    for li in range(DEPTH):
        j = li // 2
        if li % 2 == 0:
            x, new_pool = _pool_layer(
                x, mods, li, row0, w["norm_g"], w["pool_w_in"], w["pool_w_grp"],
                w["pool_scale"], w["pool_w_out"], pool_state, new_pool,
                layer=j, n_layers=n_pool, tb=tb_pool, tl=tl_pool, pos0=pos0)
        else:
            x, new_gla = _gla_layer(
                x, mods, li, row0, w["norm_g"], w["gla"], w["final_g"] if li == DEPTH - 1 else None,
                gla_state, new_gla, layer=j, n_layers=n_gla, tb=tb_gla, tl=tl_gla)
    return x, new_pool, new_gla


def kernel(x_prompt, x_sample, c_prompt, c_sample, state_pool, state_gla, ada_w, ada_b, norm_g, pool_w_in,
           pool_w_grp, pool_scale, pool_w_out, gla_w_in, gla_w_gate_up, gla_b_gate, gla_norm_g, gla_w_out,
           final_g):
    n_sample = x_sample.shape[0]
    n_gla = gla_w_in.shape[0]
    c_all = jnp.concatenate([c_sample, c_prompt], axis=0)
    mods = _ada_mods(c_all, ada_w, ada_b).reshape(DEPTH * 3, c_all.shape[0], D_MODEL)

    assert gla_w_in.shape[-1] == GLA_QKVZ + GLA_RANK
    w_gl_pad = jnp.concatenate(
        [gla_w_in[:, :, GLA_QKVZ:], jnp.zeros((n_gla, D_MODEL, GLA_GL_PAD - GLA_RANK), gla_w_in.dtype)], axis=-1)
    w_up_pad = jnp.concatenate(
        [gla_w_gate_up, jnp.zeros((n_gla, GLA_GL_PAD - GLA_RANK, GLA_DK), gla_w_gate_up.dtype)], axis=1)
    w = {
        "norm_g": norm_g.reshape(DEPTH, 1, D_MODEL),
        "pool_w_in": pool_w_in.astype(BF16),
        "pool_w_grp": pool_w_grp.astype(BF16),
        "pool_scale": pool_scale.reshape(-1, 1, D_INNER),
        "pool_w_out": pool_w_out.astype(BF16),
        "gla": {
            "w_in": gla_w_in.astype(BF16),
            "w_gl": w_gl_pad.astype(BF16),
            "w_up": w_up_pad.astype(BF16),
            "b_gate": gla_b_gate.reshape(-1, 1, GLA_DK),
            "norm_g": gla_norm_g.reshape(-1, 1, GLA_DV),
            "w_out": gla_w_out.astype(BF16),
        },
        "final_g": final_g.reshape(1, D_MODEL),
    }
    y_p, pool_p, gla_p = _run_trunk(x_prompt, mods, n_sample, None, None, 0, w,
                                    tb_pool=1, tl_pool=512, tb_gla=1, tl_gla=512)
    y_s, pool_s, gla_s = _run_trunk(x_sample, mods, 0, state_pool.transpose(0, 2, 1, 3), state_gla, PAST_LEN, w,
                                    tb_pool=32, tl_pool=x_sample.shape[1], tb_gla=8, tl_gla=x_sample.shape[1])
    return (y_p, y_s, pool_p, gla_p, pool_s.transpose(0, 2, 1, 3), gla_s)
```

```python
import functools

import jax
import jax.numpy as jnp
from jax import lax
from jax.experimental import pallas as pl
from jax.experimental.pallas import tpu as pltpu

D_MODEL = 1024
DEPTH = 4
PAST_LEN = 16384
D_INNER = 2048
POOL_WINDOWS = (2, 4, 8, 16)
POOL_GROUP = D_INNER // len(POOL_WINDOWS)
POOL_BUF = max(POOL_WINDOWS) - 1
POOL_HALO = POOL_BUF + 1
GLA_HEADS = 4
GLA_DK = 512
GLA_DV = D_INNER
GLA_HK = GLA_DK // GLA_HEADS
GLA_HV = GLA_DV // GLA_HEADS
GLA_RANK = 16
GLA_GATE_NORM = 16.0
GLA_CHUNK = 64
GLA_GL_PAD = 128
GLA_QKVZ = 2 * GLA_DK + 2 * GLA_DV
EPS = 1e-6
F32 = jnp.float32
BF16 = jnp.bfloat16

VMEM_LIMIT_BYTES = 62 * 1024 * 1024

_NT = (((1,), (1,)), ((), ()))
_TN = (((0,), (0,)), ((), ()))


def _silu(v):
    return v / (1.0 + jnp.exp(-v))


def _mod_norm(x, g, shift, scale):
    ms = jnp.mean(x * x, axis=-1, keepdims=True)
    y = x * lax.rsqrt(ms + EPS) * g
    return y * (1.0 + scale) + shift


def _const_spec(shape, lead=(), tail=None):
    idx = tuple(lead) + tuple(tail if tail is not None else (0,) * len(shape))
    return pl.BlockSpec((None,) * len(lead) + tuple(shape), lambda *_: idx, pipeline_mode=pl.Buffered(1))


def _ada_kernel(c_ref, w_ref, b_ref, o_ref):
    cs = _silu(c_ref[...]).astype(BF16)
    for j in range(3):
        cols = slice(j * D_MODEL, (j + 1) * D_MODEL)
        o_ref[0, j] = jnp.dot(cs, w_ref[0, :, cols].astype(BF16), preferred_element_type=F32) + b_ref[0, :, cols]


def _ada_mods(c_all, ada_w, ada_b):
    nb = c_all.shape[0]
    return pl.pallas_call(
        _ada_kernel,
        grid=(DEPTH,),
        in_specs=[
            pl.BlockSpec((nb, D_MODEL), lambda li: (0, 0)),
            pl.BlockSpec((1, D_MODEL, 3 * D_MODEL), lambda li: (li, 0, 0)),
            pl.BlockSpec((1, 1, 3 * D_MODEL), lambda li: (li, 0, 0)),
        ],
        out_specs=pl.BlockSpec((1, 3, nb, D_MODEL), lambda li: (li, 0, 0, 0)),
        out_shape=jax.ShapeDtypeStruct((DEPTH, 3, nb, D_MODEL), F32),
        compiler_params=pltpu.CompilerParams(vmem_limit_bytes=VMEM_LIMIT_BYTES),
        name="ada_mods",
    )(c_all, ada_w, ada_b.reshape(DEPTH, 1, 3 * D_MODEL))


def _pool_mix(ext_ref, uz, wgrp_ref, pscale_ref, gated_ref, *, halo, time_axis, rows, pos):
    m = uz.shape[0]

    def window(j, c0):
        idx = [slice(None)] * 3
        idx[time_axis] = slice(halo - j, halo - j + rows)
        idx[2] = slice(c0, c0 + POOL_GROUP)
        return ext_ref[tuple(idx)]

    diffs = []
    for gi, w in enumerate(POOL_WINDOWS):
        c0 = gi * POOL_GROUP
        u_g = window(0, c0)
        s = u_g
        for j in range(1, w):
            s = s + window(j, c0)
        mean = s * (1.0 / w) if pos is None else s / jnp.minimum(pos + 1, w).astype(F32)
        diffs.append((mean - u_g).reshape(m, POOL_GROUP).astype(BF16))
    mixed = [jnp.dot(diffs[gi], wgrp_ref[gi], preferred_element_type=F32) for gi in range(len(POOL_WINDOWS))]
    for gi in range(len(POOL_WINDOWS)):
        c0 = gi * POOL_GROUP
        z_g = uz[:, D_INNER + c0:D_INNER + c0 + POOL_GROUP]
        gated_ref[:, c0:c0 + POOL_GROUP] = (
            mixed[gi] * pscale_ref[:, c0:c0 + POOL_GROUP] * _silu(z_g)).astype(BF16)


def _pool_kernel(*refs, tl, n_l, row0, has_prev):
    it = iter(refs)
    x_ref = next(it)
    mod_refs = [next(it) for _ in range(3)]
    g_ref, win_ref, wgrp_ref, pscale_ref, wout_ref = (next(it) for _ in range(5))
    if has_prev:
        next(it)
    y_ref, nbuf_ref, ext_ref, gated_ref = (next(it) for _ in range(4))

    l = pl.program_id(1)

    @pl.when(l == 0)
    def _():
        ext_ref[:, 0:POOL_HALO, :] = jnp.zeros((1, POOL_HALO, D_INNER), F32)

    shift, scale, gate = _load_mods(mod_refs, 1, row0)
    x = x_ref[...]
    h = _mod_norm(x, g_ref[...], shift, scale).reshape(tl, D_MODEL).astype(BF16)
    uz = jnp.dot(h, win_ref[...], preferred_element_type=F32)
    ext_ref[:, POOL_HALO:POOL_HALO + tl, :] = uz[:, :D_INNER].reshape(1, tl, D_INNER)
    pos = l * tl + lax.broadcasted_iota(jnp.int32, (1, tl, POOL_GROUP), 1)
    _pool_mix(ext_ref, uz, wgrp_ref, pscale_ref, gated_ref, halo=POOL_HALO, time_axis=1, rows=tl, pos=pos)
    y = jnp.dot(gated_ref[...], wout_ref[...], preferred_element_type=F32)
    y_ref[...] = x + gate * y.reshape(1, tl, D_MODEL)

    @pl.when(l == n_l - 1)
    def _():
        nbuf_ref[0] = ext_ref[:, tl + 1:tl + POOL_HALO, :]

    if n_l > 1:
        @pl.when(l < n_l - 1)
        def _():
            ext_ref[:, 0:POOL_HALO, :] = ext_ref[:, tl:tl + POOL_HALO, :]


def _pool_cont_kernel(*refs, tb, tl, pos0, has_prev):
    it = iter(refs)
    x_ref = next(it)
    mod_refs = [next(it) for _ in range(3)]
    g_ref, win_ref, wgrp_ref, pscale_ref, wout_ref, buf_ref = (next(it) for _ in range(6))
    if has_prev:
        next(it)
    y_ref, nbuf_ref, ext_ref, gated_ref = (next(it) for _ in range(4))
    m = tb * tl

    shift, scale, gate = _load_mods(mod_refs, tb, 0)
    x = x_ref[...]
    h = jnp.transpose(_mod_norm(x, g_ref[...], shift, scale), (1, 0, 2))
    uz = jnp.dot(h.reshape(m, D_MODEL).astype(BF16), win_ref[...], preferred_element_type=F32)
    ext_ref[0:POOL_BUF] = buf_ref[...]
    ext_ref[POOL_BUF:POOL_BUF + tl] = uz[:, :D_INNER].reshape(tl, tb, D_INNER)
    pos = None
    if pos0 < POOL_BUF:
        pos = pos0 + lax.broadcasted_iota(jnp.int32, (tl, tb, POOL_GROUP), 0)
    _pool_mix(ext_ref, uz, wgrp_ref, pscale_ref, gated_ref, halo=POOL_BUF, time_axis=0, rows=tl, pos=pos)
    y = jnp.dot(gated_ref[...], wout_ref[...], preferred_element_type=F32)
    y_ref[...] = x + gate * jnp.transpose(y.reshape(tl, tb, D_MODEL), (1, 0, 2))
    nbuf_ref[...] = ext_ref[tl:tl + POOL_BUF]


MOD_ROWS = 8


def _mod_specs(tb, li, row0):
    rows = max(tb, MOD_ROWS)
    assert row0 % rows == 0 and rows % tb == 0
    return [pl.BlockSpec((None, rows, D_MODEL), lambda bi, *_, j=j: (li * 3 + j, (row0 + bi * tb) // rows, 0))
            for j in range(3)]


def _load_mods(refs, tb, row0):
    if tb >= MOD_ROWS:
        return [ref[...][:, None, :] for ref in refs]
    assert tb == 1
    r = (row0 + pl.program_id(0)) % MOD_ROWS
    return [ref[pl.ds(r, 1), :] for ref in refs]


def _pool_layer(x, mods, li, row0, norm_g, w_in, w_grp, pscale, w_out, state, prev_out, *, layer, n_layers,
                tb, tl, pos0):
    b, L, _ = x.shape
    fresh = state is None
    has_prev = prev_out is not None
    in_specs = [
        None,
        *_mod_specs(tb, li, row0),
        _const_spec((1, D_MODEL), lead=(li,)),
        _const_spec((D_MODEL, 2 * D_INNER), lead=(layer,)),
        _const_spec((len(POOL_WINDOWS), POOL_GROUP, POOL_GROUP), lead=(layer,)),
        _const_spec((1, D_INNER), lead=(layer,)),
        _const_spec((D_INNER, D_MODEL), lead=(layer,)),
    ]
    args = [x, mods, mods, mods, norm_g, w_in, w_grp, pscale, w_out]
    if fresh:
        assert tb == 1 and pos0 == 0
        n_l = L // tl
        grid = (b, n_l)
        x_spec = pl.BlockSpec((1, tl, D_MODEL), lambda bi, l: (bi, l, 0))
        buf_shape = (n_layers, b, POOL_BUF, D_INNER)
        buf_spec = pl.BlockSpec((1, 1, POOL_BUF, D_INNER), lambda bi, l: (layer, bi, 0, 0))
        ext_shape = (1, POOL_HALO + tl, D_INNER)
        kern = functools.partial(_pool_kernel, tl=tl, n_l=n_l, row0=row0, has_prev=has_prev)
        semantics = ("arbitrary", "arbitrary")
    else:
        assert tl == L and row0 == 0
        grid = (b // tb,)
        x_spec = pl.BlockSpec((tb, tl, D_MODEL), lambda bi: (bi, 0, 0))
        buf_shape = (n_layers, POOL_BUF, b, D_INNER)
        buf_spec = pl.BlockSpec((None, POOL_BUF, tb, D_INNER), lambda bi: (layer, 0, bi, 0))
        ext_shape = (POOL_BUF + tl, tb, D_INNER)
        in_specs.append(buf_spec)
        args.append(state)
        kern = functools.partial(_pool_cont_kernel, tb=tb, tl=tl, pos0=pos0, has_prev=has_prev)
        semantics = ("arbitrary",)
    in_specs[0] = x_spec
    aliases = {}
    if has_prev:
        in_specs.append(pl.BlockSpec(memory_space=pl.ANY))
        aliases = {len(args): 1}
        args.append(prev_out)
    return pl.pallas_call(
        kern,
        grid=grid,
        in_specs=in_specs,
        out_specs=[x_spec, buf_spec],
        out_shape=[jax.ShapeDtypeStruct(x.shape, F32), jax.ShapeDtypeStruct(buf_shape, F32)],
        scratch_shapes=[pltpu.VMEM(ext_shape, F32), pltpu.VMEM((tb * tl, D_INNER), BF16)],
        input_output_aliases=aliases,
        compiler_params=pltpu.CompilerParams(dimension_semantics=semantics, vmem_limit_bytes=VMEM_LIMIT_BYTES),
        name=f"pool_layer{layer}_{'fresh' if fresh else 'cont'}",
    )(*args)


def _gla_kernel(*refs, tb, tl, n_l, row0, fresh, has_prev, final):
    it = iter(refs)
    x_ref = next(it)
    mod_refs = [next(it) for _ in range(3)]
    g_ref, win_ref, wgl_ref, wup_ref, bg_ref, ng_ref, wout_ref = (next(it) for _ in range(7))
    fg_ref = next(it) if final else None
    s0_ref = None if fresh else next(it)
    if has_prev:
        next(it)
    y_ref, s_ref, gated_ref = (next(it) for _ in range(3))

    l = pl.program_id(1)
    m = tb * tl

    @pl.when(l == 0)
    def _():
        if fresh:
            s_ref[...] = jnp.zeros(s_ref.shape, F32)
        else:
            s_ref[...] = s0_ref[...]

    shift, scale, gate = _load_mods(mod_refs, tb, row0)
    x = x_ref[...]
    h = _mod_norm(x, g_ref[...], shift, scale).reshape(m, D_MODEL).astype(BF16)
    qk = jnp.dot(h, win_ref[:, :2 * GLA_DK], preferred_element_type=F32)
    gl = jnp.dot(h, wgl_ref[...], preferred_element_type=F32).astype(BF16)
    gate_pre = jnp.dot(gl, wup_ref[...], preferred_element_type=F32) + bg_ref[...]
    lg = (jnp.minimum(gate_pre, 0.0) - jnp.log(1.0 + jnp.exp(-jnp.abs(gate_pre)))) * (1.0 / GLA_GATE_NORM)

    def project_v():
        return jnp.dot(h, win_ref[:, 2 * GLA_DK:2 * GLA_DK + GLA_DV], preferred_element_type=F32).astype(BF16)

    def project_sz():
        return _silu(jnp.dot(h, win_ref[:, 2 * GLA_DK + GLA_DV:], preferred_element_type=F32))

    def finish_head(rows, hd, o_h, sz):
        o_h = o_h * lax.rsqrt(jnp.mean(o_h * o_h, axis=-1, keepdims=True) + EPS)
        hc = slice(hd * GLA_HV, (hd + 1) * GLA_HV)
        gated_ref[rows, hc] = (o_h * ng_ref[:, hc] * sz[rows, hc]).astype(BF16)

    def project_out(rows):
        y = jnp.dot(gated_ref[rows, :], wout_ref[...], preferred_element_type=F32)
        if tb == 1:
            out = x_ref[0, rows, :] + gate * y
        else:
            assert rows == slice(0, m)
            out = x + gate * y.reshape(tb, tl, D_MODEL)
        if final:
            out = out * lax.rsqrt(jnp.mean(out * out, axis=-1, keepdims=True) + EPS) * fg_ref[...]
        if tb == 1:
            y_ref[0, rows, :] = out
        else:
            y_ref[...] = out

    if tl >= 2 * GLA_CHUNK:
        _gla_long_chunks(qk, lg, project_v, project_sz, s_ref, finish_head, project_out, m=m, tl=tl)
    else:
        _gla_short_chunks(qk, lg, project_v, project_sz, s_ref, finish_head, project_out, m=m, tl=tl)


def _prefix_sums(pat, lg_rows):
    hi = lg_rows.astype(BF16)
    lo = (lg_rows - hi.astype(F32)).astype(BF16)
    return jnp.dot(pat, hi, preferred_element_type=F32) + jnp.dot(pat, lo, preferred_element_type=F32)


def _decay_cols(decay_row):
    d_col = jnp.transpose(jnp.broadcast_to(decay_row, (GLA_HK, GLA_HK)))
    return jnp.concatenate([d_col] * (GLA_HV // GLA_HK), axis=1)


def _gla_long_chunks(qk, lg, project_v, project_sz, s_ref, finish_head, project_out, *, m, tl):
    hb = GLA_CHUNK
    cr = 2 * hb
    ri = lax.broadcasted_iota(jnp.int32, (cr, cr), 0)
    ci = lax.broadcasted_iota(jnp.int32, (cr, cr), 1)
    causal = ci <= ri
    tri = jnp.where(causal, 1.0, 0.0).astype(BF16)
    zeros = jnp.zeros((hb, GLA_HK), BF16)
    heads = range(GLA_HEADS)
    kcs = [slice(hd * GLA_HK, (hd + 1) * GLA_HK) for hd in heads]

    def prepare(c):
        rows = slice(c * cr, (c + 1) * cr)
        bcum = _prefix_sums(tri, lg[rows])
        b_a, b_b = bcum[:hb], bcum[hb:]
        ref_a, bnd = bcum[hb // 2 - 1:hb // 2], bcum[hb - 1:hb]
        ref_b, tot = bcum[hb + hb // 2 - 1:hb + hb // 2], bcum[cr - 1:cr]
        q = qk[rows, 0:GLA_DK] * (GLA_HK ** -0.5)
        k = qk[rows, GLA_DK:2 * GLA_DK]
        q_a, q_b, k_a, k_b = q[:hb], q[hb:], k[:hb], k[hb:]
        qi_a = (q_a * jnp.exp(b_a - ref_a)).astype(BF16)
        ki_a = (k_a * jnp.exp(ref_a - b_a)).astype(BF16)
        qi_b = (q_b * jnp.exp(b_b - ref_b)).astype(BF16)
        ki_b = (k_b * jnp.exp(ref_b - b_b)).astype(BF16)
        qx_b = (q_b * jnp.exp(b_b - bnd)).astype(BF16)
        kx_a = (k_a * jnp.exp(bnd - b_a)).astype(BF16)
        q_dec = (q * jnp.exp(bcum)).astype(BF16)
        k_state = (k * jnp.exp(tot - bcum)).astype(BF16)
        decay = jnp.exp(tot)
        return qi_a, ki_a, qi_b, ki_b, qx_b, kx_a, q_dec, k_state, decay

    sz = project_sz()
    prepared = [prepare(c) for c in range(m // cr)]
    v_all = project_v()
    for c, (qi_a, ki_a, qi_b, ki_b, qx_b, kx_a, q_dec, k_state, decay) in enumerate(prepared):
        rows = slice(c * cr, (c + 1) * cr)
        si = (c * cr) // tl
        v = [v_all[rows, hd * GLA_HV:(hd + 1) * GLA_HV] for hd in heads]
        s_old = [s_ref[0, si, hd] for hd in heads]
        att_a = [lax.dot_general(qi_a[:, kc], jnp.concatenate([ki_a[:, kc], zeros], axis=0), _NT,
                                 preferred_element_type=F32) for kc in kcs]
        att_b = [lax.dot_general(
            jnp.concatenate([qx_b[:, kc], qi_b[:, kc]], axis=1),
            jnp.concatenate([jnp.concatenate([kx_a[:, kc], zeros], axis=1),
                             jnp.concatenate([zeros, ki_b[:, kc]], axis=1)], axis=0),
            _NT, preferred_element_type=F32) for kc in kcs]
        upd = [lax.dot_general(k_state[:, kc], v[hd], _TN, preferred_element_type=F32)
               for hd, kc in enumerate(kcs)]
        att = [jnp.where(causal, jnp.concatenate([att_a[hd], att_b[hd]], axis=0), 0.0).astype(BF16)
               for hd in heads]
        o = [jnp.dot(jnp.concatenate([att[hd], q_dec[:, kc]], axis=1),
                     jnp.concatenate([v[hd], s_old[hd].astype(BF16)], axis=0), preferred_element_type=F32)
             for hd, kc in enumerate(kcs)]
        for hd, kc in enumerate(kcs):
            s_ref[0, si, hd] = s_old[hd] * _decay_cols(decay[:, kc]) + upd[hd]
        for hd in heads:
            finish_head(rows, hd, o[hd], sz)
    project_out(slice(0, m))


def _gla_short_chunks(qk, lg, project_v, project_sz, s_ref, finish_head, project_out, *, m, tl):
    seg = tl
    n_seg = GLA_CHUNK // seg
    v_all = project_v()
    sz = project_sz()
    ri = lax.broadcasted_iota(jnp.int32, (GLA_CHUNK, GLA_CHUNK), 0)
    ci = lax.broadcasted_iota(jnp.int32, (GLA_CHUNK, GLA_CHUNK), 1)
    sh = seg.bit_length() - 1
    same = (ri >> sh) == (ci >> sh)
    causal = same & (ci <= ri)
    mid = same & ((ci & (seg - 1)) <= (seg // 2 - 1))
    pat = jnp.concatenate([jnp.where(mk, 1.0, 0.0).astype(BF16) for mk in (causal, mid, same)], axis=0)

    for c in range(m // GLA_CHUNK):
        r0 = c * GLA_CHUNK
        rows = slice(r0, r0 + GLA_CHUNK)
        sums = _prefix_sums(pat, lg[rows])
        bcum, bref, btot = sums[:GLA_CHUNK], sums[GLA_CHUNK:2 * GLA_CHUNK], sums[2 * GLA_CHUNK:]
        q = qk[rows, 0:GLA_DK] * (GLA_HK ** -0.5)
        k = qk[rows, GLA_DK:2 * GLA_DK]
        q_intra = (q * jnp.exp(bcum - bref)).astype(BF16)
        k_intra = (k * jnp.exp(bref - bcum)).astype(BF16)
        q_dec = (q * jnp.exp(bcum)).astype(BF16)
        k_state = (k * jnp.exp(btot - bcum)).astype(BF16)
        decay = jnp.exp(btot)
        for hd in range(GLA_HEADS):
            kc = slice(hd * GLA_HK, (hd + 1) * GLA_HK)
            v_h = v_all[rows, hd * GLA_HV:(hd + 1) * GLA_HV]
            att = lax.dot_general(q_intra[:, kc], k_intra[:, kc], _NT, preferred_element_type=F32)
            att = jnp.where(causal, att, 0.0).astype(BF16)
            o_h = jnp.dot(att, v_h, preferred_element_type=F32)
            o_inter = []
            for sg in range(n_seg):
                srows = slice(sg * seg, (sg + 1) * seg)
                si = (r0 + sg * seg) // tl
                s_old = s_ref[0, si, hd]
                o_inter.append(jnp.dot(q_dec[srows, kc], s_old.astype(BF16), preferred_element_type=F32))
                upd = lax.dot_general(k_state[srows, kc], v_h[srows], _TN, preferred_element_type=F32)
                s_ref[0, si, hd] = s_old * _decay_cols(decay[sg * seg:sg * seg + 1, kc]) + upd
            finish_head(rows, hd, o_h + jnp.concatenate(o_inter, axis=0), sz)
    project_out(slice(0, m))


def _gla_layer(x, mods, li, row0, norm_g, gw, final_g, state, prev_out, *, layer, n_layers, tb, tl):
    b, L, _ = x.shape
    n_l = L // tl
    fresh = state is None
    has_prev = prev_out is not None
    final = final_g is not None
    m = tb * tl
    seq = lambda bi, l: (bi, l, 0)
    st_block = (1, tb, GLA_HEADS, GLA_HK, GLA_HV)
    st_map = lambda bi, l: (layer, bi, 0, 0, 0)
    in_specs = [
        pl.BlockSpec((tb, tl, D_MODEL), seq),
        *_mod_specs(tb, li, row0),
        _const_spec((1, D_MODEL), lead=(li,)),
        _const_spec((D_MODEL, GLA_QKVZ), lead=(layer,)),
        _const_spec((D_MODEL, GLA_GL_PAD), lead=(layer,)),
        _const_spec((GLA_GL_PAD, GLA_DK), lead=(layer,)),
        _const_spec((1, GLA_DK), lead=(layer,)),
        _const_spec((1, GLA_DV), lead=(layer,)),
        _const_spec((GLA_DV, D_MODEL), lead=(layer,)),
    ]
    args = [x, mods, mods, mods, norm_g, gw["w_in"], gw["w_gl"], gw["w_up"], gw["b_gate"], gw["norm_g"],
            gw["w_out"]]
    if final:
        in_specs.append(_const_spec((1, D_MODEL)))
        args.append(final_g)
    if not fresh:
        in_specs.append(pl.BlockSpec(st_block, st_map))
        args.append(state)
    aliases = {}
    if has_prev:
        in_specs.append(pl.BlockSpec(memory_space=pl.ANY))
        aliases = {len(args): 1}
        args.append(prev_out)
    kern = functools.partial(_gla_kernel, tb=tb, tl=tl, n_l=n_l, row0=row0, fresh=fresh, has_prev=has_prev,
                             final=final)
    return pl.pallas_call(
        kern,
        grid=(b // tb, n_l),
        in_specs=in_specs,
        out_specs=[pl.BlockSpec((tb, tl, D_MODEL), seq), pl.BlockSpec(st_block, st_map)],
        out_shape=[
            jax.ShapeDtypeStruct(x.shape, F32),
            jax.ShapeDtypeStruct((n_layers, b, GLA_HEADS, GLA_HK, GLA_HV), F32),
        ],
        scratch_shapes=[pltpu.VMEM((m, GLA_DV), BF16)],
        input_output_aliases=aliases,
        compiler_params=pltpu.CompilerParams(
            dimension_semantics=("arbitrary", "arbitrary"), vmem_limit_bytes=VMEM_LIMIT_BYTES),
        name=f"gla_layer{layer}_{'fresh' if fresh else 'cont'}",
    )(*args)


def _run_trunk(x, mods, row0, pool_state, gla_state, pos0, w, *, tb_pool, tl_pool, tb_gla, tl_gla):
    n_pool, n_gla = (DEPTH + 1) // 2, DEPTH // 2
    new_pool = new_gla = None
    for li in range(DEPTH):
        j = li // 2
        if li % 2 == 0:
            x, new_pool = _pool_layer(
                x, mods, li, row0, w["norm_g"], w["pool_w_in"], w["pool_w_grp"],
                w["pool_scale"], w["pool_w_out"], pool_state, new_pool,
                layer=j, n_layers=n_pool, tb=tb_pool, tl=tl_pool, pos0=pos0)
        else:
            x, new_gla = _gla_layer(
                x, mods, li, row0, w["norm_g"], w["gla"], w["final_g"] if li == DEPTH - 1 else None,
                gla_state, new_gla, layer=j, n_layers=n_gla, tb=tb_gla, tl=tl_gla)
    return x, new_pool, new_gla


def kernel(x_prompt, x_sample, c_prompt, c_sample, state_pool, state_gla, ada_w, ada_b, norm_g, pool_w_in,
           pool_w_grp, pool_scale, pool_w_out, gla_w_in, gla_w_gate_up, gla_b_gate, gla_norm_g, gla_w_out,
           final_g):
    n_sample = x_sample.shape[0]
    n_gla = gla_w_in.shape[0]
    c_all = jnp.concatenate([c_sample, c_prompt], axis=0)
    mods = _ada_mods(c_all, ada_w, ada_b).reshape(DEPTH * 3, c_all.shape[0], D_MODEL)

    assert gla_w_in.shape[-1] == GLA_QKVZ + GLA_RANK
    w_gl_pad = jnp.concatenate(
        [gla_w_in[:, :, GLA_QKVZ:], jnp.zeros((n_gla, D_MODEL, GLA_GL_PAD - GLA_RANK), gla_w_in.dtype)], axis=-1)
    w_up_pad = jnp.concatenate(
        [gla_w_gate_up, jnp.zeros((n_gla, GLA_GL_PAD - GLA_RANK, GLA_DK), gla_w_gate_up.dtype)], axis=1)
    w = {
        "norm_g": norm_g.reshape(DEPTH, 1, D_MODEL),
        "pool_w_in": pool_w_in.astype(BF16),
        "pool_w_grp": pool_w_grp.astype(BF16),
        "pool_scale": pool_scale.reshape(-1, 1, D_INNER),
        "pool_w_out": pool_w_out.astype(BF16),
        "gla": {
            "w_in": gla_w_in.astype(BF16),
            "w_gl": w_gl_pad.astype(BF16),
            "w_up": w_up_pad.astype(BF16),
            "b_gate": gla_b_gate.reshape(-1, 1, GLA_DK),
            "norm_g": gla_norm_g.reshape(-1, 1, GLA_DV),
            "w_out": gla_w_out.astype(BF16),
        },
        "final_g": final_g.reshape(1, D_MODEL),
    }
    y_p, pool_p, gla_p = _run_trunk(x_prompt, mods, n_sample, None, None, 0, w,
                                    tb_pool=1, tl_pool=512, tb_gla=1, tl_gla=512)
    y_s, pool_s, gla_s = _run_trunk(x_sample, mods, 0, state_pool.transpose(0, 2, 1, 3), state_gla, PAST_LEN, w,
                                    tb_pool=32, tl_pool=x_sample.shape[1], tb_gla=8, tl_gla=x_sample.shape[1])
    return (y_p, y_s, pool_p, gla_p, pool_s.transpose(0, 2, 1, 3), gla_s)
```

```python
import functools

import jax
import jax.numpy as jnp
from jax import lax
from jax.experimental import pallas as pl
from jax.experimental.pallas import tpu as pltpu

D_MODEL = 1024
DEPTH = 4
PAST_LEN = 16384
D_INNER = 2048
POOL_WINDOWS = (2, 4, 8, 16)
POOL_GROUP = D_INNER // len(POOL_WINDOWS)
POOL_BUF = max(POOL_WINDOWS) - 1
POOL_HALO = POOL_BUF + 1
GLA_HEADS = 4
GLA_DK = 512
GLA_DV = D_INNER
GLA_HK = GLA_DK // GLA_HEADS
GLA_HV = GLA_DV // GLA_HEADS
GLA_RANK = 16
GLA_GATE_NORM = 16.0
GLA_CHUNK = 64
GLA_GL_PAD = 128
GLA_QKVZ = 2 * GLA_DK + 2 * GLA_DV
EPS = 1e-6
F32 = jnp.float32
BF16 = jnp.bfloat16

VMEM_LIMIT_BYTES = 62 * 1024 * 1024

_NT = (((1,), (1,)), ((), ()))
_TN = (((0,), (0,)), ((), ()))


def _silu(v):
    return v / (1.0 + jnp.exp(-v))


def _mod_norm(x, g, shift, scale):
    ms = jnp.mean(x * x, axis=-1, keepdims=True)
    y = x * lax.rsqrt(ms + EPS) * g
    return y * (1.0 + scale) + shift


def _const_spec(shape, lead=(), tail=None):
    idx = tuple(lead) + tuple(tail if tail is not None else (0,) * len(shape))
    return pl.BlockSpec((None,) * len(lead) + tuple(shape), lambda *_: idx, pipeline_mode=pl.Buffered(1))


def _ada_kernel(c_ref, w_ref, b_ref, o_ref):
    cs = _silu(c_ref[...]).astype(BF16)
    for j in range(3):
        cols = slice(j * D_MODEL, (j + 1) * D_MODEL)
        o_ref[0, j] = jnp.dot(cs, w_ref[0, :, cols].astype(BF16), preferred_element_type=F32) + b_ref[0, :, cols]


def _ada_mods(c_all, ada_w, ada_b):
    nb = c_all.shape[0]
    return pl.pallas_call(
        _ada_kernel,
        grid=(DEPTH,),
        in_specs=[
            pl.BlockSpec((nb, D_MODEL), lambda li: (0, 0)),
            pl.BlockSpec((1, D_MODEL, 3 * D_MODEL), lambda li: (li, 0, 0)),
            pl.BlockSpec((1, 1, 3 * D_MODEL), lambda li: (li, 0, 0)),
        ],
        out_specs=pl.BlockSpec((1, 3, nb, D_MODEL), lambda li: (li, 0, 0, 0)),
        out_shape=jax.ShapeDtypeStruct((DEPTH, 3, nb, D_MODEL), F32),
        compiler_params=pltpu.CompilerParams(vmem_limit_bytes=VMEM_LIMIT_BYTES),
        name="ada_mods",
    )(c_all, ada_w, ada_b.reshape(DEPTH, 1, 3 * D_MODEL))


def _pool_mix(ext_ref, uz, wgrp_ref, pscale_ref, gated_ref, *, halo, time_axis, rows, pos):
    m = uz.shape[0]

    def window(j, c0):
        idx = [slice(None)] * 3
        idx[time_axis] = slice(halo - j, halo - j + rows)
        idx[2] = slice(c0, c0 + POOL_GROUP)
        return ext_ref[tuple(idx)]

    diffs = []
    for gi, w in enumerate(POOL_WINDOWS):
        c0 = gi * POOL_GROUP
        u_g = window(0, c0)
        s = u_g
        for j in range(1, w):
            s = s + window(j, c0)
        mean = s * (1.0 / w) if pos is None else s / jnp.minimum(pos + 1, w).astype(F32)
        diffs.append((mean - u_g).reshape(m, POOL_GROUP).astype(BF16))
    mixed = [jnp.dot(diffs[gi], wgrp_ref[gi], preferred_element_type=F32) for gi in range(len(POOL_WINDOWS))]
    for gi in range(len(POOL_WINDOWS)):
        c0 = gi * POOL_GROUP
        z_g = uz[:, D_INNER + c0:D_INNER + c0 + POOL_GROUP]
        gated_ref[:, c0:c0 + POOL_GROUP] = (
            mixed[gi] * pscale_ref[:, c0:c0 + POOL_GROUP] * _silu(z_g)).astype(BF16)


def _pool_kernel(*refs, tl, n_l, row0, has_prev):
    it = iter(refs)
    x_ref = next(it)
    mod_refs = [next(it) for _ in range(3)]
    g_ref, win_ref, wgrp_ref, pscale_ref, wout_ref = (next(it) for _ in range(5))
    if has_prev:
        next(it)
    y_ref, nbuf_ref, ext_ref, gated_ref = (next(it) for _ in range(4))

    l = pl.program_id(1)

    @pl.when(l == 0)
    def _():
        ext_ref[:, 0:POOL_HALO, :] = jnp.zeros((1, POOL_HALO, D_INNER), F32)

    shift, scale, gate = _load_mods(mod_refs, 1, row0)
    x = x_ref[...]
    h = _mod_norm(x, g_ref[...], shift, scale).reshape(tl, D_MODEL).astype(BF16)
    uz = jnp.dot(h, win_ref[...], preferred_element_type=F32)
    ext_ref[:, POOL_HALO:POOL_HALO + tl, :] = uz[:, :D_INNER].reshape(1, tl, D_INNER)
    pos = l * tl + lax.broadcasted_iota(jnp.int32, (1, tl, POOL_GROUP), 1)
    _pool_mix(ext_ref, uz, wgrp_ref, pscale_ref, gated_ref, halo=POOL_HALO, time_axis=1, rows=tl, pos=pos)
    y = jnp.dot(gated_ref[...], wout_ref[...], preferred_element_type=F32)
    y_ref[...] = x + gate * y.reshape(1, tl, D_MODEL)

    @pl.when(l == n_l - 1)
    def _():
        nbuf_ref[0] = ext_ref[:, tl + 1:tl + POOL_HALO, :]

    if n_l > 1:
        @pl.when(l < n_l - 1)
        def _():
            ext_ref[:, 0:POOL_HALO, :] = ext_ref[:, tl:tl + POOL_HALO, :]


def _pool_cont_kernel(*refs, tb, tl, pos0, has_prev):
    it = iter(refs)
    x_ref = next(it)
    mod_refs = [next(it) for _ in range(3)]
    g_ref, win_ref, wgrp_ref, pscale_ref, wout_ref, buf_ref = (next(it) for _ in range(6))
    if has_prev:
        next(it)
    y_ref, nbuf_ref, ext_ref, gated_ref = (next(it) for _ in range(4))
    m = tb * tl

    shift, scale, gate = _load_mods(mod_refs, tb, 0)
    x = x_ref[...]
    h = jnp.transpose(_mod_norm(x, g_ref[...], shift, scale), (1, 0, 2))
    uz = jnp.dot(h.reshape(m, D_MODEL).astype(BF16), win_ref[...], preferred_element_type=F32)
    ext_ref[0:POOL_BUF] = buf_ref[...]
    ext_ref[POOL_BUF:POOL_BUF + tl] = uz[:, :D_INNER].reshape(tl, tb, D_INNER)
    pos = None
    if pos0 < POOL_BUF:
        pos = pos0 + lax.broadcasted_iota(jnp.int32, (tl, tb, POOL_GROUP), 0)
    _pool_mix(ext_ref, uz, wgrp_ref, pscale_ref, gated_ref, halo=POOL_BUF, time_axis=0, rows=tl, pos=pos)
    y = jnp.dot(gated_ref[...], wout_ref[...], preferred_element_type=F32)
    y_ref[...] = x + gate * jnp.transpose(y.reshape(tl, tb, D_MODEL), (1, 0, 2))
    nbuf_ref[...] = ext_ref[tl:tl + POOL_BUF]


MOD_ROWS = 8


def _mod_specs(tb, li, row0):
    rows = max(tb, MOD_ROWS)
    assert row0 % rows == 0 and rows % tb == 0
    return [pl.BlockSpec((None, rows, D_MODEL), lambda bi, *_, j=j: (li * 3 + j, (row0 + bi * tb) // rows, 0))
            for j in range(3)]


def _load_mods(refs, tb, row0):
    if tb >= MOD_ROWS:
        return [ref[...][:, None, :] for ref in refs]
    assert tb == 1
    r = (row0 + pl.program_id(0)) % MOD_ROWS
    return [ref[pl.ds(r, 1), :] for ref in refs]


def _pool_layer(x, mods, li, row0, norm_g, w_in, w_grp, pscale, w_out, state, prev_out, *, layer, n_layers,
                tb, tl, pos0):
    b, L, _ = x.shape
    fresh = state is None
    has_prev = prev_out is not None
    in_specs = [
        None,
        *_mod_specs(tb, li, row0),
        _const_spec((1, D_MODEL), lead=(li,)),
        _const_spec((D_MODEL, 2 * D_INNER), lead=(layer,)),
        _const_spec((len(POOL_WINDOWS), POOL_GROUP, POOL_GROUP), lead=(layer,)),
        _const_spec((1, D_INNER), lead=(layer,)),
        _const_spec((D_INNER, D_MODEL), lead=(layer,)),
    ]
    args = [x, mods, mods, mods, norm_g, w_in, w_grp, pscale, w_out]
    if fresh:
        assert tb == 1 and pos0 == 0
        n_l = L // tl
        grid = (b, n_l)
        x_spec = pl.BlockSpec((1, tl, D_MODEL), lambda bi, l: (bi, l, 0))
        buf_shape = (n_layers, b, POOL_BUF, D_INNER)
        buf_spec = pl.BlockSpec((1, 1, POOL_BUF, D_INNER), lambda bi, l: (layer, bi, 0, 0))
        ext_shape = (1, POOL_HALO + tl, D_INNER)
        kern = functools.partial(_pool_kernel, tl=tl, n_l=n_l, row0=row0, has_prev=has_prev)
        semantics = ("arbitrary", "arbitrary")
    else:
        assert tl == L and row0 == 0
        grid = (b // tb,)
        x_spec = pl.BlockSpec((tb, tl, D_MODEL), lambda bi: (bi, 0, 0))
        buf_shape = (n_layers, POOL_BUF, b, D_INNER)
        buf_spec = pl.BlockSpec((None, POOL_BUF, tb, D_INNER), lambda bi: (layer, 0, bi, 0))
        ext_shape = (POOL_BUF + tl, tb, D_INNER)
        in_specs.append(buf_spec)
        args.append(state)
        kern = functools.partial(_pool_cont_kernel, tb=tb, tl=tl, pos0=pos0, has_prev=has_prev)
        semantics = ("arbitrary",)
    in_specs[0] = x_spec
    aliases = {}
    if has_prev:
        in_specs.append(pl.BlockSpec(memory_space=pl.ANY))
        aliases = {len(args): 1}
        args.append(prev_out)
    return pl.pallas_call(
        kern,
        grid=grid,
        in_specs=in_specs,
        out_specs=[x_spec, buf_spec],
        out_shape=[jax.ShapeDtypeStruct(x.shape, F32), jax.ShapeDtypeStruct(buf_shape, F32)],
        scratch_shapes=[pltpu.VMEM(ext_shape, F32), pltpu.VMEM((tb * tl, D_INNER), BF16)],
        input_output_aliases=aliases,
        compiler_params=pltpu.CompilerParams(dimension_semantics=semantics, vmem_limit_bytes=VMEM_LIMIT_BYTES),
        name=f"pool_layer{layer}_{'fresh' if fresh else 'cont'}",
    )(*args)


def _gla_kernel(*refs, tb, tl, n_l, row0, fresh, has_prev, final):
    it = iter(refs)
    x_ref = next(it)
    mod_refs = [next(it) for _ in range(3)]
    g_ref, win_ref, wgl_ref, wup_ref, bg_ref, ng_ref, wout_ref = (next(it) for _ in range(7))
    fg_ref = next(it) if final else None
    s0_ref = None if fresh else next(it)
    if has_prev:
        next(it)
    y_ref, s_ref, gated_ref = (next(it) for _ in range(3))

    l = pl.program_id(1)
    m = tb * tl

    @pl.when(l == 0)
    def _():
        if fresh:
            s_ref[...] = jnp.zeros(s_ref.shape, F32)
        else:
            s_ref[...] = s0_ref[...]

    shift, scale, gate = _load_mods(mod_refs, tb, row0)
    x = x_ref[...]
    h = _mod_norm(x, g_ref[...], shift, scale).reshape(m, D_MODEL).astype(BF16)
    qk = jnp.dot(h, win_ref[:, :2 * GLA_DK], preferred_element_type=F32)
    gl = jnp.dot(h, wgl_ref[...], preferred_element_type=F32).astype(BF16)
    gate_pre = jnp.dot(gl, wup_ref[...], preferred_element_type=F32) + bg_ref[...]
    lg = (jnp.minimum(gate_pre, 0.0) - jnp.log(1.0 + jnp.exp(-jnp.abs(gate_pre)))) * (1.0 / GLA_GATE_NORM)

    def project_v():
        return jnp.dot(h, win_ref[:, 2 * GLA_DK:2 * GLA_DK + GLA_DV], preferred_element_type=F32).astype(BF16)

    def project_sz():
        return _silu(jnp.dot(h, win_ref[:, 2 * GLA_DK + GLA_DV:], preferred_element_type=F32)).astype(BF16)

    def finish_head(rows, hd, o_h, sz):
        o_h = o_h * lax.rsqrt(jnp.mean(o_h * o_h, axis=-1, keepdims=True) + EPS)
        hc = slice(hd * GLA_HV, (hd + 1) * GLA_HV)
        gated_ref[rows, hc] = o_h.astype(BF16) * ng_ref[:, hc].astype(BF16) * sz[rows, hc]

    def project_out(rows):
        y = jnp.dot(gated_ref[rows, :], wout_ref[...], preferred_element_type=F32)
        if tb == 1:
            out = x_ref[0, rows, :] + gate * y
        else:
            assert rows == slice(0, m)
            out = x + gate * y.reshape(tb, tl, D_MODEL)
        if final:
            out = out * lax.rsqrt(jnp.mean(out * out, axis=-1, keepdims=True) + EPS) * fg_ref[...]
        if tb == 1:
            y_ref[0, rows, :] = out
        else:
            y_ref[...] = out

    if tl >= 2 * GLA_CHUNK:
        _gla_long_chunks(qk, lg, project_v, project_sz, s_ref, finish_head, project_out, m=m, tl=tl)
    else:
        _gla_short_chunks(qk, lg, project_v, project_sz, s_ref, finish_head, project_out, m=m, tl=tl)


def _prefix_sums(pat, lg_rows):
    hi = lg_rows.astype(BF16)
    lo = (lg_rows - hi.astype(F32)).astype(BF16)
    return jnp.dot(jnp.concatenate([pat, pat], axis=1), jnp.concatenate([hi, lo], axis=0),
                   preferred_element_type=F32)


def _decay_cols(decay_row):
    d_col = jnp.transpose(jnp.broadcast_to(decay_row, (GLA_HK, GLA_HK)))
    return jnp.concatenate([d_col] * (GLA_HV // GLA_HK), axis=1)


def _gla_long_chunks(qk, lg, project_v, project_sz, s_ref, finish_head, project_out, *, m, tl):
    hb = GLA_CHUNK
    cr = 2 * hb
    ri = lax.broadcasted_iota(jnp.int32, (cr, cr), 0)
    ci = lax.broadcasted_iota(jnp.int32, (cr, cr), 1)
    causal = ci <= ri
    tri = jnp.where(causal, 1.0, 0.0).astype(BF16)
    zeros = jnp.zeros((hb, GLA_HK), BF16)
    heads = range(GLA_HEADS)
    kcs = [slice(hd * GLA_HK, (hd + 1) * GLA_HK) for hd in heads]

    def prepare(c):
        rows = slice(c * cr, (c + 1) * cr)
        bcum = _prefix_sums(tri, lg[rows])
        b_a, b_b = bcum[:hb], bcum[hb:]
        ref_a, bnd = bcum[hb // 2 - 1:hb // 2], bcum[hb - 1:hb]
        ref_b, tot = bcum[hb + hb // 2 - 1:hb + hb // 2], bcum[cr - 1:cr]
        q = qk[rows, 0:GLA_DK] * (GLA_HK ** -0.5)
        k = qk[rows, GLA_DK:2 * GLA_DK]
        q_a, q_b, k_a, k_b = q[:hb], q[hb:], k[:hb], k[hb:]
        qi_a = (q_a * jnp.exp(b_a - ref_a)).astype(BF16)
        ki_a = (k_a * jnp.exp(ref_a - b_a)).astype(BF16)
        qi_b = (q_b * jnp.exp(b_b - ref_b)).astype(BF16)
        ki_b = (k_b * jnp.exp(ref_b - b_b)).astype(BF16)
        qx_b = (q_b * jnp.exp(b_b - bnd)).astype(BF16)
        kx_a = (k_a * jnp.exp(bnd - b_a)).astype(BF16)
        q_dec = (q * jnp.exp(bcum)).astype(BF16)
        k_state = (k * jnp.exp(tot - bcum)).astype(BF16)
        decay = jnp.exp(tot)
        return qi_a, ki_a, qi_b, ki_b, qx_b, kx_a, q_dec, k_state, decay

    sz = project_sz()
    prepared = [prepare(c) for c in range(m // cr)]
    v_all = project_v()
    for c, (qi_a, ki_a, qi_b, ki_b, qx_b, kx_a, q_dec, k_state, decay) in enumerate(prepared):
        rows = slice(c * cr, (c + 1) * cr)
        si = (c * cr) // tl
        v = [v_all[rows, hd * GLA_HV:(hd + 1) * GLA_HV] for hd in heads]
        s_old = [s_ref[0, si, hd] for hd in heads]
        att_a = [lax.dot_general(qi_a[:, kc], jnp.concatenate([ki_a[:, kc], zeros], axis=0), _NT,
                                 preferred_element_type=F32) for kc in kcs]
        att_b = [lax.dot_general(
            jnp.concatenate([qx_b[:, kc], qi_b[:, kc]], axis=1),
            jnp.concatenate([jnp.concatenate([kx_a[:, kc], zeros], axis=1),
                             jnp.concatenate([zeros, ki_b[:, kc]], axis=1)], axis=0),
            _NT, preferred_element_type=F32) for kc in kcs]
        upd = [lax.dot_general(k_state[:, kc], v[hd], _TN, preferred_element_type=F32)
               for hd, kc in enumerate(kcs)]
        att = [jnp.where(causal, jnp.concatenate([att_a[hd], att_b[hd]], axis=0), 0.0).astype(BF16)
               for hd in heads]
        o = [jnp.dot(jnp.concatenate([att[hd], q_dec[:, kc]], axis=1),
                     jnp.concatenate([v[hd], s_old[hd].astype(BF16)], axis=0), preferred_element_type=F32)
             for hd, kc in enumerate(kcs)]
        for hd, kc in enumerate(kcs):
            s_ref[0, si, hd] = s_old[hd] * _decay_cols(decay[:, kc]) + upd[hd]
        for hd in heads:
            finish_head(rows, hd, o[hd], sz)
    project_out(slice(0, m))


def _gla_short_chunks(qk, lg, project_v, project_sz, s_ref, finish_head, project_out, *, m, tl):
    seg = tl
    n_seg = GLA_CHUNK // seg
    v_all = project_v()
    sz = project_sz()
    ri = lax.broadcasted_iota(jnp.int32, (GLA_CHUNK, GLA_CHUNK), 0)
    ci = lax.broadcasted_iota(jnp.int32, (GLA_CHUNK, GLA_CHUNK), 1)
    sh = seg.bit_length() - 1
    same = (ri >> sh) == (ci >> sh)
    causal = same & (ci <= ri)
    mid = same & ((ci & (seg - 1)) <= (seg // 2 - 1))
    pat = jnp.concatenate([jnp.where(mk, 1.0, 0.0).astype(BF16) for mk in (causal, mid, same)], axis=0)

    for c in range(m // GLA_CHUNK):
        r0 = c * GLA_CHUNK
        rows = slice(r0, r0 + GLA_CHUNK)
        sums = _prefix_sums(pat, lg[rows])
        bcum, bref, btot = sums[:GLA_CHUNK], sums[GLA_CHUNK:2 * GLA_CHUNK], sums[2 * GLA_CHUNK:]
        q = qk[rows, 0:GLA_DK] * (GLA_HK ** -0.5)
        k = qk[rows, GLA_DK:2 * GLA_DK]
        q_intra = (q * jnp.exp(bcum - bref)).astype(BF16)
        k_intra = (k * jnp.exp(bref - bcum)).astype(BF16)
        q_dec = (q * jnp.exp(bcum)).astype(BF16)
        k_state = (k * jnp.exp(btot - bcum)).astype(BF16)
        decay = jnp.exp(btot)
        for hd in range(GLA_HEADS):
            kc = slice(hd * GLA_HK, (hd + 1) * GLA_HK)
            v_h = v_all[rows, hd * GLA_HV:(hd + 1) * GLA_HV]
            att = lax.dot_general(q_intra[:, kc], k_intra[:, kc], _NT, preferred_element_type=F32)
            att = jnp.where(causal, att, 0.0).astype(BF16)
            o_h = jnp.dot(att, v_h, preferred_element_type=F32)
            o_inter = []
            for sg in range(n_seg):
                srows = slice(sg * seg, (sg + 1) * seg)
                si = (r0 + sg * seg) // tl
                s_old = s_ref[0, si, hd]
                o_inter.append(jnp.dot(q_dec[srows, kc], s_old.astype(BF16), preferred_element_type=F32))
                upd = lax.dot_general(k_state[srows, kc], v_h[srows], _TN, preferred_element_type=F32)
                s_ref[0, si, hd] = s_old * _decay_cols(decay[sg * seg:sg * seg + 1, kc]) + upd
            finish_head(rows, hd, o_h + jnp.concatenate(o_inter, axis=0), sz)
    project_out(slice(0, m))


def _gla_layer(x, mods, li, row0, norm_g, gw, final_g, state, prev_out, *, layer, n_layers, tb, tl):
    b, L, _ = x.shape
    n_l = L // tl
    fresh = state is None
    has_prev = prev_out is not None
    final = final_g is not None
    m = tb * tl
    seq = lambda bi, l: (bi, l, 0)
    st_block = (1, tb, GLA_HEADS, GLA_HK, GLA_HV)
    st_map = lambda bi, l: (layer, bi, 0, 0, 0)
    in_specs = [
        pl.BlockSpec((tb, tl, D_MODEL), seq),
        *_mod_specs(tb, li, row0),
        _const_spec((1, D_MODEL), lead=(li,)),
        _const_spec((D_MODEL, GLA_QKVZ), lead=(layer,)),
        _const_spec((D_MODEL, GLA_GL_PAD), lead=(layer,)),
        _const_spec((GLA_GL_PAD, GLA_DK), lead=(layer,)),
        _const_spec((1, GLA_DK), lead=(layer,)),
        _const_spec((1, GLA_DV), lead=(layer,)),
        _const_spec((GLA_DV, D_MODEL), lead=(layer,)),
    ]
    args = [x, mods, mods, mods, norm_g, gw["w_in"], gw["w_gl"], gw["w_up"], gw["b_gate"], gw["norm_g"],
            gw["w_out"]]
    if final:
        in_specs.append(_const_spec((1, D_MODEL)))
        args.append(final_g)
    if not fresh:
        in_specs.append(pl.BlockSpec(st_block, st_map))
        args.append(state)
    aliases = {}
    if has_prev:
        in_specs.append(pl.BlockSpec(memory_space=pl.ANY))
        aliases = {len(args): 1}
        args.append(prev_out)
    kern = functools.partial(_gla_kernel, tb=tb, tl=tl, n_l=n_l, row0=row0, fresh=fresh, has_prev=has_prev,
                             final=final)
    return pl.pallas_call(
        kern,
        grid=(b // tb, n_l),
        in_specs=in_specs,
        out_specs=[pl.BlockSpec((tb, tl, D_MODEL), seq), pl.BlockSpec(st_block, st_map)],
        out_shape=[
            jax.ShapeDtypeStruct(x.shape, F32),
            jax.ShapeDtypeStruct((n_layers, b, GLA_HEADS, GLA_HK, GLA_HV), F32),
        ],
        scratch_shapes=[pltpu.VMEM((m, GLA_DV), BF16)],
        input_output_aliases=aliases,
        compiler_params=pltpu.CompilerParams(
            dimension_semantics=("arbitrary", "arbitrary"), vmem_limit_bytes=VMEM_LIMIT_BYTES),
        name=f"gla_layer{layer}_{'fresh' if fresh else 'cont'}",
    )(*args)


def _run_trunk(x, mods, row0, pool_state, gla_state, pos0, w, *, tb_pool, tl_pool, tb_gla, tl_gla):
    n_pool, n_gla = (DEPTH + 1) // 2, DEPTH // 2
    new_pool = new_gla = None
    for li in range(DEPTH):
        j = li // 2
        if li % 2 == 0:
            x, new_pool = _pool_layer(
                x, mods, li, row0, w["norm_g"], w["pool_w_in"], w["pool_w_grp"],
                w["pool_scale"], w["pool_w_out"], pool_state, new_pool,
                layer=j, n_layers=n_pool, tb=tb_pool, tl=tl_pool, pos0=pos0)
        else:
            x, new_gla = _gla_layer(
                x, mods, li, row0, w["norm_g"], w["gla"], w["final_g"] if li == DEPTH - 1 else None,
                gla_state, new_gla, layer=j, n_layers=n_gla, tb=tb_gla, tl=tl_gla)
    return x, new_pool, new_gla


def kernel(x_prompt, x_sample, c_prompt, c_sample, state_pool, state_gla, ada_w, ada_b, norm_g, pool_w_in,
           pool_w_grp, pool_scale, pool_w_out, gla_w_in, gla_w_gate_up, gla_b_gate, gla_norm_g, gla_w_out,
           final_g):
    n_sample = x_sample.shape[0]
    n_gla = gla_w_in.shape[0]
    c_all = jnp.concatenate([c_sample, c_prompt], axis=0)
    mods = _ada_mods(c_all, ada_w, ada_b).reshape(DEPTH * 3, c_all.shape[0], D_MODEL)

    assert gla_w_in.shape[-1] == GLA_QKVZ + GLA_RANK
    w_gl_pad = jnp.concatenate(
        [gla_w_in[:, :, GLA_QKVZ:], jnp.zeros((n_gla, D_MODEL, GLA_GL_PAD - GLA_RANK), gla_w_in.dtype)], axis=-1)
    w_up_pad = jnp.concatenate(
        [gla_w_gate_up, jnp.zeros((n_gla, GLA_GL_PAD - GLA_RANK, GLA_DK), gla_w_gate_up.dtype)], axis=1)
    w = {
        "norm_g": norm_g.reshape(DEPTH, 1, D_MODEL),
        "pool_w_in": pool_w_in.astype(BF16),
        "pool_w_grp": pool_w_grp.astype(BF16),
        "pool_scale": pool_scale.reshape(-1, 1, D_INNER),
        "pool_w_out": pool_w_out.astype(BF16),
        "gla": {
            "w_in": gla_w_in.astype(BF16),
            "w_gl": w_gl_pad.astype(BF16),
            "w_up": w_up_pad.astype(BF16),
            "b_gate": gla_b_gate.reshape(-1, 1, GLA_DK),
            "norm_g": gla_norm_g.reshape(-1, 1, GLA_DV),
            "w_out": gla_w_out.astype(BF16),
        },
        "final_g": final_g.reshape(1, D_MODEL),
    }
    y_p, pool_p, gla_p = _run_trunk(x_prompt, mods, n_sample, None, None, 0, w,
                                    tb_pool=1, tl_pool=512, tb_gla=1, tl_gla=512)
    y_s, pool_s, gla_s = _run_trunk(x_sample, mods, 0, state_pool.transpose(0, 2, 1, 3), state_gla, PAST_LEN, w,
                                    tb_pool=32, tl_pool=x_sample.shape[1], tb_gla=8, tl_gla=x_sample.shape[1])
    return (y_p, y_s, pool_p, gla_p, pool_s.transpose(0, 2, 1, 3), gla_s)
```

```python
import functools

import jax
import jax.numpy as jnp
from jax import lax
from jax.experimental import pallas as pl
from jax.experimental.pallas import tpu as pltpu

D_MODEL = 1024
DEPTH = 4
PAST_LEN = 16384
D_INNER = 2048
POOL_WINDOWS = (2, 4, 8, 16)
POOL_GROUP = D_INNER // len(POOL_WINDOWS)
POOL_BUF = max(POOL_WINDOWS) - 1
POOL_HALO = POOL_BUF + 1
GLA_HEADS = 4
GLA_DK = 512
GLA_DV = D_INNER
GLA_HK = GLA_DK // GLA_HEADS
GLA_HV = GLA_DV // GLA_HEADS
GLA_RANK = 16
GLA_GATE_NORM = 16.0
GLA_CHUNK = 64
GLA_GL_PAD = 128
GLA_QKVZ = 2 * GLA_DK + 2 * GLA_DV
MOD_ROWS = 8
EPS = 1e-6
F32 = jnp.float32
BF16 = jnp.bfloat16

VMEM_LIMIT_BYTES = 62 * 1024 * 1024

_NT = (((1,), (1,)), ((), ()))
_TN = (((0,), (0,)), ((), ()))


def _silu(v):
    return v / (1.0 + jnp.exp(-v))


def _mod_norm(x, g, shift, scale):
    ms = jnp.mean(x * x, axis=-1, keepdims=True)
    y = x * lax.rsqrt(ms + EPS) * g
    return y * (1.0 + scale) + shift


def _const_spec(shape, lead=(), tail=None):
    idx = tuple(lead) + tuple(tail if tail is not None else (0,) * len(shape))
    return pl.BlockSpec((None,) * len(lead) + tuple(shape), lambda *_: idx, pipeline_mode=pl.Buffered(1))


def _mod_specs(tb, li, row0):
    rows = max(tb, MOD_ROWS)
    assert row0 % rows == 0 and rows % tb == 0
    return [pl.BlockSpec((None, rows, D_MODEL), lambda bi, *_, j=j: (li * 3 + j, (row0 + bi * tb) // rows, 0))
            for j in range(3)]


def _load_mods(refs, tb, row0):
    if tb >= MOD_ROWS:
        return [ref[...][:, None, :] for ref in refs]
    assert tb == 1
    r = (row0 + pl.program_id(0)) % MOD_ROWS
    return [ref[pl.ds(r, 1), :] for ref in refs]


def _ada_kernel(c_ref, w_ref, b_ref, o_ref):
    cs = _silu(c_ref[...]).astype(BF16)
    for j in range(3):
        cols = slice(j * D_MODEL, (j + 1) * D_MODEL)
        o_ref[0, j] = jnp.dot(cs, w_ref[0, :, cols].astype(BF16), preferred_element_type=F32) + b_ref[0, :, cols]


def _ada_mods(c_all, ada_w, ada_b):
    nb = c_all.shape[0]
    return pl.pallas_call(
        _ada_kernel,
        grid=(DEPTH,),
        in_specs=[
            pl.BlockSpec((nb, D_MODEL), lambda li: (0, 0)),
            pl.BlockSpec((1, D_MODEL, 3 * D_MODEL), lambda li: (li, 0, 0)),
            pl.BlockSpec((1, 1, 3 * D_MODEL), lambda li: (li, 0, 0)),
        ],
        out_specs=pl.BlockSpec((1, 3, nb, D_MODEL), lambda li: (li, 0, 0, 0)),
        out_shape=jax.ShapeDtypeStruct((DEPTH, 3, nb, D_MODEL), F32),
        compiler_params=pltpu.CompilerParams(vmem_limit_bytes=VMEM_LIMIT_BYTES),
        name="ada_mods",
    )(c_all, ada_w, ada_b.reshape(DEPTH, 1, 3 * D_MODEL))


def _pool_mix(ext_ref, uz, wgrp_ref, pscale_ref, gated_ref, *, halo, time_axis, rows, pos):
    m = uz.shape[0]

    def window(j, c0):
        idx = [slice(None)] * 3
        idx[time_axis] = slice(halo - j, halo - j + rows)
        idx[2] = slice(c0, c0 + POOL_GROUP)
        return ext_ref[tuple(idx)]

    diffs = []
    for gi, w in enumerate(POOL_WINDOWS):
        c0 = gi * POOL_GROUP
        u_g = window(0, c0)
        s = u_g
        for j in range(1, w):
            s = s + window(j, c0)
        mean = s * (1.0 / w) if pos is None else s / jnp.minimum(pos + 1, w).astype(F32)
        diffs.append((mean - u_g).reshape(m, POOL_GROUP).astype(BF16))
    mixed = [jnp.dot(diffs[gi], wgrp_ref[gi], preferred_element_type=F32) for gi in range(len(POOL_WINDOWS))]
    for gi in range(len(POOL_WINDOWS)):
        c0 = gi * POOL_GROUP
        z_g = uz[:, D_INNER + c0:D_INNER + c0 + POOL_GROUP]
        gated_ref[:, c0:c0 + POOL_GROUP] = (
            mixed[gi] * pscale_ref[:, c0:c0 + POOL_GROUP] * _silu(z_g)).astype(BF16)


def _pool_kernel(*refs, tl, n_l, row0, has_prev):
    it = iter(refs)
    x_ref = next(it)
    mod_refs = [next(it) for _ in range(3)]
    g_ref, win_ref, wgrp_ref, pscale_ref, wout_ref = (next(it) for _ in range(5))
    if has_prev:
        next(it)
    y_ref, nbuf_ref, ext_ref, gated_ref = (next(it) for _ in range(4))

    l = pl.program_id(1)

    @pl.when(l == 0)
    def _():
        ext_ref[:, 0:POOL_HALO, :] = jnp.zeros((1, POOL_HALO, D_INNER), F32)

    shift, scale, gate = _load_mods(mod_refs, 1, row0)
    x = x_ref[...]
    h = _mod_norm(x, g_ref[...], shift, scale).reshape(tl, D_MODEL).astype(BF16)
    uz = jnp.dot(h, win_ref[...], preferred_element_type=F32)
    ext_ref[:, POOL_HALO:POOL_HALO + tl, :] = uz[:, :D_INNER].reshape(1, tl, D_INNER)
    pos = l * tl + lax.broadcasted_iota(jnp.int32, (1, tl, POOL_GROUP), 1)
    _pool_mix(ext_ref, uz, wgrp_ref, pscale_ref, gated_ref, halo=POOL_HALO, time_axis=1, rows=tl, pos=pos)
    y = jnp.dot(gated_ref[...], wout_ref[...], preferred_element_type=F32)
    y_ref[...] = x + gate * y.reshape(1, tl, D_MODEL)

    @pl.when(l == n_l - 1)
    def _():
        nbuf_ref[0] = ext_ref[:, tl + 1:tl + POOL_HALO, :]

    if n_l > 1:
        @pl.when(l < n_l - 1)
        def _():
            ext_ref[:, 0:POOL_HALO, :] = ext_ref[:, tl:tl + POOL_HALO, :]


def _pool_cont_kernel(*refs, tb, tl, pos0, has_prev):
    it = iter(refs)
    x_ref = next(it)
    mod_refs = [next(it) for _ in range(3)]
    g_ref, win_ref, wgrp_ref, pscale_ref, wout_ref, buf_ref = (next(it) for _ in range(6))
    if has_prev:
        next(it)
    y_ref, nbuf_ref, ext_ref, gated_ref = (next(it) for _ in range(4))
    m = tb * tl

    shift, scale, gate = _load_mods(mod_refs, tb, 0)
    x = x_ref[...]
    h = jnp.transpose(_mod_norm(x, g_ref[...], shift, scale), (1, 0, 2))
    uz = jnp.dot(h.reshape(m, D_MODEL).astype(BF16), win_ref[...], preferred_element_type=F32)
    ext_ref[0:POOL_BUF] = buf_ref[...]
    ext_ref[POOL_BUF:POOL_BUF + tl] = uz[:, :D_INNER].reshape(tl, tb, D_INNER)
    pos = None
    if pos0 < POOL_BUF:
        pos = pos0 + lax.broadcasted_iota(jnp.int32, (tl, tb, POOL_GROUP), 0)
    _pool_mix(ext_ref, uz, wgrp_ref, pscale_ref, gated_ref, halo=POOL_BUF, time_axis=0, rows=tl, pos=pos)
    y = jnp.dot(gated_ref[...], wout_ref[...], preferred_element_type=F32)
    y_ref[...] = x + gate * jnp.transpose(y.reshape(tl, tb, D_MODEL), (1, 0, 2))
    nbuf_ref[...] = ext_ref[tl:tl + POOL_BUF]


def _pool_layer(x, mods, li, row0, norm_g, w_in, w_grp, pscale, w_out, state, prev_out, *, layer, n_layers,
                tb, tl, pos0):
    b, L, _ = x.shape
    fresh = state is None
    has_prev = prev_out is not None
    in_specs = [
        None,
        *_mod_specs(tb, li, row0),
        _const_spec((1, D_MODEL), lead=(li,)),
        _const_spec((D_MODEL, 2 * D_INNER), lead=(layer,)),
        _const_spec((len(POOL_WINDOWS), POOL_GROUP, POOL_GROUP), lead=(layer,)),
        _const_spec((1, D_INNER), lead=(layer,)),
        _const_spec((D_INNER, D_MODEL), lead=(layer,)),
    ]
    args = [x, mods, mods, mods, norm_g, w_in, w_grp, pscale, w_out]
    if fresh:
        assert tb == 1 and pos0 == 0
        n_l = L // tl
        grid = (b, n_l)
        x_spec = pl.BlockSpec((1, tl, D_MODEL), lambda bi, l: (bi, l, 0))
        buf_shape = (n_layers, b, POOL_BUF, D_INNER)
        buf_spec = pl.BlockSpec((1, 1, POOL_BUF, D_INNER), lambda bi, l: (layer, bi, 0, 0))
        ext_shape = (1, POOL_HALO + tl, D_INNER)
        kern = functools.partial(_pool_kernel, tl=tl, n_l=n_l, row0=row0, has_prev=has_prev)
        semantics = ("arbitrary", "arbitrary")
    else:
        assert tl == L and row0 == 0
        grid = (b // tb,)
        x_spec = pl.BlockSpec((tb, tl, D_MODEL), lambda bi: (bi, 0, 0))
        buf_shape = (n_layers, POOL_BUF, b, D_INNER)
        buf_spec = pl.BlockSpec((None, POOL_BUF, tb, D_INNER), lambda bi: (layer, 0, bi, 0))
        ext_shape = (POOL_BUF + tl, tb, D_INNER)
        in_specs.append(buf_spec)
        args.append(state)
        kern = functools.partial(_pool_cont_kernel, tb=tb, tl=tl, pos0=pos0, has_prev=has_prev)
        semantics = ("arbitrary",)
    in_specs[0] = x_spec
    aliases = {}
    if has_prev:
        in_specs.append(pl.BlockSpec(memory_space=pl.ANY))
        aliases = {len(args): 1}
        args.append(prev_out)
    return pl.pallas_call(
        kern,
        grid=grid,
        in_specs=in_specs,
        out_specs=[x_spec, buf_spec],
        out_shape=[jax.ShapeDtypeStruct(x.shape, F32), jax.ShapeDtypeStruct(buf_shape, F32)],
        scratch_shapes=[pltpu.VMEM(ext_shape, F32), pltpu.VMEM((tb * tl, D_INNER), BF16)],
        input_output_aliases=aliases,
        compiler_params=pltpu.CompilerParams(dimension_semantics=semantics, vmem_limit_bytes=VMEM_LIMIT_BYTES),
        name=f"pool_layer{layer}_{'fresh' if fresh else 'cont'}",
    )(*args)


def _gla_kernel(*refs, tl, n_l, tb_s, tl_s, row0_p, has_prev, final):
    it = iter(refs)
    xp_ref, xs_ref = next(it), next(it)
    modp_refs = [next(it) for _ in range(3)]
    mods_refs = [next(it) for _ in range(3)]
    g_ref, win_ref, wgl_ref, wup_ref, bg_ref, ng_ref, wout_ref = (next(it) for _ in range(7))
    fg_ref = next(it) if final else None
    s0_ref = next(it)
    if has_prev:
        next(it), next(it)
    yp_ref, ys_ref, sp_ref, ss_ref, gated_ref = (next(it) for _ in range(5))

    l = pl.program_id(1)
    step = pl.program_id(0) * n_l + l
    m_p, m_s = tl, tb_s * tl_s

    @pl.when(l == 0)
    def _():
        sp_ref[...] = jnp.zeros(sp_ref.shape, F32)

    shift_p, scale_p, gate_p = _load_mods(modp_refs, 1, row0_p)
    first = (step * tb_s) % MOD_ROWS
    shift_s, scale_s, gate_s = [
        sum(jnp.where(first == r0, ref[r0:r0 + tb_s, :], 0.0) for r0 in range(0, MOD_ROWS, tb_s))[:, None, :]
        for ref in mods_refs]
    xp, xs = xp_ref[...], xs_ref[...]
    h = jnp.concatenate([
        _mod_norm(xp, g_ref[...], shift_p, scale_p).reshape(m_p, D_MODEL),
        _mod_norm(xs, g_ref[...], shift_s, scale_s).reshape(m_s, D_MODEL)], axis=0).astype(BF16)
    qk = jnp.dot(h, win_ref[:, :2 * GLA_DK], preferred_element_type=F32)
    gl = jnp.dot(h, wgl_ref[...], preferred_element_type=F32).astype(BF16)
    gate_pre = jnp.dot(gl, wup_ref[...], preferred_element_type=F32) + bg_ref[...]
    lg = (jnp.minimum(gate_pre, 0.0) - jnp.log(1.0 + jnp.exp(-jnp.abs(gate_pre)))) * (1.0 / GLA_GATE_NORM)

    def project_v():
        return jnp.dot(h, win_ref[:, 2 * GLA_DK:2 * GLA_DK + GLA_DV], preferred_element_type=F32).astype(BF16)

    def project_sz():
        return _silu(jnp.dot(h, win_ref[:, 2 * GLA_DK + GLA_DV:], preferred_element_type=F32)).astype(BF16)

    def finish_head(rows, hd, o_h, sz):
        o_h = o_h * lax.rsqrt(jnp.mean(o_h * o_h, axis=-1, keepdims=True) + EPS)
        hc = slice(hd * GLA_HV, (hd + 1) * GLA_HV)
        gated_ref[rows, hc] = o_h.astype(BF16) * ng_ref[:, hc].astype(BF16) * sz[rows, hc]

    v_all, sz = _gla_long_chunks(qk, lg, project_v, project_sz, sp_ref, finish_head, m=m_p, tl=tl)
    _gla_short_chunks(qk, lg, v_all, sz, s0_ref, ss_ref, finish_head, row0=m_p, m=m_s, tl=tl_s)

    y = jnp.dot(gated_ref[...], wout_ref[...], preferred_element_type=F32)
    out_p = xp + gate_p * y[:m_p].reshape(1, tl, D_MODEL)
    out_s = xs + gate_s * y[m_p:].reshape(tb_s, tl_s, D_MODEL)
    if final:
        out_p = out_p * lax.rsqrt(jnp.mean(out_p * out_p, axis=-1, keepdims=True) + EPS) * fg_ref[...]
        out_s = out_s * lax.rsqrt(jnp.mean(out_s * out_s, axis=-1, keepdims=True) + EPS) * fg_ref[...]
    yp_ref[...] = out_p
    ys_ref[...] = out_s


def _prefix_sums(pat, lg_rows):
    hi = lg_rows.astype(BF16)
    lo = (lg_rows - hi.astype(F32)).astype(BF16)
    return jnp.dot(jnp.concatenate([pat, pat], axis=1), jnp.concatenate([hi, lo], axis=0),
                   preferred_element_type=F32)


def _decay_cols(decay_row):
    d_col = jnp.transpose(jnp.broadcast_to(decay_row, (GLA_HK, GLA_HK)))
    return jnp.concatenate([d_col] * (GLA_HV // GLA_HK), axis=1)


def _gla_long_chunks(qk, lg, project_v, project_sz, s_ref, finish_head, *, m, tl):
    hb = GLA_CHUNK
    cr = 2 * hb
    ri = lax.broadcasted_iota(jnp.int32, (cr, cr), 0)
    ci = lax.broadcasted_iota(jnp.int32, (cr, cr), 1)
    causal = ci <= ri
    tri = jnp.where(causal, 1.0, 0.0).astype(BF16)
    zeros = jnp.zeros((hb, GLA_HK), BF16)
    heads = range(GLA_HEADS)
    kcs = [slice(hd * GLA_HK, (hd + 1) * GLA_HK) for hd in heads]

    def prepare(c):
        rows = slice(c * cr, (c + 1) * cr)
        bcum = _prefix_sums(tri, lg[rows])
        b_a, b_b = bcum[:hb], bcum[hb:]
        ref_a, bnd = bcum[hb // 2 - 1:hb // 2], bcum[hb - 1:hb]
        ref_b, tot = bcum[hb + hb // 2 - 1:hb + hb // 2], bcum[cr - 1:cr]
        q = qk[rows, 0:GLA_DK] * (GLA_HK ** -0.5)
        k = qk[rows, GLA_DK:2 * GLA_DK]
        q_a, q_b, k_a, k_b = q[:hb], q[hb:], k[:hb], k[hb:]
        qi_a = (q_a * jnp.exp(b_a - ref_a)).astype(BF16)
        ki_a = (k_a * jnp.exp(ref_a - b_a)).astype(BF16)
        qi_b = (q_b * jnp.exp(b_b - ref_b)).astype(BF16)
        ki_b = (k_b * jnp.exp(ref_b - b_b)).astype(BF16)
        qx_b = (q_b * jnp.exp(b_b - bnd)).astype(BF16)
        kx_a = (k_a * jnp.exp(bnd - b_a)).astype(BF16)
        q_dec = (q * jnp.exp(bcum)).astype(BF16)
        k_state = (k * jnp.exp(tot - bcum)).astype(BF16)
        decay = jnp.exp(tot)
        return qi_a, ki_a, qi_b, ki_b, qx_b, kx_a, q_dec, k_state, decay

    sz = project_sz()
    prepared = [prepare(c) for c in range(m // cr)]
    v_all = project_v()
    for c, (qi_a, ki_a, qi_b, ki_b, qx_b, kx_a, q_dec, k_state, decay) in enumerate(prepared):
        rows = slice(c * cr, (c + 1) * cr)
        si = (c * cr) // tl
        v = [v_all[rows, hd * GLA_HV:(hd + 1) * GLA_HV] for hd in heads]
        s_old = [s_ref[0, si, hd] for hd in heads]
        att_a = [lax.dot_general(qi_a[:, kc], jnp.concatenate([ki_a[:, kc], zeros], axis=0), _NT,
                                 preferred_element_type=F32) for kc in kcs]
        att_b = [lax.dot_general(
            jnp.concatenate([qx_b[:, kc], qi_b[:, kc]], axis=1),
            jnp.concatenate([jnp.concatenate([kx_a[:, kc], zeros], axis=1),
                             jnp.concatenate([zeros, ki_b[:, kc]], axis=1)], axis=0),
            _NT, preferred_element_type=F32) for kc in kcs]
        upd = [lax.dot_general(k_state[:, kc], v[hd], _TN, preferred_element_type=F32)
               for hd, kc in enumerate(kcs)]
        att = [jnp.where(causal, jnp.concatenate([att_a[hd], att_b[hd]], axis=0), 0.0).astype(BF16)
               for hd in heads]
        o = [jnp.dot(jnp.concatenate([att[hd], q_dec[:, kc]], axis=1),
                     jnp.concatenate([v[hd], s_old[hd].astype(BF16)], axis=0), preferred_element_type=F32)
             for hd, kc in enumerate(kcs)]
        for hd, kc in enumerate(kcs):
            s_ref[0, si, hd] = s_old[hd] * _decay_cols(decay[:, kc]) + upd[hd]
        for hd in heads:
            finish_head(rows, hd, o[hd], sz)
    return v_all, sz


def _gla_short_chunks(qk, lg, v_all, sz, s_in_ref, s_out_ref, finish_head, *, row0, m, tl):
    seg = tl
    chunk = min(GLA_CHUNK, m)
    n_seg = chunk // seg
    ri = lax.broadcasted_iota(jnp.int32, (chunk, chunk), 0)
    ci = lax.broadcasted_iota(jnp.int32, (chunk, chunk), 1)
    sh = seg.bit_length() - 1
    same = (ri >> sh) == (ci >> sh)
    causal = same & (ci <= ri)
    mid = same & ((ci & (seg - 1)) <= (seg // 2 - 1))
    pat = jnp.concatenate([jnp.where(mk, 1.0, 0.0).astype(BF16) for mk in (causal, mid, same)], axis=0)

    for c in range(m // chunk):
        r0 = c * chunk
        rows = slice(row0 + r0, row0 + r0 + chunk)
        sums = _prefix_sums(pat, lg[rows])
        bcum, bref, btot = sums[:chunk], sums[chunk:2 * chunk], sums[2 * chunk:]
        q = qk[rows, 0:GLA_DK] * (GLA_HK ** -0.5)
        k = qk[rows, GLA_DK:2 * GLA_DK]
        q_intra = (q * jnp.exp(bcum - bref)).astype(BF16)
        k_intra = (k * jnp.exp(bref - bcum)).astype(BF16)
        q_dec = (q * jnp.exp(bcum)).astype(BF16)
        k_state = (k * jnp.exp(btot - bcum)).astype(BF16)
        decay = jnp.exp(btot)
        for hd in range(GLA_HEADS):
            kc = slice(hd * GLA_HK, (hd + 1) * GLA_HK)
            v_h = v_all[rows, hd * GLA_HV:(hd + 1) * GLA_HV]
            att = lax.dot_general(q_intra[:, kc], k_intra[:, kc], _NT, preferred_element_type=F32)
            att = jnp.where(causal, att, 0.0).astype(BF16)
            o_h = jnp.dot(att, v_h, preferred_element_type=F32)
            o_inter = []
            for sg in range(n_seg):
                srows = slice(sg * seg, (sg + 1) * seg)
                si = (r0 + sg * seg) // tl
                s_old = s_in_ref[0, si, hd]
                o_inter.append(jnp.dot(q_dec[srows, kc], s_old.astype(BF16), preferred_element_type=F32))
                upd = lax.dot_general(k_state[srows, kc], v_h[srows], _TN, preferred_element_type=F32)
                s_out_ref[0, si, hd] = s_old * _decay_cols(decay[sg * seg:sg * seg + 1, kc]) + upd
            finish_head(rows, hd, o_h + jnp.concatenate(o_inter, axis=0), sz)


def _gla_layer(x_p, x_s, mods, li, row0_p, norm_g, gw, final_g, state_s, prev_p, prev_s, *, layer, n_layers,
               tl, tb_s):
    bp, L, _ = x_p.shape
    bs, tl_s, _ = x_s.shape
    n_l = L // tl
    assert bp * n_l * tb_s == bs and MOD_ROWS % tb_s == 0
    has_prev = prev_p is not None
    final = final_g is not None
    m = tl + tb_s * tl_s
    step = lambda bi, l: bi * n_l + l
    tile_p = lambda bi, l: (bi, l, 0)
    tile_s = lambda bi, l: (step(bi, l), 0, 0)
    st_p = pl.BlockSpec((1, 1, GLA_HEADS, GLA_HK, GLA_HV), lambda bi, l: (layer, bi, 0, 0, 0))
    st_s = pl.BlockSpec((1, tb_s, GLA_HEADS, GLA_HK, GLA_HV), lambda bi, l: (layer, step(bi, l), 0, 0, 0))
    in_specs = [
        pl.BlockSpec((1, tl, D_MODEL), tile_p),
        pl.BlockSpec((tb_s, tl_s, D_MODEL), tile_s),
        *_mod_specs(1, li, row0_p),
        *[pl.BlockSpec((None, MOD_ROWS, D_MODEL),
                       lambda bi, l, j=j: (li * 3 + j, (step(bi, l) * tb_s) // MOD_ROWS, 0)) for j in range(3)],
        _const_spec((1, D_MODEL), lead=(li,)),
        _const_spec((D_MODEL, GLA_QKVZ), lead=(layer,)),
        _const_spec((D_MODEL, GLA_GL_PAD), lead=(layer,)),
        _const_spec((GLA_GL_PAD, GLA_DK), lead=(layer,)),
        _const_spec((1, GLA_DK), lead=(layer,)),
        _const_spec((1, GLA_DV), lead=(layer,)),
        _const_spec((GLA_DV, D_MODEL), lead=(layer,)),
    ]
    args = [x_p, x_s, mods, mods, mods, mods, mods, mods, norm_g, gw["w_in"], gw["w_gl"], gw["w_up"],
            gw["b_gate"], gw["norm_g"], gw["w_out"]]
    if final:
        in_specs.append(_const_spec((1, D_MODEL)))
        args.append(final_g)
    in_specs.append(st_s)
    args.append(state_s)
    aliases = {}
    if has_prev:
        in_specs += [pl.BlockSpec(memory_space=pl.ANY)] * 2
        aliases = {len(args): 2, len(args) + 1: 3}
        args += [prev_p, prev_s]
    kern = functools.partial(_gla_kernel, tl=tl, n_l=n_l, tb_s=tb_s, tl_s=tl_s, row0_p=row0_p, has_prev=has_prev,
                             final=final)
    return pl.pallas_call(
        kern,
        grid=(bp, n_l),
        in_specs=in_specs,
        out_specs=[pl.BlockSpec((1, tl, D_MODEL), tile_p), pl.BlockSpec((tb_s, tl_s, D_MODEL), tile_s), st_p, st_s],
        out_shape=[
            jax.ShapeDtypeStruct(x_p.shape, F32),
            jax.ShapeDtypeStruct(x_s.shape, F32),
            jax.ShapeDtypeStruct((n_layers, bp, GLA_HEADS, GLA_HK, GLA_HV), F32),
            jax.ShapeDtypeStruct((n_layers, bs, GLA_HEADS, GLA_HK, GLA_HV), F32),
        ],
        scratch_shapes=[pltpu.VMEM((m, GLA_DV), BF16)],
        input_output_aliases=aliases,
        compiler_params=pltpu.CompilerParams(
            dimension_semantics=("arbitrary", "arbitrary"), vmem_limit_bytes=VMEM_LIMIT_BYTES),
        name=f"gla_layer{layer}",
    )(*args)


def _run_trunk(x_p, x_s, mods, row0_p, pool_state_s, gla_state_s, pos0_s, w, *, tl_p, tb_pool_s, tb_gla_s):
    n_pool, n_gla = (DEPTH + 1) // 2, DEPTH // 2
    pool_p = pool_s = gla_p = gla_s = None
    for li in range(DEPTH):
        j = li // 2
        if li % 2 == 0:
            pool_w = (w["norm_g"], w["pool_w_in"], w["pool_w_grp"], w["pool_scale"], w["pool_w_out"])
            x_p, pool_p = _pool_layer(x_p, mods, li, row0_p, *pool_w, None, pool_p,
                                      layer=j, n_layers=n_pool, tb=1, tl=tl_p, pos0=0)
            x_s, pool_s = _pool_layer(x_s, mods, li, 0, *pool_w, pool_state_s, pool_s,
                                      layer=j, n_layers=n_pool, tb=tb_pool_s, tl=x_s.shape[1], pos0=pos0_s)
        else:
            x_p, x_s, gla_p, gla_s = _gla_layer(
                x_p, x_s, mods, li, row0_p, w["norm_g"], w["gla"], w["final_g"] if li == DEPTH - 1 else None,
                gla_state_s, gla_p, gla_s, layer=j, n_layers=n_gla, tl=tl_p, tb_s=tb_gla_s)
    return x_p, x_s, pool_p, gla_p, pool_s, gla_s


def kernel(x_prompt, x_sample, c_prompt, c_sample, state_pool, state_gla, ada_w, ada_b, norm_g, pool_w_in,
           pool_w_grp, pool_scale, pool_w_out, gla_w_in, gla_w_gate_up, gla_b_gate, gla_norm_g, gla_w_out,
           final_g):
    n_sample = x_sample.shape[0]
    n_gla = gla_w_in.shape[0]
    c_all = jnp.concatenate([c_sample, c_prompt], axis=0)
    mods = _ada_mods(c_all, ada_w, ada_b).reshape(DEPTH * 3, c_all.shape[0], D_MODEL)

    assert gla_w_in.shape[-1] == GLA_QKVZ + GLA_RANK
    w_gl_pad = jnp.concatenate(
        [gla_w_in[:, :, GLA_QKVZ:], jnp.zeros((n_gla, D_MODEL, GLA_GL_PAD - GLA_RANK), gla_w_in.dtype)], axis=-1)
    w_up_pad = jnp.concatenate(
        [gla_w_gate_up, jnp.zeros((n_gla, GLA_GL_PAD - GLA_RANK, GLA_DK), gla_w_gate_up.dtype)], axis=1)
    w = {
        "norm_g": norm_g.reshape(DEPTH, 1, D_MODEL),
        "pool_w_in": pool_w_in.astype(BF16),
        "pool_w_grp": pool_w_grp.astype(BF16),
        "pool_scale": pool_scale.reshape(-1, 1, D_INNER),
        "pool_w_out": pool_w_out.astype(BF16),
        "gla": {
            "w_in": gla_w_in.astype(BF16),
            "w_gl": w_gl_pad.astype(BF16),
            "w_up": w_up_pad.astype(BF16),
            "b_gate": gla_b_gate.reshape(-1, 1, GLA_DK),
            "norm_g": gla_norm_g.reshape(-1, 1, GLA_DV),
            "w_out": gla_w_out.astype(BF16),
        },
        "final_g": final_g.reshape(1, D_MODEL),
    }
    y_p, y_s, pool_p, gla_p, pool_s, gla_s = _run_trunk(
        x_prompt, x_sample, mods, n_sample, state_pool.transpose(0, 2, 1, 3), state_gla, PAST_LEN, w,
        tl_p=512, tb_pool_s=32, tb_gla_s=4)
    return (y_p, y_s, pool_p, gla_p, pool_s.transpose(0, 2, 1, 3), gla_s)
```

```python
import functools

import jax
import jax.numpy as jnp
from jax import lax
from jax.experimental import pallas as pl
from jax.experimental.pallas import tpu as pltpu

D_MODEL = 1024
DEPTH = 4
PAST_LEN = 16384
D_INNER = 2048
POOL_WINDOWS = (2, 4, 8, 16)
POOL_GROUP = D_INNER // len(POOL_WINDOWS)
POOL_BUF = max(POOL_WINDOWS) - 1
POOL_HALO = POOL_BUF + 1
GLA_HEADS = 4
GLA_DK = 512
GLA_DV = D_INNER
GLA_HK = GLA_DK // GLA_HEADS
GLA_HV = GLA_DV // GLA_HEADS
GLA_RANK = 16
GLA_GATE_NORM = 16.0
GLA_CHUNK = 64
GLA_GL_PAD = 128
GLA_QKVZ = 2 * GLA_DK + 2 * GLA_DV
MOD_ROWS = 8
EPS = 1e-6
F32 = jnp.float32
BF16 = jnp.bfloat16

VMEM_LIMIT_BYTES = 62 * 1024 * 1024

_NT = (((1,), (1,)), ((), ()))
_TN = (((0,), (0,)), ((), ()))


def _silu(v):
    return v / (1.0 + jnp.exp(-v))


def _mod_norm(x, g, shift, scale):
    ms = jnp.mean(x * x, axis=-1, keepdims=True)
    y = x * lax.rsqrt(ms + EPS) * g
    return y * (1.0 + scale) + shift


def _const_spec(shape, lead=(), tail=None):
    idx = tuple(lead) + tuple(tail if tail is not None else (0,) * len(shape))
    return pl.BlockSpec((None,) * len(lead) + tuple(shape), lambda *_: idx, pipeline_mode=pl.Buffered(1))


def _mod_specs(tb, li, row0):
    rows = max(tb, MOD_ROWS)
    assert row0 % rows == 0 and rows % tb == 0
    return [pl.BlockSpec((None, rows, D_MODEL), lambda bi, *_, j=j: (li * 3 + j, (row0 + bi * tb) // rows, 0))
            for j in range(3)]


def _load_mods(refs, tb, row0):
    if tb >= MOD_ROWS:
        return [ref[...][:, None, :] for ref in refs]
    assert tb == 1
    r = (row0 + pl.program_id(0)) % MOD_ROWS
    return [ref[pl.ds(r, 1), :] for ref in refs]


ADA_STREAMS = 2


def _ada_kernel(c_ref, *refs):
    w_refs, (b_ref, o_ref) = refs[:ADA_STREAMS], refs[ADA_STREAMS:]
    kb = D_MODEL // ADA_STREAMS
    cs = _silu(c_ref[...]).astype(BF16)
    for j in range(3):
        cols = slice(j * D_MODEL, (j + 1) * D_MODEL)
        acc = b_ref[0, :, cols]
        for s, w_ref in enumerate(w_refs):
            acc = acc + jnp.dot(cs[:, s * kb:(s + 1) * kb], w_ref[0, :, cols].astype(BF16),
                                preferred_element_type=F32)
        o_ref[0, j] = acc


def _ada_mods(c_all, ada_w, ada_b):
    nb = c_all.shape[0]
    kb = D_MODEL // ADA_STREAMS
    return pl.pallas_call(
        _ada_kernel,
        grid=(DEPTH,),
        in_specs=[
            pl.BlockSpec((nb, D_MODEL), lambda li: (0, 0)),
            *[pl.BlockSpec((1, kb, 3 * D_MODEL), lambda li, s=s: (li, s, 0)) for s in range(ADA_STREAMS)],
            pl.BlockSpec((1, 1, 3 * D_MODEL), lambda li: (li, 0, 0)),
        ],
        out_specs=pl.BlockSpec((1, 3, nb, D_MODEL), lambda li: (li, 0, 0, 0)),
        out_shape=jax.ShapeDtypeStruct((DEPTH, 3, nb, D_MODEL), F32),
        compiler_params=pltpu.CompilerParams(vmem_limit_bytes=VMEM_LIMIT_BYTES),
        name="ada_mods",
    )(c_all, *[ada_w] * ADA_STREAMS, ada_b.reshape(DEPTH, 1, 3 * D_MODEL))


def _pool_mix(ext_ref, uz, wgrp_ref, pscale_ref, gated_ref, *, halo, time_axis, rows, pos):
    m = uz.shape[0]

    def window(j, c0):
        idx = [slice(None)] * 3
        idx[time_axis] = slice(halo - j, halo - j + rows)
        idx[2] = slice(c0, c0 + POOL_GROUP)
        return ext_ref[tuple(idx)]

    diffs = []
    for gi, w in enumerate(POOL_WINDOWS):
        c0 = gi * POOL_GROUP
        u_g = window(0, c0)
        s = u_g
        for j in range(1, w):
            s = s + window(j, c0)
        mean = s * (1.0 / w) if pos is None else s / jnp.minimum(pos + 1, w).astype(F32)
        diffs.append((mean - u_g).reshape(m, POOL_GROUP).astype(BF16))
    mixed = [jnp.dot(diffs[gi], wgrp_ref[gi], preferred_element_type=F32) for gi in range(len(POOL_WINDOWS))]
    for gi in range(len(POOL_WINDOWS)):
        c0 = gi * POOL_GROUP
        z_g = uz[:, D_INNER + c0:D_INNER + c0 + POOL_GROUP]
        gated_ref[:, c0:c0 + POOL_GROUP] = (
            mixed[gi] * pscale_ref[:, c0:c0 + POOL_GROUP] * _silu(z_g)).astype(BF16)


def _pool_kernel(*refs, tl, n_l, row0, has_prev):
    it = iter(refs)
    x_ref = next(it)
    mod_refs = [next(it) for _ in range(3)]
    g_ref, win_ref, wgrp_ref, pscale_ref, wout_ref = (next(it) for _ in range(5))
    if has_prev:
        next(it)
    y_ref, nbuf_ref, ext_ref, gated_ref = (next(it) for _ in range(4))

    l = pl.program_id(1)

    @pl.when(l == 0)
    def _():
        ext_ref[:, 0:POOL_HALO, :] = jnp.zeros((1, POOL_HALO, D_INNER), F32)

    shift, scale, gate = _load_mods(mod_refs, 1, row0)
    x = x_ref[...]
    h = _mod_norm(x, g_ref[...], shift, scale).reshape(tl, D_MODEL).astype(BF16)
    uz = jnp.dot(h, win_ref[...], preferred_element_type=F32)
    ext_ref[:, POOL_HALO:POOL_HALO + tl, :] = uz[:, :D_INNER].reshape(1, tl, D_INNER)
    pos = l * tl + lax.broadcasted_iota(jnp.int32, (1, tl, POOL_GROUP), 1)
    _pool_mix(ext_ref, uz, wgrp_ref, pscale_ref, gated_ref, halo=POOL_HALO, time_axis=1, rows=tl, pos=pos)
    y = jnp.dot(gated_ref[...], wout_ref[...], preferred_element_type=F32)
    y_ref[...] = x + gate * y.reshape(1, tl, D_MODEL)

    @pl.when(l == n_l - 1)
    def _():
        nbuf_ref[0] = ext_ref[:, tl + 1:tl + POOL_HALO, :]

    if n_l > 1:
        @pl.when(l < n_l - 1)
        def _():
            ext_ref[:, 0:POOL_HALO, :] = ext_ref[:, tl:tl + POOL_HALO, :]


def _pool_cont_kernel(*refs, tb, tl, pos0, has_prev):
    it = iter(refs)
    x_ref = next(it)
    mod_refs = [next(it) for _ in range(3)]
    g_ref, win_ref, wgrp_ref, pscale_ref, wout_ref, buf_ref = (next(it) for _ in range(6))
    if has_prev:
        next(it)
    y_ref, nbuf_ref, ext_ref, gated_ref = (next(it) for _ in range(4))
    m = tb * tl

    shift, scale, gate = _load_mods(mod_refs, tb, 0)
    x = x_ref[...]
    h = jnp.transpose(_mod_norm(x, g_ref[...], shift, scale), (1, 0, 2))
    uz = jnp.dot(h.reshape(m, D_MODEL).astype(BF16), win_ref[...], preferred_element_type=F32)
    ext_ref[0:POOL_BUF] = buf_ref[...]
    ext_ref[POOL_BUF:POOL_BUF + tl] = uz[:, :D_INNER].reshape(tl, tb, D_INNER)
    pos = None
    if pos0 < POOL_BUF:
        pos = pos0 + lax.broadcasted_iota(jnp.int32, (tl, tb, POOL_GROUP), 0)
    _pool_mix(ext_ref, uz, wgrp_ref, pscale_ref, gated_ref, halo=POOL_BUF, time_axis=0, rows=tl, pos=pos)
    y = jnp.dot(gated_ref[...], wout_ref[...], preferred_element_type=F32)
    y_ref[...] = x + gate * jnp.transpose(y.reshape(tl, tb, D_MODEL), (1, 0, 2))
    nbuf_ref[...] = ext_ref[tl:tl + POOL_BUF]


def _pool_layer(x, mods, li, row0, norm_g, w_in, w_grp, pscale, w_out, state, prev_out, *, layer, n_layers,
                tb, tl, pos0):
    b, L, _ = x.shape
    fresh = state is None
    has_prev = prev_out is not None
    in_specs = [
        None,
        *_mod_specs(tb, li, row0),
        _const_spec((1, D_MODEL), lead=(li,)),
        _const_spec((D_MODEL, 2 * D_INNER), lead=(layer,)),
        _const_spec((len(POOL_WINDOWS), POOL_GROUP, POOL_GROUP), lead=(layer,)),
        _const_spec((1, D_INNER), lead=(layer,)),
        _const_spec((D_INNER, D_MODEL), lead=(layer,)),
    ]
    args = [x, mods, mods, mods, norm_g, w_in, w_grp, pscale, w_out]
    if fresh:
        assert tb == 1 and pos0 == 0
        n_l = L // tl
        grid = (b, n_l)
        x_spec = pl.BlockSpec((1, tl, D_MODEL), lambda bi, l: (bi, l, 0))
        buf_shape = (n_layers, b, POOL_BUF, D_INNER)
        buf_spec = pl.BlockSpec((1, 1, POOL_BUF, D_INNER), lambda bi, l: (layer, bi, 0, 0))
        ext_shape = (1, POOL_HALO + tl, D_INNER)
        kern = functools.partial(_pool_kernel, tl=tl, n_l=n_l, row0=row0, has_prev=has_prev)
        semantics = ("arbitrary", "arbitrary")
    else:
        assert tl == L and row0 == 0
        grid = (b // tb,)
        x_spec = pl.BlockSpec((tb, tl, D_MODEL), lambda bi: (bi, 0, 0))
        buf_shape = (n_layers, POOL_BUF, b, D_INNER)
        buf_spec = pl.BlockSpec((None, POOL_BUF, tb, D_INNER), lambda bi: (layer, 0, bi, 0))
        ext_shape = (POOL_BUF + tl, tb, D_INNER)
        in_specs.append(buf_spec)
        args.append(state)
        kern = functools.partial(_pool_cont_kernel, tb=tb, tl=tl, pos0=pos0, has_prev=has_prev)
        semantics = ("arbitrary",)
    in_specs[0] = x_spec
    aliases = {}
    if has_prev:
        in_specs.append(pl.BlockSpec(memory_space=pl.ANY))
        aliases = {len(args): 1}
        args.append(prev_out)
    return pl.pallas_call(
        kern,
        grid=grid,
        in_specs=in_specs,
        out_specs=[x_spec, buf_spec],
        out_shape=[jax.ShapeDtypeStruct(x.shape, F32), jax.ShapeDtypeStruct(buf_shape, F32)],
        scratch_shapes=[pltpu.VMEM(ext_shape, F32), pltpu.VMEM((tb * tl, D_INNER), BF16)],
        input_output_aliases=aliases,
        compiler_params=pltpu.CompilerParams(dimension_semantics=semantics, vmem_limit_bytes=VMEM_LIMIT_BYTES),
        name=f"pool_layer{layer}_{'fresh' if fresh else 'cont'}",
    )(*args)


def _gla_kernel(*refs, tl, n_l, tb_s, tl_s, row0_p, has_prev, final):
    it = iter(refs)
    xp_ref, xs_ref = next(it), next(it)
    modp_refs = [next(it) for _ in range(3)]
    mods_refs = [next(it) for _ in range(3)]
    g_ref, win_ref, wgl_ref, wup_ref, bg_ref, ng_ref, wout_ref = (next(it) for _ in range(7))
    fg_ref = next(it) if final else None
    s0_ref = next(it)
    if has_prev:
        next(it), next(it)
    yp_ref, ys_ref, sp_ref, ss_ref, gated_ref = (next(it) for _ in range(5))

    l = pl.program_id(1)
    step = pl.program_id(0) * n_l + l
    m_p, m_s = tl, tb_s * tl_s

    @pl.when(l == 0)
    def _():
        sp_ref[...] = jnp.zeros(sp_ref.shape, F32)

    shift_p, scale_p, gate_p = _load_mods(modp_refs, 1, row0_p)
    first = (step * tb_s) % MOD_ROWS
    shift_s, scale_s, gate_s = [
        sum(jnp.where(first == r0, ref[r0:r0 + tb_s, :], 0.0) for r0 in range(0, MOD_ROWS, tb_s))[:, None, :]
        for ref in mods_refs]
    xp, xs = xp_ref[...], xs_ref[...]
    h = jnp.concatenate([
        _mod_norm(xp, g_ref[...], shift_p, scale_p).reshape(m_p, D_MODEL),
        _mod_norm(xs, g_ref[...], shift_s, scale_s).reshape(m_s, D_MODEL)], axis=0).astype(BF16)
    qk = jnp.dot(h, win_ref[:, :2 * GLA_DK], preferred_element_type=F32)
    gl = jnp.dot(h, wgl_ref[...], preferred_element_type=F32).astype(BF16)
    gate_pre = jnp.dot(gl, wup_ref[...], preferred_element_type=F32) + bg_ref[...]
    lg = (jnp.minimum(gate_pre, 0.0) - jnp.log(1.0 + jnp.exp(-jnp.abs(gate_pre)))) * (1.0 / GLA_GATE_NORM)

    def project_v():
        return jnp.dot(h, win_ref[:, 2 * GLA_DK:2 * GLA_DK + GLA_DV], preferred_element_type=F32).astype(BF16)

    def project_sz():
        return _silu(jnp.dot(h, win_ref[:, 2 * GLA_DK + GLA_DV:], preferred_element_type=F32)).astype(BF16)

    def finish_head(rows, hd, o_h, sz):
        o_h = o_h * lax.rsqrt(jnp.mean(o_h * o_h, axis=-1, keepdims=True) + EPS)
        hc = slice(hd * GLA_HV, (hd + 1) * GLA_HV)
        gated_ref[rows, hc] = o_h.astype(BF16) * ng_ref[:, hc].astype(BF16) * sz[rows, hc]

    v_all, sz = _gla_long_chunks(qk, lg, project_v, project_sz, sp_ref, finish_head, m=m_p, tl=tl)
    _gla_short_chunks(qk, lg, v_all, sz, s0_ref, ss_ref, finish_head, row0=m_p, m=m_s, tl=tl_s)

    y = jnp.dot(gated_ref[...], wout_ref[...], preferred_element_type=F32)
    out_p = xp + gate_p * y[:m_p].reshape(1, tl, D_MODEL)
    out_s = xs + gate_s * y[m_p:].reshape(tb_s, tl_s, D_MODEL)
    if final:
        out_p = out_p * lax.rsqrt(jnp.mean(out_p * out_p, axis=-1, keepdims=True) + EPS) * fg_ref[...]
        out_s = out_s * lax.rsqrt(jnp.mean(out_s * out_s, axis=-1, keepdims=True) + EPS) * fg_ref[...]
    yp_ref[...] = out_p
    ys_ref[...] = out_s


def _prefix_sums(pat, lg_rows):
    hi = lg_rows.astype(BF16)
    lo = (lg_rows - hi.astype(F32)).astype(BF16)
    return jnp.dot(jnp.concatenate([pat, pat], axis=1), jnp.concatenate([hi, lo], axis=0),
                   preferred_element_type=F32)


def _decay_cols(decay_row):
    d_col = jnp.transpose(jnp.broadcast_to(decay_row, (GLA_HK, GLA_HK)))
    return jnp.concatenate([d_col] * (GLA_HV // GLA_HK), axis=1)


def _gla_long_chunks(qk, lg, project_v, project_sz, s_ref, finish_head, *, m, tl):
    hb = GLA_CHUNK
    cr = 2 * hb
    ri = lax.broadcasted_iota(jnp.int32, (cr, cr), 0)
    ci = lax.broadcasted_iota(jnp.int32, (cr, cr), 1)
    causal = ci <= ri
    tri = jnp.where(causal, 1.0, 0.0).astype(BF16)
    zeros = jnp.zeros((hb, GLA_HK), BF16)
    heads = range(GLA_HEADS)
    kcs = [slice(hd * GLA_HK, (hd + 1) * GLA_HK) for hd in heads]

    def prepare(c):
        rows = slice(c * cr, (c + 1) * cr)
        bcum = _prefix_sums(tri, lg[rows])
        b_a, b_b = bcum[:hb], bcum[hb:]
        ref_a, bnd = bcum[hb // 2 - 1:hb // 2], bcum[hb - 1:hb]
        ref_b, tot = bcum[hb + hb // 2 - 1:hb + hb // 2], bcum[cr - 1:cr]
        q = qk[rows, 0:GLA_DK] * (GLA_HK ** -0.5)
        k = qk[rows, GLA_DK:2 * GLA_DK]
        q_a, q_b, k_a, k_b = q[:hb], q[hb:], k[:hb], k[hb:]
        qi_a = (q_a * jnp.exp(b_a - ref_a)).astype(BF16)
        ki_a = (k_a * jnp.exp(ref_a - b_a)).astype(BF16)
        qi_b = (q_b * jnp.exp(b_b - ref_b)).astype(BF16)
        ki_b = (k_b * jnp.exp(ref_b - b_b)).astype(BF16)
        qx_b = (q_b * jnp.exp(b_b - bnd)).astype(BF16)
        kx_a = (k_a * jnp.exp(bnd - b_a)).astype(BF16)
        q_dec = (q * jnp.exp(bcum)).astype(BF16)
        k_state = (k * jnp.exp(tot - bcum)).astype(BF16)
        decay = jnp.exp(tot)
        return qi_a, ki_a, qi_b, ki_b, qx_b, kx_a, q_dec, k_state, decay

    sz = project_sz()
    prepared = [prepare(c) for c in range(m // cr)]
    v_all = project_v()
    for c, (qi_a, ki_a, qi_b, ki_b, qx_b, kx_a, q_dec, k_state, decay) in enumerate(prepared):
        rows = slice(c * cr, (c + 1) * cr)
        si = (c * cr) // tl
        v = [v_all[rows, hd * GLA_HV:(hd + 1) * GLA_HV] for hd in heads]
        s_old = [s_ref[0, si, hd] for hd in heads]
        att_a = [lax.dot_general(qi_a[:, kc], jnp.concatenate([ki_a[:, kc], zeros], axis=0), _NT,
                                 preferred_element_type=F32) for kc in kcs]
        att_b = [lax.dot_general(
            jnp.concatenate([qx_b[:, kc], qi_b[:, kc]], axis=1),
            jnp.concatenate([jnp.concatenate([kx_a[:, kc], zeros], axis=1),
                             jnp.concatenate([zeros, ki_b[:, kc]], axis=1)], axis=0),
            _NT, preferred_element_type=F32) for kc in kcs]
        upd = [lax.dot_general(k_state[:, kc], v[hd], _TN, preferred_element_type=F32)
               for hd, kc in enumerate(kcs)]
        att = [jnp.where(causal, jnp.concatenate([att_a[hd], att_b[hd]], axis=0), 0.0).astype(BF16)
               for hd in heads]
        o = [jnp.dot(jnp.concatenate([att[hd], q_dec[:, kc]], axis=1),
                     jnp.concatenate([v[hd], s_old[hd].astype(BF16)], axis=0), preferred_element_type=F32)
             for hd, kc in enumerate(kcs)]
        for hd, kc in enumerate(kcs):
            s_ref[0, si, hd] = s_old[hd] * _decay_cols(decay[:, kc]) + upd[hd]
        for hd in heads:
            finish_head(rows, hd, o[hd], sz)
    return v_all, sz


def _gla_short_chunks(qk, lg, v_all, sz, s_in_ref, s_out_ref, finish_head, *, row0, m, tl):
    seg = tl
    chunk = min(GLA_CHUNK, m)
    n_seg = chunk // seg
    ri = lax.broadcasted_iota(jnp.int32, (chunk, chunk), 0)
    ci = lax.broadcasted_iota(jnp.int32, (chunk, chunk), 1)
    sh = seg.bit_length() - 1
    same = (ri >> sh) == (ci >> sh)
    causal = same & (ci <= ri)
    mid = same & ((ci & (seg - 1)) <= (seg // 2 - 1))
    pat = jnp.concatenate([jnp.where(mk, 1.0, 0.0).astype(BF16) for mk in (causal, mid, same)], axis=0)

    for c in range(m // chunk):
        r0 = c * chunk
        rows = slice(row0 + r0, row0 + r0 + chunk)
        sums = _prefix_sums(pat, lg[rows])
        bcum, bref, btot = sums[:chunk], sums[chunk:2 * chunk], sums[2 * chunk:]
        q = qk[rows, 0:GLA_DK] * (GLA_HK ** -0.5)
        k = qk[rows, GLA_DK:2 * GLA_DK]
        q_intra = (q * jnp.exp(bcum - bref)).astype(BF16)
        k_intra = (k * jnp.exp(bref - bcum)).astype(BF16)
        q_dec = (q * jnp.exp(bcum)).astype(BF16)
        k_state = (k * jnp.exp(btot - bcum)).astype(BF16)
        decay = jnp.exp(btot)
        for hd in range(GLA_HEADS):
            kc = slice(hd * GLA_HK, (hd + 1) * GLA_HK)
            v_h = v_all[rows, hd * GLA_HV:(hd + 1) * GLA_HV]
            att = lax.dot_general(q_intra[:, kc], k_intra[:, kc], _NT, preferred_element_type=F32)
            att = jnp.where(causal, att, 0.0).astype(BF16)
            o_h = jnp.dot(att, v_h, preferred_element_type=F32)
            o_inter = []
            for sg in range(n_seg):
                srows = slice(sg * seg, (sg + 1) * seg)
                si = (r0 + sg * seg) // tl
                s_old = s_in_ref[0, si, hd]
                o_inter.append(jnp.dot(q_dec[srows, kc], s_old.astype(BF16), preferred_element_type=F32))
                upd = lax.dot_general(k_state[srows, kc], v_h[srows], _TN, preferred_element_type=F32)
                s_out_ref[0, si, hd] = s_old * _decay_cols(decay[sg * seg:sg * seg + 1, kc]) + upd
            finish_head(rows, hd, o_h + jnp.concatenate(o_inter, axis=0), sz)


def _gla_layer(x_p, x_s, mods, li, row0_p, norm_g, gw, final_g, state_s, prev_p, prev_s, *, layer, n_layers,
               tl, tb_s):
    bp, L, _ = x_p.shape
    bs, tl_s, _ = x_s.shape
    n_l = L // tl
    assert bp * n_l * tb_s == bs and MOD_ROWS % tb_s == 0
    has_prev = prev_p is not None
    final = final_g is not None
    m = tl + tb_s * tl_s
    step = lambda bi, l: bi * n_l + l
    tile_p = lambda bi, l: (bi, l, 0)
    tile_s = lambda bi, l: (step(bi, l), 0, 0)
    st_p = pl.BlockSpec((1, 1, GLA_HEADS, GLA_HK, GLA_HV), lambda bi, l: (layer, bi, 0, 0, 0))
    st_s = pl.BlockSpec((1, tb_s, GLA_HEADS, GLA_HK, GLA_HV), lambda bi, l: (layer, step(bi, l), 0, 0, 0))
    in_specs = [
        pl.BlockSpec((1, tl, D_MODEL), tile_p),
        pl.BlockSpec((tb_s, tl_s, D_MODEL), tile_s),
        *_mod_specs(1, li, row0_p),
        *[pl.BlockSpec((None, MOD_ROWS, D_MODEL),
                       lambda bi, l, j=j: (li * 3 + j, (step(bi, l) * tb_s) // MOD_ROWS, 0)) for j in range(3)],
        _const_spec((1, D_MODEL), lead=(li,)),
        _const_spec((D_MODEL, GLA_QKVZ), lead=(layer,)),
        _const_spec((D_MODEL, GLA_GL_PAD), lead=(layer,)),
        _const_spec((GLA_GL_PAD, GLA_DK), lead=(layer,)),
        _const_spec((1, GLA_DK), lead=(layer,)),
        _const_spec((1, GLA_DV), lead=(layer,)),
        _const_spec((GLA_DV, D_MODEL), lead=(layer,)),
    ]
    args = [x_p, x_s, mods, mods, mods, mods, mods, mods, norm_g, gw["w_in"], gw["w_gl"], gw["w_up"],
            gw["b_gate"], gw["norm_g"], gw["w_out"]]
    if final:
        in_specs.append(_const_spec((1, D_MODEL)))
        args.append(final_g)
    in_specs.append(st_s)
    args.append(state_s)
    aliases = {}
    if has_prev:
        in_specs += [pl.BlockSpec(memory_space=pl.ANY)] * 2
        aliases = {len(args): 2, len(args) + 1: 3}
        args += [prev_p, prev_s]
    kern = functools.partial(_gla_kernel, tl=tl, n_l=n_l, tb_s=tb_s, tl_s=tl_s, row0_p=row0_p, has_prev=has_prev,
                             final=final)
    return pl.pallas_call(
        kern,
        grid=(bp, n_l),
        in_specs=in_specs,
        out_specs=[pl.BlockSpec((1, tl, D_MODEL), tile_p), pl.BlockSpec((tb_s, tl_s, D_MODEL), tile_s), st_p, st_s],
        out_shape=[
            jax.ShapeDtypeStruct(x_p.shape, F32),
            jax.ShapeDtypeStruct(x_s.shape, F32),
            jax.ShapeDtypeStruct((n_layers, bp, GLA_HEADS, GLA_HK, GLA_HV), F32),
            jax.ShapeDtypeStruct((n_layers, bs, GLA_HEADS, GLA_HK, GLA_HV), F32),
        ],
        scratch_shapes=[pltpu.VMEM((m, GLA_DV), BF16)],
        input_output_aliases=aliases,
        compiler_params=pltpu.CompilerParams(
            dimension_semantics=("arbitrary", "arbitrary"), vmem_limit_bytes=VMEM_LIMIT_BYTES),
        name=f"gla_layer{layer}",
    )(*args)


def _run_trunk(x_p, x_s, mods, row0_p, pool_state_s, gla_state_s, pos0_s, w, *, tl_p, tb_pool_s, tb_gla_s):
    n_pool, n_gla = (DEPTH + 1) // 2, DEPTH // 2
    pool_p = pool_s = gla_p = gla_s = None
    for li in range(DEPTH):
        j = li // 2
        if li % 2 == 0:
            pool_w = (w["norm_g"], w["pool_w_in"], w["pool_w_grp"], w["pool_scale"], w["pool_w_out"])
            x_p, pool_p = _pool_layer(x_p, mods, li, row0_p, *pool_w, None, pool_p,
                                      layer=j, n_layers=n_pool, tb=1, tl=tl_p, pos0=0)
            x_s, pool_s = _pool_layer(x_s, mods, li, 0, *pool_w, pool_state_s, pool_s,
                                      layer=j, n_layers=n_pool, tb=tb_pool_s, tl=x_s.shape[1], pos0=pos0_s)
        else:
            x_p, x_s, gla_p, gla_s = _gla_layer(
                x_p, x_s, mods, li, row0_p, w["norm_g"], w["gla"], w["final_g"] if li == DEPTH - 1 else None,
                gla_state_s, gla_p, gla_s, layer=j, n_layers=n_gla, tl=tl_p, tb_s=tb_gla_s)
    return x_p, x_s, pool_p, gla_p, pool_s, gla_s


def kernel(x_prompt, x_sample, c_prompt, c_sample, state_pool, state_gla, ada_w, ada_b, norm_g, pool_w_in,
           pool_w_grp, pool_scale, pool_w_out, gla_w_in, gla_w_gate_up, gla_b_gate, gla_norm_g, gla_w_out,
           final_g):
    n_sample = x_sample.shape[0]
    n_gla = gla_w_in.shape[0]
    c_all = jnp.concatenate([c_sample, c_prompt], axis=0)
    mods = _ada_mods(c_all, ada_w, ada_b).reshape(DEPTH * 3, c_all.shape[0], D_MODEL)

    assert gla_w_in.shape[-1] == GLA_QKVZ + GLA_RANK
    w_gl_pad = jnp.concatenate(
        [gla_w_in[:, :, GLA_QKVZ:], jnp.zeros((n_gla, D_MODEL, GLA_GL_PAD - GLA_RANK), gla_w_in.dtype)], axis=-1)
    w_up_pad = jnp.concatenate(
        [gla_w_gate_up, jnp.zeros((n_gla, GLA_GL_PAD - GLA_RANK, GLA_DK), gla_w_gate_up.dtype)], axis=1)
    w = {
        "norm_g": norm_g.reshape(DEPTH, 1, D_MODEL),
        "pool_w_in": pool_w_in.astype(BF16),
        "pool_w_grp": pool_w_grp.astype(BF16),
        "pool_scale": pool_scale.reshape(-1, 1, D_INNER),
        "pool_w_out": pool_w_out.astype(BF16),
        "gla": {
            "w_in": gla_w_in[:, :, :GLA_QKVZ].astype(BF16),
            "w_gl": w_gl_pad.astype(BF16),
            "w_up": w_up_pad.astype(BF16),
            "b_gate": gla_b_gate.reshape(-1, 1, GLA_DK),
            "norm_g": gla_norm_g.reshape(-1, 1, GLA_DV),
            "w_out": gla_w_out.astype(BF16),
        },
        "final_g": final_g.reshape(1, D_MODEL),
    }
    y_p, y_s, pool_p, gla_p, pool_s, gla_s = _run_trunk(
        x_prompt, x_sample, mods, n_sample, state_pool.transpose(0, 2, 1, 3), state_gla, PAST_LEN, w,
        tl_p=512, tb_pool_s=32, tb_gla_s=4)
    return (y_p, y_s, pool_p, gla_p, pool_s.transpose(0, 2, 1, 3), gla_s)
```

```python
import functools

import jax
import jax.numpy as jnp
from jax import lax
from jax.experimental import pallas as pl
from jax.experimental.pallas import tpu as pltpu

D_MODEL = 1024
DEPTH = 4
PAST_LEN = 16384
D_INNER = 2048
POOL_WINDOWS = (2, 4, 8, 16)
POOL_GROUP = D_INNER // len(POOL_WINDOWS)
POOL_BUF = max(POOL_WINDOWS) - 1
POOL_HALO = POOL_BUF + 1
GLA_HEADS = 4
GLA_DK = 512
GLA_DV = D_INNER
GLA_HK = GLA_DK // GLA_HEADS
GLA_HV = GLA_DV // GLA_HEADS
GLA_RANK = 16
GLA_GATE_NORM = 16.0
GLA_CHUNK = 64
GLA_GL_PAD = 128
GLA_QKVZ = 2 * GLA_DK + 2 * GLA_DV
MOD_ROWS = 8
EPS = 1e-6
F32 = jnp.float32
BF16 = jnp.bfloat16

VMEM_LIMIT_BYTES = 62 * 1024 * 1024

_NT = (((1,), (1,)), ((), ()))
_TN = (((0,), (0,)), ((), ()))


def _silu(v):
    return v / (1.0 + jnp.exp(-v))


def _mod_norm(x, g, shift, scale):
    ms = jnp.mean(x * x, axis=-1, keepdims=True)
    y = x * lax.rsqrt(ms + EPS) * g
    return y * (1.0 + scale) + shift


def _const_spec(shape, lead=(), tail=None):
    idx = tuple(lead) + tuple(tail if tail is not None else (0,) * len(shape))
    return pl.BlockSpec((None,) * len(lead) + tuple(shape), lambda *_: idx, pipeline_mode=pl.Buffered(1))


def _mod_specs(tb, li, row0):
    rows = max(tb, MOD_ROWS)
    assert row0 % rows == 0 and rows % tb == 0
    return [pl.BlockSpec((None, rows, D_MODEL), lambda bi, *_, j=j: (li * 3 + j, (row0 + bi * tb) // rows, 0))
            for j in range(3)]


def _load_mods(refs, tb, row0):
    if tb >= MOD_ROWS:
        return [ref[...][:, None, :] for ref in refs]
    assert tb == 1
    r = (row0 + pl.program_id(0)) % MOD_ROWS
    return [ref[pl.ds(r, 1), :] for ref in refs]


def _ada_kernel(c_ref, w_ref, b_ref, o_ref):
    cs = _silu(c_ref[...]).astype(BF16)
    for j in range(3):
        cols = slice(j * D_MODEL, (j + 1) * D_MODEL)
        o_ref[0, j] = jnp.dot(cs, w_ref[0, :, cols].astype(BF16), preferred_element_type=F32) + b_ref[0, :, cols]


def _ada_mods(c_all, ada_w, ada_b):
    nb = c_all.shape[0]
    return pl.pallas_call(
        _ada_kernel,
        grid=(DEPTH,),
        in_specs=[
            pl.BlockSpec((nb, D_MODEL), lambda li: (0, 0)),
            pl.BlockSpec((1, D_MODEL, 3 * D_MODEL), lambda li: (li, 0, 0)),
            pl.BlockSpec((1, 1, 3 * D_MODEL), lambda li: (li, 0, 0)),
        ],
        out_specs=pl.BlockSpec((1, 3, nb, D_MODEL), lambda li: (li, 0, 0, 0)),
        out_shape=jax.ShapeDtypeStruct((DEPTH, 3, nb, D_MODEL), F32),
        compiler_params=pltpu.CompilerParams(vmem_limit_bytes=VMEM_LIMIT_BYTES),
        name="ada_mods",
    )(c_all, ada_w, ada_b.reshape(DEPTH, 1, 3 * D_MODEL))


def _pool_mix(ext_ref, uz, wgrp_ref, pscale_ref, gated_ref, *, halo, time_axis, rows, pos):
    m = uz.shape[0]

    def rows_of(a, lo, n):
        idx = [slice(None)] * 3
        idx[time_axis] = slice(lo, lo + n)
        return a[tuple(idx)]

    def window_sum(c0, w):
        idx = [slice(None)] * 3
        idx[2] = slice(c0, c0 + POOL_GROUP)
        acc = ext_ref[tuple(idx)]
        first, k = 0, 1
        while k < w:
            n = acc.shape[time_axis] - k
            acc = rows_of(acc, k, n) + rows_of(acc, 0, n)
            first, k = first + k, 2 * k
        return rows_of(acc, halo - first, rows)

    diffs = []
    for gi, w in enumerate(POOL_WINDOWS):
        c0 = gi * POOL_GROUP
        u_idx = [slice(None)] * 3
        u_idx[time_axis] = slice(halo, halo + rows)
        u_idx[2] = slice(c0, c0 + POOL_GROUP)
        u_g = ext_ref[tuple(u_idx)]
        s = window_sum(c0, w)
        mean = s * (1.0 / w) if pos is None else s / jnp.minimum(pos + 1, w).astype(F32)
        diffs.append((mean - u_g).reshape(m, POOL_GROUP).astype(BF16))
    mixed = [jnp.dot(diffs[gi], wgrp_ref[gi], preferred_element_type=F32) for gi in range(len(POOL_WINDOWS))]
    for gi in range(len(POOL_WINDOWS)):
        c0 = gi * POOL_GROUP
        z_g = uz[:, D_INNER + c0:D_INNER + c0 + POOL_GROUP]
        gated_ref[:, c0:c0 + POOL_GROUP] = (
            mixed[gi] * pscale_ref[:, c0:c0 + POOL_GROUP] * _silu(z_g)).astype(BF16)


def _pool_kernel(*refs, tl, n_l, row0, has_prev):
    it = iter(refs)
    x_ref = next(it)
    mod_refs = [next(it) for _ in range(3)]
    g_ref, win_ref, wgrp_ref, pscale_ref, wout_ref = (next(it) for _ in range(5))
    if has_prev:
        next(it)
    y_ref, nbuf_ref, ext_ref, gated_ref = (next(it) for _ in range(4))

    l = pl.program_id(1)

    @pl.when(l == 0)
    def _():
        ext_ref[:, 0:POOL_HALO, :] = jnp.zeros((1, POOL_HALO, D_INNER), F32)

    shift, scale, gate = _load_mods(mod_refs, 1, row0)
    x = x_ref[...]
    h = _mod_norm(x, g_ref[...], shift, scale).reshape(tl, D_MODEL).astype(BF16)
    uz = jnp.dot(h, win_ref[...], preferred_element_type=F32)
    ext_ref[:, POOL_HALO:POOL_HALO + tl, :] = uz[:, :D_INNER].reshape(1, tl, D_INNER)
    pos = l * tl + lax.broadcasted_iota(jnp.int32, (1, tl, POOL_GROUP), 1)
    _pool_mix(ext_ref, uz, wgrp_ref, pscale_ref, gated_ref, halo=POOL_HALO, time_axis=1, rows=tl, pos=pos)
    y = jnp.dot(gated_ref[...], wout_ref[...], preferred_element_type=F32)
    y_ref[...] = x + gate * y.reshape(1, tl, D_MODEL)

    @pl.when(l == n_l - 1)
    def _():
        nbuf_ref[0] = ext_ref[:, tl + 1:tl + POOL_HALO, :]

    if n_l > 1:
        @pl.when(l < n_l - 1)
        def _():
            ext_ref[:, 0:POOL_HALO, :] = ext_ref[:, tl:tl + POOL_HALO, :]


def _pool_cont_kernel(*refs, tb, tl, pos0, has_prev):
    it = iter(refs)
    x_ref = next(it)
    mod_refs = [next(it) for _ in range(3)]
    g_ref, win_ref, wgrp_ref, pscale_ref, wout_ref, buf_ref = (next(it) for _ in range(6))
    if has_prev:
        next(it)
    y_ref, nbuf_ref, ext_ref, gated_ref = (next(it) for _ in range(4))
    m = tb * tl

    shift, scale, gate = _load_mods(mod_refs, tb, 0)
    x = x_ref[...]
    h = jnp.transpose(_mod_norm(x, g_ref[...], shift, scale), (1, 0, 2))
    uz = jnp.dot(h.reshape(m, D_MODEL).astype(BF16), win_ref[...], preferred_element_type=F32)
    ext_ref[0:POOL_BUF] = buf_ref[...]
    ext_ref[POOL_BUF:POOL_BUF + tl] = uz[:, :D_INNER].reshape(tl, tb, D_INNER)
    pos = None
    if pos0 < POOL_BUF:
        pos = pos0 + lax.broadcasted_iota(jnp.int32, (tl, tb, POOL_GROUP), 0)
    _pool_mix(ext_ref, uz, wgrp_ref, pscale_ref, gated_ref, halo=POOL_BUF, time_axis=0, rows=tl, pos=pos)
    y = jnp.dot(gated_ref[...], wout_ref[...], preferred_element_type=F32)
    y_ref[...] = x + gate * jnp.transpose(y.reshape(tl, tb, D_MODEL), (1, 0, 2))
    nbuf_ref[...] = ext_ref[tl:tl + POOL_BUF]


def _pool_layer(x, mods, li, row0, norm_g, w_in, w_grp, pscale, w_out, state, prev_out, *, layer, n_layers,
                tb, tl, pos0):
    b, L, _ = x.shape
    fresh = state is None
    has_prev = prev_out is not None
    in_specs = [
        None,
        *_mod_specs(tb, li, row0),
        _const_spec((1, D_MODEL), lead=(li,)),
        _const_spec((D_MODEL, 2 * D_INNER), lead=(layer,)),
        _const_spec((len(POOL_WINDOWS), POOL_GROUP, POOL_GROUP), lead=(layer,)),
        _const_spec((1, D_INNER), lead=(layer,)),
        _const_spec((D_INNER, D_MODEL), lead=(layer,)),
    ]
    args = [x, mods, mods, mods, norm_g, w_in, w_grp, pscale, w_out]
    if fresh:
        assert tb == 1 and pos0 == 0
        n_l = L // tl
        grid = (b, n_l)
        x_spec = pl.BlockSpec((1, tl, D_MODEL), lambda bi, l: (bi, l, 0))
        buf_shape = (n_layers, b, POOL_BUF, D_INNER)
        buf_spec = pl.BlockSpec((1, 1, POOL_BUF, D_INNER), lambda bi, l: (layer, bi, 0, 0))
        ext_shape = (1, POOL_HALO + tl, D_INNER)
        kern = functools.partial(_pool_kernel, tl=tl, n_l=n_l, row0=row0, has_prev=has_prev)
        semantics = ("arbitrary", "arbitrary")
    else:
        assert tl == L and row0 == 0
        grid = (b // tb,)
        x_spec = pl.BlockSpec((tb, tl, D_MODEL), lambda bi: (bi, 0, 0))
        buf_shape = (n_layers, POOL_BUF, b, D_INNER)
        buf_spec = pl.BlockSpec((None, POOL_BUF, tb, D_INNER), lambda bi: (layer, 0, bi, 0))
        ext_shape = (POOL_BUF + tl, tb, D_INNER)
        in_specs.append(buf_spec)
        args.append(state)
        kern = functools.partial(_pool_cont_kernel, tb=tb, tl=tl, pos0=pos0, has_prev=has_prev)
        semantics = ("arbitrary",)
    in_specs[0] = x_spec
    aliases = {}
    if has_prev:
        in_specs.append(pl.BlockSpec(memory_space=pl.ANY))
        aliases = {len(args): 1}
        args.append(prev_out)
    return pl.pallas_call(
        kern,
        grid=grid,
        in_specs=in_specs,
        out_specs=[x_spec, buf_spec],
        out_shape=[jax.ShapeDtypeStruct(x.shape, F32), jax.ShapeDtypeStruct(buf_shape, F32)],
        scratch_shapes=[pltpu.VMEM(ext_shape, F32), pltpu.VMEM((tb * tl, D_INNER), BF16)],
        input_output_aliases=aliases,
        compiler_params=pltpu.CompilerParams(dimension_semantics=semantics, vmem_limit_bytes=VMEM_LIMIT_BYTES),
        name=f"pool_layer{layer}_{'fresh' if fresh else 'cont'}",
    )(*args)


def _gla_kernel(*refs, tl, n_l, tb_s, tl_s, row0_p, has_prev, final):
    it = iter(refs)
    xp_ref, xs_ref = next(it), next(it)
    modp_refs = [next(it) for _ in range(3)]
    mods_refs = [next(it) for _ in range(3)]
    g_ref, win_ref, wgl_ref, wup_ref, bg_ref, ng_ref, wout_ref = (next(it) for _ in range(7))
    fg_ref = next(it) if final else None
    s0_ref = next(it)
    if has_prev:
        next(it), next(it)
    yp_ref, ys_ref, sp_ref, ss_ref, gated_ref = (next(it) for _ in range(5))

    l = pl.program_id(1)
    step = pl.program_id(0) * n_l + l
    m_p, m_s = tl, tb_s * tl_s

    @pl.when(l == 0)
    def _():
        sp_ref[...] = jnp.zeros(sp_ref.shape, F32)

    shift_p, scale_p, gate_p = _load_mods(modp_refs, 1, row0_p)
    first = (step * tb_s) % MOD_ROWS
    shift_s, scale_s, gate_s = [
        sum(jnp.where(first == r0, ref[r0:r0 + tb_s, :], 0.0) for r0 in range(0, MOD_ROWS, tb_s))[:, None, :]
        for ref in mods_refs]
    xp, xs = xp_ref[...], xs_ref[...]
    h = jnp.concatenate([
        _mod_norm(xp, g_ref[...], shift_p, scale_p).reshape(m_p, D_MODEL),
        _mod_norm(xs, g_ref[...], shift_s, scale_s).reshape(m_s, D_MODEL)], axis=0).astype(BF16)
    qk = jnp.dot(h, win_ref[:, :2 * GLA_DK], preferred_element_type=F32)
    gl = jnp.dot(h, wgl_ref[...], preferred_element_type=F32).astype(BF16)
    gate_pre = jnp.dot(gl, wup_ref[...], preferred_element_type=F32) + bg_ref[...]
    lg = (jnp.minimum(gate_pre, 0.0) - jnp.log(1.0 + jnp.exp(-jnp.abs(gate_pre)))) * (1.0 / GLA_GATE_NORM)

    def project_v():
        return jnp.dot(h, win_ref[:, 2 * GLA_DK:2 * GLA_DK + GLA_DV], preferred_element_type=F32).astype(BF16)

    def project_sz():
        return _silu(jnp.dot(h, win_ref[:, 2 * GLA_DK + GLA_DV:], preferred_element_type=F32)).astype(BF16)

    def finish_head(rows, hd, o_h, sz):
        o_h = o_h * lax.rsqrt(jnp.mean(o_h * o_h, axis=-1, keepdims=True) + EPS)
        hc = slice(hd * GLA_HV, (hd + 1) * GLA_HV)
        gated_ref[rows, hc] = o_h.astype(BF16) * ng_ref[:, hc].astype(BF16) * sz[rows, hc]

    v_all, sz = _gla_long_chunks(qk, lg, project_v, project_sz, sp_ref, finish_head, m=m_p, tl=tl)
    _gla_short_chunks(qk, lg, v_all, sz, s0_ref, ss_ref, finish_head, row0=m_p, m=m_s, tl=tl_s)

    y = jnp.dot(gated_ref[...], wout_ref[...], preferred_element_type=F32)
    out_p = xp + gate_p * y[:m_p].reshape(1, tl, D_MODEL)
    out_s = xs + gate_s * y[m_p:].reshape(tb_s, tl_s, D_MODEL)
    if final:
        out_p = out_p * lax.rsqrt(jnp.mean(out_p * out_p, axis=-1, keepdims=True) + EPS) * fg_ref[...]
        out_s = out_s * lax.rsqrt(jnp.mean(out_s * out_s, axis=-1, keepdims=True) + EPS) * fg_ref[...]
    yp_ref[...] = out_p
    ys_ref[...] = out_s


def _prefix_sums(pat, lg_rows):
    hi = lg_rows.astype(BF16)
    lo = (lg_rows - hi.astype(F32)).astype(BF16)
    return jnp.dot(jnp.concatenate([pat, pat], axis=1), jnp.concatenate([hi, lo], axis=0),
                   preferred_element_type=F32)


def _decay_cols(decay_row):
    d_col = jnp.transpose(jnp.broadcast_to(decay_row, (GLA_HK, GLA_HK)))
    return jnp.concatenate([d_col] * (GLA_HV // GLA_HK), axis=1)


def _gla_long_chunks(qk, lg, project_v, project_sz, s_ref, finish_head, *, m, tl):
    hb = GLA_CHUNK
    cr = 2 * hb
    ri = lax.broadcasted_iota(jnp.int32, (cr, cr), 0)
    ci = lax.broadcasted_iota(jnp.int32, (cr, cr), 1)
    causal = ci <= ri
    tri = jnp.where(causal, 1.0, 0.0).astype(BF16)
    zeros = jnp.zeros((hb, GLA_HK), BF16)
    heads = range(GLA_HEADS)
    kcs = [slice(hd * GLA_HK, (hd + 1) * GLA_HK) for hd in heads]

    def prepare(c):
        rows = slice(c * cr, (c + 1) * cr)
        bcum = _prefix_sums(tri, lg[rows])
        b_a, b_b = bcum[:hb], bcum[hb:]
        ref_a, bnd = bcum[hb // 2 - 1:hb // 2], bcum[hb - 1:hb]
        ref_b, tot = bcum[hb + hb // 2 - 1:hb + hb // 2], bcum[cr - 1:cr]
        q = qk[rows, 0:GLA_DK] * (GLA_HK ** -0.5)
        k = qk[rows, GLA_DK:2 * GLA_DK]
        q_a, q_b, k_a, k_b = q[:hb], q[hb:], k[:hb], k[hb:]
        qi_a = (q_a * jnp.exp(b_a - ref_a)).astype(BF16)
        ki_a = (k_a * jnp.exp(ref_a - b_a)).astype(BF16)
        qi_b = (q_b * jnp.exp(b_b - ref_b)).astype(BF16)
        ki_b = (k_b * jnp.exp(ref_b - b_b)).astype(BF16)
        qx_b = (q_b * jnp.exp(b_b - bnd)).astype(BF16)
        kx_a = (k_a * jnp.exp(bnd - b_a)).astype(BF16)
        q_dec = (q * jnp.exp(bcum)).astype(BF16)
        k_state = (k * jnp.exp(tot - bcum)).astype(BF16)
        decay = jnp.exp(tot)
        return qi_a, ki_a, qi_b, ki_b, qx_b, kx_a, q_dec, k_state, decay

    sz = project_sz()
    prepared = [prepare(c) for c in range(m // cr)]
    v_all = project_v()
    for c, (qi_a, ki_a, qi_b, ki_b, qx_b, kx_a, q_dec, k_state, decay) in enumerate(prepared):
        rows = slice(c * cr, (c + 1) * cr)
        si = (c * cr) // tl
        v = [v_all[rows, hd * GLA_HV:(hd + 1) * GLA_HV] for hd in heads]
        s_old = [s_ref[0, si, hd] for hd in heads]
        att_a = [lax.dot_general(qi_a[:, kc], jnp.concatenate([ki_a[:, kc], zeros], axis=0), _NT,
                                 preferred_element_type=F32) for kc in kcs]
        att_b = [lax.dot_general(
            jnp.concatenate([qx_b[:, kc], qi_b[:, kc]], axis=1),
            jnp.concatenate([jnp.concatenate([kx_a[:, kc], zeros], axis=1),
                             jnp.concatenate([zeros, ki_b[:, kc]], axis=1)], axis=0),
            _NT, preferred_element_type=F32) for kc in kcs]
        upd = [lax.dot_general(k_state[:, kc], v[hd], _TN, preferred_element_type=F32)
               for hd, kc in enumerate(kcs)]
        att = [jnp.where(causal, jnp.concatenate([att_a[hd], att_b[hd]], axis=0), 0.0).astype(BF16)
               for hd in heads]
        o = [jnp.dot(jnp.concatenate([att[hd], q_dec[:, kc]], axis=1),
                     jnp.concatenate([v[hd], s_old[hd].astype(BF16)], axis=0), preferred_element_type=F32)
             for hd, kc in enumerate(kcs)]
        for hd, kc in enumerate(kcs):
            s_ref[0, si, hd] = s_old[hd] * _decay_cols(decay[:, kc]) + upd[hd]
        for hd in heads:
            finish_head(rows, hd, o[hd], sz)
    return v_all, sz


def _gla_short_chunks(qk, lg, v_all, sz, s_in_ref, s_out_ref, finish_head, *, row0, m, tl):
    seg = tl
    chunk = min(GLA_CHUNK, m)
    n_seg = chunk // seg
    ri = lax.broadcasted_iota(jnp.int32, (chunk, chunk), 0)
    ci = lax.broadcasted_iota(jnp.int32, (chunk, chunk), 1)
    sh = seg.bit_length() - 1
    same = (ri >> sh) == (ci >> sh)
    causal = same & (ci <= ri)
    mid = same & ((ci & (seg - 1)) <= (seg // 2 - 1))
    pat = jnp.concatenate([jnp.where(mk, 1.0, 0.0).astype(BF16) for mk in (causal, mid, same)], axis=0)

    for c in range(m // chunk):
        r0 = c * chunk
        rows = slice(row0 + r0, row0 + r0 + chunk)
        sums = _prefix_sums(pat, lg[rows])
        bcum, bref, btot = sums[:chunk], sums[chunk:2 * chunk], sums[2 * chunk:]
        q = qk[rows, 0:GLA_DK] * (GLA_HK ** -0.5)
        k = qk[rows, GLA_DK:2 * GLA_DK]
        q_intra = (q * jnp.exp(bcum - bref)).astype(BF16)
        k_intra = (k * jnp.exp(bref - bcum)).astype(BF16)
        q_dec = (q * jnp.exp(bcum)).astype(BF16)
        k_state = (k * jnp.exp(btot - bcum)).astype(BF16)
        decay = jnp.exp(btot)
        for hd in range(GLA_HEADS):
            kc = slice(hd * GLA_HK, (hd + 1) * GLA_HK)
            v_h = v_all[rows, hd * GLA_HV:(hd + 1) * GLA_HV]
            att = lax.dot_general(q_intra[:, kc], k_intra[:, kc], _NT, preferred_element_type=F32)
            att = jnp.where(causal, att, 0.0).astype(BF16)
            o_h = jnp.dot(att, v_h, preferred_element_type=F32)
            o_inter = []
            for sg in range(n_seg):
                srows = slice(sg * seg, (sg + 1) * seg)
                si = (r0 + sg * seg) // tl
                s_old = s_in_ref[0, si, hd]
                o_inter.append(jnp.dot(q_dec[srows, kc], s_old.astype(BF16), preferred_element_type=F32))
                upd = lax.dot_general(k_state[srows, kc], v_h[srows], _TN, preferred_element_type=F32)
                s_out_ref[0, si, hd] = s_old * _decay_cols(decay[sg * seg:sg * seg + 1, kc]) + upd
            finish_head(rows, hd, o_h + jnp.concatenate(o_inter, axis=0), sz)


def _gla_layer(x_p, x_s, mods, li, row0_p, norm_g, gw, final_g, state_s, prev_p, prev_s, *, layer, n_layers,
               tl, tb_s):
    bp, L, _ = x_p.shape
    bs, tl_s, _ = x_s.shape
    n_l = L // tl
    assert bp * n_l * tb_s == bs and MOD_ROWS % tb_s == 0
    has_prev = prev_p is not None
    final = final_g is not None
    m = tl + tb_s * tl_s
    step = lambda bi, l: bi * n_l + l
    tile_p = lambda bi, l: (bi, l, 0)
    tile_s = lambda bi, l: (step(bi, l), 0, 0)
    st_p = pl.BlockSpec((1, 1, GLA_HEADS, GLA_HK, GLA_HV), lambda bi, l: (layer, bi, 0, 0, 0))
    st_s = pl.BlockSpec((1, tb_s, GLA_HEADS, GLA_HK, GLA_HV), lambda bi, l: (layer, step(bi, l), 0, 0, 0))
    in_specs = [
        pl.BlockSpec((1, tl, D_MODEL), tile_p),
        pl.BlockSpec((tb_s, tl_s, D_MODEL), tile_s),
        *_mod_specs(1, li, row0_p),
        *[pl.BlockSpec((None, MOD_ROWS, D_MODEL),
                       lambda bi, l, j=j: (li * 3 + j, (step(bi, l) * tb_s) // MOD_ROWS, 0)) for j in range(3)],
        _const_spec((1, D_MODEL), lead=(li,)),
        _const_spec((D_MODEL, GLA_QKVZ), lead=(layer,)),
        _const_spec((D_MODEL, GLA_GL_PAD), lead=(layer,)),
        _const_spec((GLA_GL_PAD, GLA_DK), lead=(layer,)),
        _const_spec((1, GLA_DK), lead=(layer,)),
        _const_spec((1, GLA_DV), lead=(layer,)),
        _const_spec((GLA_DV, D_MODEL), lead=(layer,)),
    ]
    args = [x_p, x_s, mods, mods, mods, mods, mods, mods, norm_g, gw["w_in"], gw["w_gl"], gw["w_up"],
            gw["b_gate"], gw["norm_g"], gw["w_out"]]
    if final:
        in_specs.append(_const_spec((1, D_MODEL)))
        args.append(final_g)
    in_specs.append(st_s)
    args.append(state_s)
    aliases = {}
    if has_prev:
        in_specs += [pl.BlockSpec(memory_space=pl.ANY)] * 2
        aliases = {len(args): 2, len(args) + 1: 3}
        args += [prev_p, prev_s]
    kern = functools.partial(_gla_kernel, tl=tl, n_l=n_l, tb_s=tb_s, tl_s=tl_s, row0_p=row0_p, has_prev=has_prev,
                             final=final)
    return pl.pallas_call(
        kern,
        grid=(bp, n_l),
        in_specs=in_specs,
        out_specs=[pl.BlockSpec((1, tl, D_MODEL), tile_p), pl.BlockSpec((tb_s, tl_s, D_MODEL), tile_s), st_p, st_s],
        out_shape=[
            jax.ShapeDtypeStruct(x_p.shape, F32),
            jax.ShapeDtypeStruct(x_s.shape, F32),
            jax.ShapeDtypeStruct((n_layers, bp, GLA_HEADS, GLA_HK, GLA_HV), F32),
            jax.ShapeDtypeStruct((n_layers, bs, GLA_HEADS, GLA_HK, GLA_HV), F32),
        ],
        scratch_shapes=[pltpu.VMEM((m, GLA_DV), BF16)],
        input_output_aliases=aliases,
        compiler_params=pltpu.CompilerParams(
            dimension_semantics=("arbitrary", "arbitrary"), vmem_limit_bytes=VMEM_LIMIT_BYTES),
        name=f"gla_layer{layer}",
    )(*args)


def _run_trunk(x_p, x_s, mods, row0_p, pool_state_s, gla_state_s, pos0_s, w, *, tl_p, tb_pool_s, tb_gla_s):
    n_pool, n_gla = (DEPTH + 1) // 2, DEPTH // 2
    pool_p = pool_s = gla_p = gla_s = None
    for li in range(DEPTH):
        j = li // 2
        if li % 2 == 0:
            pool_w = (w["norm_g"], w["pool_w_in"], w["pool_w_grp"], w["pool_scale"], w["pool_w_out"])
            x_p, pool_p = _pool_layer(x_p, mods, li, row0_p, *pool_w, None, pool_p,
                                      layer=j, n_layers=n_pool, tb=1, tl=tl_p, pos0=0)
            x_s, pool_s = _pool_layer(x_s, mods, li, 0, *pool_w, pool_state_s, pool_s,
                                      layer=j, n_layers=n_pool, tb=tb_pool_s, tl=x_s.shape[1], pos0=pos0_s)
        else:
            x_p, x_s, gla_p, gla_s = _gla_layer(
                x_p, x_s, mods, li, row0_p, w["norm_g"], w["gla"], w["final_g"] if li == DEPTH - 1 else None,
                gla_state_s, gla_p, gla_s, layer=j, n_layers=n_gla, tl=tl_p, tb_s=tb_gla_s)
    return x_p, x_s, pool_p, gla_p, pool_s, gla_s


def kernel(x_prompt, x_sample, c_prompt, c_sample, state_pool, state_gla, ada_w, ada_b, norm_g, pool_w_in,
           pool_w_grp, pool_scale, pool_w_out, gla_w_in, gla_w_gate_up, gla_b_gate, gla_norm_g, gla_w_out,
           final_g):
    n_sample = x_sample.shape[0]
    n_gla = gla_w_in.shape[0]
    c_all = jnp.concatenate([c_sample, c_prompt], axis=0)
    mods = _ada_mods(c_all, ada_w, ada_b).reshape(DEPTH * 3, c_all.shape[0], D_MODEL)

    assert gla_w_in.shape[-1] == GLA_QKVZ + GLA_RANK
    w_gl_pad = jnp.concatenate(
        [gla_w_in[:, :, GLA_QKVZ:], jnp.zeros((n_gla, D_MODEL, GLA_GL_PAD - GLA_RANK), gla_w_in.dtype)], axis=-1)
    w_up_pad = jnp.concatenate(
        [gla_w_gate_up, jnp.zeros((n_gla, GLA_GL_PAD - GLA_RANK, GLA_DK), gla_w_gate_up.dtype)], axis=1)
    w = {
        "norm_g": norm_g.reshape(DEPTH, 1, D_MODEL),
        "pool_w_in": pool_w_in.astype(BF16),
        "pool_w_grp": pool_w_grp.astype(BF16),
        "pool_scale": pool_scale.reshape(-1, 1, D_INNER),
        "pool_w_out": pool_w_out.astype(BF16),
        "gla": {
            "w_in": gla_w_in.astype(BF16),
            "w_gl": w_gl_pad.astype(BF16),
            "w_up": w_up_pad.astype(BF16),
            "b_gate": gla_b_gate.reshape(-1, 1, GLA_DK),
            "norm_g": gla_norm_g.reshape(-1, 1, GLA_DV),
            "w_out": gla_w_out.astype(BF16),
        },
        "final_g": final_g.reshape(1, D_MODEL),
    }
    y_p, y_s, pool_p, gla_p, pool_s, gla_s = _run_trunk(
        x_prompt, x_sample, mods, n_sample, state_pool.transpose(0, 2, 1, 3), state_gla, PAST_LEN, w,
        tl_p=512, tb_pool_s=32, tb_gla_s=4)
    return (y_p, y_s, pool_p, gla_p, pool_s.transpose(0, 2, 1, 3), gla_s)
```

```python
import functools

import jax
import jax.numpy as jnp
from jax import lax
from jax.experimental import pallas as pl
from jax.experimental.pallas import tpu as pltpu

D_MODEL = 1024
DEPTH = 4
PAST_LEN = 16384
D_INNER = 2048
POOL_WINDOWS = (2, 4, 8, 16)
POOL_GROUP = D_INNER // len(POOL_WINDOWS)
POOL_BUF = max(POOL_WINDOWS) - 1
POOL_HALO = POOL_BUF + 1
GLA_HEADS = 4
GLA_DK = 512
GLA_DV = D_INNER
GLA_HK = GLA_DK // GLA_HEADS
GLA_HV = GLA_DV // GLA_HEADS
GLA_RANK = 16
GLA_GATE_NORM = 16.0
GLA_CHUNK = 64
GLA_GL_PAD = 128
GLA_QKVZ = 2 * GLA_DK + 2 * GLA_DV
MOD_ROWS = 8
EPS = 1e-6
F32 = jnp.float32
BF16 = jnp.bfloat16

VMEM_LIMIT_BYTES = 62 * 1024 * 1024

_NT = (((1,), (1,)), ((), ()))
_TN = (((0,), (0,)), ((), ()))


def _silu(v):
    return v / (1.0 + jnp.exp(-v))


def _mod_norm(x, g, shift, scale):
    ms = jnp.mean(x * x, axis=-1, keepdims=True)
    y = x * lax.rsqrt(ms + EPS) * g
    return y * (1.0 + scale) + shift


def _const_spec(shape, lead=(), tail=None):
    idx = tuple(lead) + tuple(tail if tail is not None else (0,) * len(shape))
    return pl.BlockSpec((None,) * len(lead) + tuple(shape), lambda *_: idx, pipeline_mode=pl.Buffered(1))


def _mod_specs(tb, li, row0):
    rows = max(tb, MOD_ROWS)
    assert row0 % rows == 0 and rows % tb == 0
    return [pl.BlockSpec((None, rows, D_MODEL), lambda bi, *_, j=j: (li * 3 + j, (row0 + bi * tb) // rows, 0))
            for j in range(3)]


def _load_mods(refs, tb, row0):
    if tb >= MOD_ROWS:
        return [ref[...][:, None, :] for ref in refs]
    assert tb == 1
    r = (row0 + pl.program_id(0)) % MOD_ROWS
    return [ref[pl.ds(r, 1), :] for ref in refs]


def _ada_kernel(c_ref, w_ref, b_ref, o_ref):
    cs = _silu(c_ref[...]).astype(BF16)
    for j in range(3):
        cols = slice(j * D_MODEL, (j + 1) * D_MODEL)
        o_ref[0, j] = jnp.dot(cs, w_ref[0, :, cols].astype(BF16), preferred_element_type=F32) + b_ref[0, :, cols]


def _ada_mods(c_all, ada_w, ada_b):
    nb = c_all.shape[0]
    return pl.pallas_call(
        _ada_kernel,
        grid=(DEPTH,),
        in_specs=[
            pl.BlockSpec((nb, D_MODEL), lambda li: (0, 0)),
            pl.BlockSpec((1, D_MODEL, 3 * D_MODEL), lambda li: (li, 0, 0)),
            pl.BlockSpec((1, 1, 3 * D_MODEL), lambda li: (li, 0, 0)),
        ],
        out_specs=pl.BlockSpec((1, 3, nb, D_MODEL), lambda li: (li, 0, 0, 0)),
        out_shape=jax.ShapeDtypeStruct((DEPTH, 3, nb, D_MODEL), F32),
        compiler_params=pltpu.CompilerParams(vmem_limit_bytes=VMEM_LIMIT_BYTES),
        name="ada_mods",
    )(c_all, ada_w, ada_b.reshape(DEPTH, 1, 3 * D_MODEL))


def _pool_mix(ext_ref, uz, wgrp_ref, pscale_ref, gated_ref, *, halo, time_axis, rows, pos):
    m = uz.shape[0]

    def rows_of(a, lo, n):
        idx = [slice(None)] * 3
        idx[time_axis] = slice(lo, lo + n)
        return a[tuple(idx)]

    def window_sum(c0, w):
        idx = [slice(None)] * 3
        idx[2] = slice(c0, c0 + POOL_GROUP)
        acc = ext_ref[tuple(idx)]
        first, k = 0, 1
        while k < w:
            n = acc.shape[time_axis] - k
            acc = rows_of(acc, k, n) + rows_of(acc, 0, n)
            first, k = first + k, 2 * k
        return rows_of(acc, halo - first, rows)

    diffs = []
    for gi, w in enumerate(POOL_WINDOWS):
        c0 = gi * POOL_GROUP
        u_idx = [slice(None)] * 3
        u_idx[time_axis] = slice(halo, halo + rows)
        u_idx[2] = slice(c0, c0 + POOL_GROUP)
        u_g = ext_ref[tuple(u_idx)]
        s = window_sum(c0, w)
        mean = s * (1.0 / w) if pos is None else s / jnp.minimum(pos + 1, w).astype(F32)
        diffs.append((mean - u_g).reshape(m, POOL_GROUP).astype(BF16))
    mixed = [jnp.dot(diffs[gi], wgrp_ref[gi], preferred_element_type=F32) for gi in range(len(POOL_WINDOWS))]
    for gi in range(len(POOL_WINDOWS)):
        c0 = gi * POOL_GROUP
        z_g = uz[:, D_INNER + c0:D_INNER + c0 + POOL_GROUP]
        gated_ref[:, c0:c0 + POOL_GROUP] = (
            mixed[gi] * pscale_ref[:, c0:c0 + POOL_GROUP] * _silu(z_g)).astype(BF16)


def _pool_kernel(*refs, tl, n_l, row0, has_prev):
    it = iter(refs)
    x_ref = next(it)
    mod_refs = [next(it) for _ in range(3)]
    g_ref, win_ref, wgrp_ref, pscale_ref, wout_ref = (next(it) for _ in range(5))
    if has_prev:
        next(it)
    y_ref, nbuf_ref, ext_ref, gated_ref = (next(it) for _ in range(4))

    l = pl.program_id(1)

    @pl.when(l == 0)
    def _():
        ext_ref[:, 0:POOL_HALO, :] = jnp.zeros((1, POOL_HALO, D_INNER), F32)

    shift, scale, gate = _load_mods(mod_refs, 1, row0)
    x = x_ref[...]
    h = _mod_norm(x, g_ref[...], shift, scale).reshape(tl, D_MODEL).astype(BF16)
    uz = jnp.dot(h, win_ref[...], preferred_element_type=F32)
    ext_ref[:, POOL_HALO:POOL_HALO + tl, :] = uz[:, :D_INNER].reshape(1, tl, D_INNER)
    pos = l * tl + lax.broadcasted_iota(jnp.int32, (1, tl, POOL_GROUP), 1)
    _pool_mix(ext_ref, uz, wgrp_ref, pscale_ref, gated_ref, halo=POOL_HALO, time_axis=1, rows=tl, pos=pos)
    y = jnp.dot(gated_ref[...], wout_ref[...], preferred_element_type=F32)
    y_ref[...] = x + gate * y.reshape(1, tl, D_MODEL)

    @pl.when(l == n_l - 1)
    def _():
        nbuf_ref[0] = ext_ref[:, tl + 1:tl + POOL_HALO, :]

    if n_l > 1:
        @pl.when(l < n_l - 1)
        def _():
            ext_ref[:, 0:POOL_HALO, :] = ext_ref[:, tl:tl + POOL_HALO, :]


def _pool_cont_kernel(*refs, tb, tl, pos0, has_prev):
    it = iter(refs)
    x_ref = next(it)
    mod_refs = [next(it) for _ in range(3)]
    g_ref, win_ref, wgrp_ref, pscale_ref, wout_ref, buf_ref = (next(it) for _ in range(6))
    if has_prev:
        next(it)
    y_ref, nbuf_ref, ext_ref, gated_ref = (next(it) for _ in range(4))
    m = tb * tl

    shift, scale, gate = _load_mods(mod_refs, tb, 0)
    x = x_ref[...]
    h = jnp.transpose(_mod_norm(x, g_ref[...], shift, scale), (1, 0, 2))
    uz = jnp.dot(h.reshape(m, D_MODEL).astype(BF16), win_ref[...], preferred_element_type=F32)
    ext_ref[0:POOL_BUF] = buf_ref[...]
    ext_ref[POOL_BUF:POOL_BUF + tl] = uz[:, :D_INNER].reshape(tl, tb, D_INNER)
    pos = None
    if pos0 < POOL_BUF:
        pos = pos0 + lax.broadcasted_iota(jnp.int32, (tl, tb, POOL_GROUP), 0)
    _pool_mix(ext_ref, uz, wgrp_ref, pscale_ref, gated_ref, halo=POOL_BUF, time_axis=0, rows=tl, pos=pos)
    y = jnp.dot(gated_ref[...], wout_ref[...], preferred_element_type=F32)
    y_ref[...] = x + gate * jnp.transpose(y.reshape(tl, tb, D_MODEL), (1, 0, 2))
    nbuf_ref[...] = ext_ref[tl:tl + POOL_BUF]


def _pool_layer(x, mods, li, row0, norm_g, w_in, w_grp, pscale, w_out, state, prev_out, *, layer, n_layers,
                tb, tl, pos0):
    b, L, _ = x.shape
    fresh = state is None
    has_prev = prev_out is not None
    in_specs = [
        None,
        *_mod_specs(tb, li, row0),
        _const_spec((1, D_MODEL), lead=(li,)),
        _const_spec((D_MODEL, 2 * D_INNER), lead=(layer,)),
        _const_spec((len(POOL_WINDOWS), POOL_GROUP, POOL_GROUP), lead=(layer,)),
        _const_spec((1, D_INNER), lead=(layer,)),
        _const_spec((D_INNER, D_MODEL), lead=(layer,)),
    ]
    args = [x, mods, mods, mods, norm_g, w_in, w_grp, pscale, w_out]
    if fresh:
        assert tb == 1 and pos0 == 0
        n_l = L // tl
        grid = (b, n_l)
        x_spec = pl.BlockSpec((1, tl, D_MODEL), lambda bi, l: (bi, l, 0))
        buf_shape = (n_layers, b, POOL_BUF, D_INNER)
        buf_spec = pl.BlockSpec((1, 1, POOL_BUF, D_INNER), lambda bi, l: (layer, bi, 0, 0))
        ext_shape = (1, POOL_HALO + tl, D_INNER)
        kern = functools.partial(_pool_kernel, tl=tl, n_l=n_l, row0=row0, has_prev=has_prev)
        semantics = ("arbitrary", "arbitrary")
    else:
        assert tl == L and row0 == 0
        grid = (b // tb,)
        x_spec = pl.BlockSpec((tb, tl, D_MODEL), lambda bi: (bi, 0, 0))
        buf_shape = (n_layers, POOL_BUF, b, D_INNER)
        buf_spec = pl.BlockSpec((None, POOL_BUF, tb, D_INNER), lambda bi: (layer, 0, bi, 0))
        ext_shape = (POOL_BUF + tl, tb, D_INNER)
        in_specs.append(buf_spec)
        args.append(state)
        kern = functools.partial(_pool_cont_kernel, tb=tb, tl=tl, pos0=pos0, has_prev=has_prev)
        semantics = ("arbitrary",)
    in_specs[0] = x_spec
    aliases = {}
    if has_prev:
        in_specs.append(pl.BlockSpec(memory_space=pl.ANY))
        aliases = {len(args): 1}
        args.append(prev_out)
    return pl.pallas_call(
        kern,
        grid=grid,
        in_specs=in_specs,
        out_specs=[x_spec, buf_spec],
        out_shape=[jax.ShapeDtypeStruct(x.shape, F32), jax.ShapeDtypeStruct(buf_shape, F32)],
        scratch_shapes=[pltpu.VMEM(ext_shape, F32), pltpu.VMEM((tb * tl, D_INNER), BF16)],
        input_output_aliases=aliases,
        compiler_params=pltpu.CompilerParams(dimension_semantics=semantics, vmem_limit_bytes=VMEM_LIMIT_BYTES),
        name=f"pool_layer{layer}_{'fresh' if fresh else 'cont'}",
    )(*args)


def _gla_kernel(*refs, tl, n_l, tb_s, tl_s, row0_p, layer, has_prev, final):
    it = iter(refs)
    xp_ref, xs_ref = next(it), next(it)
    modp_refs = [next(it) for _ in range(3)]
    mods_refs = [next(it) for _ in range(3)]
    g_ref, win_ref, wgl_ref, wup_ref, bg_ref, ng_ref, wout_hbm = (next(it) for _ in range(7))
    fg_ref = next(it) if final else None
    s0_ref = next(it)
    if has_prev:
        next(it), next(it)
    yp_ref, ys_ref, sp_ref, ss_ref, gated_ref, wout_ref, wout_sem = (next(it) for _ in range(7))

    wout_copy = pltpu.make_async_copy(wout_hbm.at[layer], wout_ref, wout_sem)
    wout_copy.start()

    l = pl.program_id(1)
    step = pl.program_id(0) * n_l + l
    m_p, m_s = tl, tb_s * tl_s

    @pl.when(l == 0)
    def _():
        sp_ref[...] = jnp.zeros(sp_ref.shape, F32)

    shift_p, scale_p, gate_p = _load_mods(modp_refs, 1, row0_p)
    first = (step * tb_s) % MOD_ROWS
    shift_s, scale_s, gate_s = [
        sum(jnp.where(first == r0, ref[r0:r0 + tb_s, :], 0.0) for r0 in range(0, MOD_ROWS, tb_s))[:, None, :]
        for ref in mods_refs]
    xp, xs = xp_ref[...], xs_ref[...]
    h = jnp.concatenate([
        _mod_norm(xp, g_ref[...], shift_p, scale_p).reshape(m_p, D_MODEL),
        _mod_norm(xs, g_ref[...], shift_s, scale_s).reshape(m_s, D_MODEL)], axis=0).astype(BF16)
    qk = jnp.dot(h, win_ref[:, :2 * GLA_DK], preferred_element_type=F32)
    gl = jnp.dot(h, wgl_ref[...], preferred_element_type=F32).astype(BF16)
    gate_pre = jnp.dot(gl, wup_ref[...], preferred_element_type=F32) + bg_ref[...]
    lg = (jnp.minimum(gate_pre, 0.0) - jnp.log(1.0 + jnp.exp(-jnp.abs(gate_pre)))) * (1.0 / GLA_GATE_NORM)

    def project_v():
        v = jnp.dot(h, win_ref[:, 2 * GLA_DK:2 * GLA_DK + GLA_DV], preferred_element_type=F32).astype(BF16)
        wout_copy.wait()
        return v

    def project_sz():
        return _silu(jnp.dot(h, win_ref[:, 2 * GLA_DK + GLA_DV:], preferred_element_type=F32)).astype(BF16)

    def finish_head(rows, hd, o_h, sz):
        o_h = o_h * lax.rsqrt(jnp.mean(o_h * o_h, axis=-1, keepdims=True) + EPS)
        hc = slice(hd * GLA_HV, (hd + 1) * GLA_HV)
        gated_ref[rows, hc] = o_h.astype(BF16) * ng_ref[:, hc].astype(BF16) * sz[rows, hc]

    v_all, sz = _gla_long_chunks(qk, lg, project_v, project_sz, sp_ref, finish_head, m=m_p, tl=tl)
    _gla_short_chunks(qk, lg, v_all, sz, s0_ref, ss_ref, finish_head, row0=m_p, m=m_s, tl=tl_s)

    y = jnp.dot(gated_ref[...], wout_ref[...], preferred_element_type=F32)
    out_p = xp + gate_p * y[:m_p].reshape(1, tl, D_MODEL)
    out_s = xs + gate_s * y[m_p:].reshape(tb_s, tl_s, D_MODEL)
    if final:
        out_p = out_p * lax.rsqrt(jnp.mean(out_p * out_p, axis=-1, keepdims=True) + EPS) * fg_ref[...]
        out_s = out_s * lax.rsqrt(jnp.mean(out_s * out_s, axis=-1, keepdims=True) + EPS) * fg_ref[...]
    yp_ref[...] = out_p
    ys_ref[...] = out_s


def _prefix_sums(pat, lg_rows):
    hi = lg_rows.astype(BF16)
    lo = (lg_rows - hi.astype(F32)).astype(BF16)
    return jnp.dot(jnp.concatenate([pat, pat], axis=1), jnp.concatenate([hi, lo], axis=0),
                   preferred_element_type=F32)


def _decay_cols(decay_row):
    d_col = jnp.transpose(jnp.broadcast_to(decay_row, (GLA_HK, GLA_HK)))
    return jnp.concatenate([d_col] * (GLA_HV // GLA_HK), axis=1)


def _gla_long_chunks(qk, lg, project_v, project_sz, s_ref, finish_head, *, m, tl):
    hb = GLA_CHUNK
    cr = 2 * hb
    ri = lax.broadcasted_iota(jnp.int32, (cr, cr), 0)
    ci = lax.broadcasted_iota(jnp.int32, (cr, cr), 1)
    causal = ci <= ri
    tri = jnp.where(causal, 1.0, 0.0).astype(BF16)
    zeros = jnp.zeros((hb, GLA_HK), BF16)
    heads = range(GLA_HEADS)
    kcs = [slice(hd * GLA_HK, (hd + 1) * GLA_HK) for hd in heads]

    def prepare(c):
        rows = slice(c * cr, (c + 1) * cr)
        bcum = _prefix_sums(tri, lg[rows])
        b_a, b_b = bcum[:hb], bcum[hb:]
        ref_a, bnd = bcum[hb // 2 - 1:hb // 2], bcum[hb - 1:hb]
        ref_b, tot = bcum[hb + hb // 2 - 1:hb + hb // 2], bcum[cr - 1:cr]
        q = qk[rows, 0:GLA_DK] * (GLA_HK ** -0.5)
        k = qk[rows, GLA_DK:2 * GLA_DK]
        q_a, q_b, k_a, k_b = q[:hb], q[hb:], k[:hb], k[hb:]
        qi_a = (q_a * jnp.exp(b_a - ref_a)).astype(BF16)
        ki_a = (k_a * jnp.exp(ref_a - b_a)).astype(BF16)
        qi_b = (q_b * jnp.exp(b_b - ref_b)).astype(BF16)
        ki_b = (k_b * jnp.exp(ref_b - b_b)).astype(BF16)
        qx_b = (q_b * jnp.exp(b_b - bnd)).astype(BF16)
        kx_a = (k_a * jnp.exp(bnd - b_a)).astype(BF16)
        q_dec = (q * jnp.exp(bcum)).astype(BF16)
        k_state = (k * jnp.exp(tot - bcum)).astype(BF16)
        decay = jnp.exp(tot)
        return qi_a, ki_a, qi_b, ki_b, qx_b, kx_a, q_dec, k_state, decay

    sz = project_sz()
    prepared = [prepare(c) for c in range(m // cr)]
    v_all = project_v()
    for c, (qi_a, ki_a, qi_b, ki_b, qx_b, kx_a, q_dec, k_state, decay) in enumerate(prepared):
        rows = slice(c * cr, (c + 1) * cr)
        si = (c * cr) // tl
        v = [v_all[rows, hd * GLA_HV:(hd + 1) * GLA_HV] for hd in heads]
        s_old = [s_ref[0, si, hd] for hd in heads]
        att_a = [lax.dot_general(qi_a[:, kc], jnp.concatenate([ki_a[:, kc], zeros], axis=0), _NT,
                                 preferred_element_type=F32) for kc in kcs]
        att_b = [lax.dot_general(
            jnp.concatenate([qx_b[:, kc], qi_b[:, kc]], axis=1),
            jnp.concatenate([jnp.concatenate([kx_a[:, kc], zeros], axis=1),
                             jnp.concatenate([zeros, ki_b[:, kc]], axis=1)], axis=0),
            _NT, preferred_element_type=F32) for kc in kcs]
        upd = [lax.dot_general(k_state[:, kc], v[hd], _TN, preferred_element_type=F32)
               for hd, kc in enumerate(kcs)]
        att = [jnp.where(causal, jnp.concatenate([att_a[hd], att_b[hd]], axis=0), 0.0).astype(BF16)
               for hd in heads]
        o = [jnp.dot(jnp.concatenate([att[hd], q_dec[:, kc]], axis=1),
                     jnp.concatenate([v[hd], s_old[hd].astype(BF16)], axis=0), preferred_element_type=F32)
             for hd, kc in enumerate(kcs)]
        for hd, kc in enumerate(kcs):
            s_ref[0, si, hd] = s_old[hd] * _decay_cols(decay[:, kc]) + upd[hd]
        for hd in heads:
            finish_head(rows, hd, o[hd], sz)
    return v_all, sz


def _gla_short_chunks(qk, lg, v_all, sz, s_in_ref, s_out_ref, finish_head, *, row0, m, tl):
    seg = tl
    chunk = min(GLA_CHUNK, m)
    n_seg = chunk // seg
    ri = lax.broadcasted_iota(jnp.int32, (chunk, chunk), 0)
    ci = lax.broadcasted_iota(jnp.int32, (chunk, chunk), 1)
    sh = seg.bit_length() - 1
    same = (ri >> sh) == (ci >> sh)
    causal = same & (ci <= ri)
    mid = same & ((ci & (seg - 1)) <= (seg // 2 - 1))
    pat = jnp.concatenate([jnp.where(mk, 1.0, 0.0).astype(BF16) for mk in (causal, mid, same)], axis=0)

    for c in range(m // chunk):
        r0 = c * chunk
        rows = slice(row0 + r0, row0 + r0 + chunk)
        sums = _prefix_sums(pat, lg[rows])
        bcum, bref, btot = sums[:chunk], sums[chunk:2 * chunk], sums[2 * chunk:]
        q = qk[rows, 0:GLA_DK] * (GLA_HK ** -0.5)
        k = qk[rows, GLA_DK:2 * GLA_DK]
        q_intra = (q * jnp.exp(bcum - bref)).astype(BF16)
        k_intra = (k * jnp.exp(bref - bcum)).astype(BF16)
        q_dec = (q * jnp.exp(bcum)).astype(BF16)
        k_state = (k * jnp.exp(btot - bcum)).astype(BF16)
        decay = jnp.exp(btot)
        for hd in range(GLA_HEADS):
            kc = slice(hd * GLA_HK, (hd + 1) * GLA_HK)
            v_h = v_all[rows, hd * GLA_HV:(hd + 1) * GLA_HV]
            att = lax.dot_general(q_intra[:, kc], k_intra[:, kc], _NT, preferred_element_type=F32)
            att = jnp.where(causal, att, 0.0).astype(BF16)
            o_h = jnp.dot(att, v_h, preferred_element_type=F32)
            o_inter = []
            for sg in range(n_seg):
                srows = slice(sg * seg, (sg + 1) * seg)
                si = (r0 + sg * seg) // tl
                s_old = s_in_ref[0, si, hd]
                o_inter.append(jnp.dot(q_dec[srows, kc], s_old.astype(BF16), preferred_element_type=F32))
                upd = lax.dot_general(k_state[srows, kc], v_h[srows], _TN, preferred_element_type=F32)
                s_out_ref[0, si, hd] = s_old * _decay_cols(decay[sg * seg:sg * seg + 1, kc]) + upd
            finish_head(rows, hd, o_h + jnp.concatenate(o_inter, axis=0), sz)


def _gla_layer(x_p, x_s, mods, li, row0_p, norm_g, gw, final_g, state_s, prev_p, prev_s, *, layer, n_layers,
               tl, tb_s):
    bp, L, _ = x_p.shape
    bs, tl_s, _ = x_s.shape
    n_l = L // tl
    assert bp * n_l * tb_s == bs and MOD_ROWS % tb_s == 0
    has_prev = prev_p is not None
    final = final_g is not None
    m = tl + tb_s * tl_s
    step = lambda bi, l: bi * n_l + l
    tile_p = lambda bi, l: (bi, l, 0)
    tile_s = lambda bi, l: (step(bi, l), 0, 0)
    st_p = pl.BlockSpec((1, 1, GLA_HEADS, GLA_HK, GLA_HV), lambda bi, l: (layer, bi, 0, 0, 0))
    st_s = pl.BlockSpec((1, tb_s, GLA_HEADS, GLA_HK, GLA_HV), lambda bi, l: (layer, step(bi, l), 0, 0, 0))
    in_specs = [
        pl.BlockSpec((1, tl, D_MODEL), tile_p),
        pl.BlockSpec((tb_s, tl_s, D_MODEL), tile_s),
        *_mod_specs(1, li, row0_p),
        *[pl.BlockSpec((None, MOD_ROWS, D_MODEL),
                       lambda bi, l, j=j: (li * 3 + j, (step(bi, l) * tb_s) // MOD_ROWS, 0)) for j in range(3)],
        _const_spec((1, D_MODEL), lead=(li,)),
        _const_spec((D_MODEL, GLA_QKVZ), lead=(layer,)),
        _const_spec((D_MODEL, GLA_GL_PAD), lead=(layer,)),
        _const_spec((GLA_GL_PAD, GLA_DK), lead=(layer,)),
        _const_spec((1, GLA_DK), lead=(layer,)),
        _const_spec((1, GLA_DV), lead=(layer,)),
        pl.BlockSpec(memory_space=pl.ANY),
    ]
    args = [x_p, x_s, mods, mods, mods, mods, mods, mods, norm_g, gw["w_in"], gw["w_gl"], gw["w_up"],
            gw["b_gate"], gw["norm_g"], gw["w_out"]]
    if final:
        in_specs.append(_const_spec((1, D_MODEL)))
        args.append(final_g)
    in_specs.append(st_s)
    args.append(state_s)
    aliases = {}
    if has_prev:
        in_specs += [pl.BlockSpec(memory_space=pl.ANY)] * 2
        aliases = {len(args): 2, len(args) + 1: 3}
        args += [prev_p, prev_s]
    kern = functools.partial(_gla_kernel, tl=tl, n_l=n_l, tb_s=tb_s, tl_s=tl_s, row0_p=row0_p, layer=layer,
                             has_prev=has_prev, final=final)
    return pl.pallas_call(
        kern,
        grid=(bp, n_l),
        in_specs=in_specs,
        out_specs=[pl.BlockSpec((1, tl, D_MODEL), tile_p), pl.BlockSpec((tb_s, tl_s, D_MODEL), tile_s), st_p, st_s],
        out_shape=[
            jax.ShapeDtypeStruct(x_p.shape, F32),
            jax.ShapeDtypeStruct(x_s.shape, F32),
            jax.ShapeDtypeStruct((n_layers, bp, GLA_HEADS, GLA_HK, GLA_HV), F32),
            jax.ShapeDtypeStruct((n_layers, bs, GLA_HEADS, GLA_HK, GLA_HV), F32),
        ],
        scratch_shapes=[
            pltpu.VMEM((m, GLA_DV), BF16),
            pltpu.VMEM((GLA_DV, D_MODEL), BF16),
            pltpu.SemaphoreType.DMA(()),
        ],
        input_output_aliases=aliases,
        compiler_params=pltpu.CompilerParams(
            dimension_semantics=("arbitrary", "arbitrary"), vmem_limit_bytes=VMEM_LIMIT_BYTES),
        name=f"gla_layer{layer}",
    )(*args)


def _run_trunk(x_p, x_s, mods, row0_p, pool_state_s, gla_state_s, pos0_s, w, *, tl_p, tb_pool_s, tb_gla_s):
    n_pool, n_gla = (DEPTH + 1) // 2, DEPTH // 2
    pool_p = pool_s = gla_p = gla_s = None
    for li in range(DEPTH):
        j = li // 2
        if li % 2 == 0:
            pool_w = (w["norm_g"], w["pool_w_in"], w["pool_w_grp"], w["pool_scale"], w["pool_w_out"])
            x_p, pool_p = _pool_layer(x_p, mods, li, row0_p, *pool_w, None, pool_p,
                                      layer=j, n_layers=n_pool, tb=1, tl=tl_p, pos0=0)
            x_s, pool_s = _pool_layer(x_s, mods, li, 0, *pool_w, pool_state_s, pool_s,
                                      layer=j, n_layers=n_pool, tb=tb_pool_s, tl=x_s.shape[1], pos0=pos0_s)
        else:
            x_p, x_s, gla_p, gla_s = _gla_layer(
                x_p, x_s, mods, li, row0_p, w["norm_g"], w["gla"], w["final_g"] if li == DEPTH - 1 else None,
                gla_state_s, gla_p, gla_s, layer=j, n_layers=n_gla, tl=tl_p, tb_s=tb_gla_s)
    return x_p, x_s, pool_p, gla_p, pool_s, gla_s


def kernel(x_prompt, x_sample, c_prompt, c_sample, state_pool, state_gla, ada_w, ada_b, norm_g, pool_w_in,
           pool_w_grp, pool_scale, pool_w_out, gla_w_in, gla_w_gate_up, gla_b_gate, gla_norm_g, gla_w_out,
           final_g):
    n_sample = x_sample.shape[0]
    n_gla = gla_w_in.shape[0]
    c_all = jnp.concatenate([c_sample, c_prompt], axis=0)
    mods = _ada_mods(c_all, ada_w, ada_b).reshape(DEPTH * 3, c_all.shape[0], D_MODEL)

    assert gla_w_in.shape[-1] == GLA_QKVZ + GLA_RANK
    w_gl_pad = jnp.concatenate(
        [gla_w_in[:, :, GLA_QKVZ:], jnp.zeros((n_gla, D_MODEL, GLA_GL_PAD - GLA_RANK), gla_w_in.dtype)], axis=-1)
    w_up_pad = jnp.concatenate(
        [gla_w_gate_up, jnp.zeros((n_gla, GLA_GL_PAD - GLA_RANK, GLA_DK), gla_w_gate_up.dtype)], axis=1)
    w = {
        "norm_g": norm_g.reshape(DEPTH, 1, D_MODEL),
        "pool_w_in": pool_w_in.astype(BF16),
        "pool_w_grp": pool_w_grp.astype(BF16),
        "pool_scale": pool_scale.reshape(-1, 1, D_INNER),
        "pool_w_out": pool_w_out.astype(BF16),
        "gla": {
            "w_in": gla_w_in.astype(BF16),
            "w_gl": w_gl_pad.astype(BF16),
            "w_up": w_up_pad.astype(BF16),
            "b_gate": gla_b_gate.reshape(-1, 1, GLA_DK),
            "norm_g": gla_norm_g.reshape(-1, 1, GLA_DV),
            "w_out": gla_w_out.astype(BF16),
        },
        "final_g": final_g.reshape(1, D_MODEL),
    }
    y_p, y_s, pool_p, gla_p, pool_s, gla_s = _run_trunk(
        x_prompt, x_sample, mods, n_sample, state_pool.transpose(0, 2, 1, 3), state_gla, PAST_LEN, w,
        tl_p=512, tb_pool_s=32, tb_gla_s=4)
    return (y_p, y_s, pool_p, gla_p, pool_s.transpose(0, 2, 1, 3), gla_s)
```

```python
import functools

import jax
import jax.numpy as jnp
from jax import lax
from jax.experimental import pallas as pl
from jax.experimental.pallas import tpu as pltpu

D_MODEL = 1024
DEPTH = 4
PAST_LEN = 16384
D_INNER = 2048
POOL_WINDOWS = (2, 4, 8, 16)
POOL_GROUP = D_INNER // len(POOL_WINDOWS)
POOL_BUF = max(POOL_WINDOWS) - 1
POOL_HALO = POOL_BUF + 1
GLA_HEADS = 4
GLA_DK = 512
GLA_DV = D_INNER
GLA_HK = GLA_DK // GLA_HEADS
GLA_HV = GLA_DV // GLA_HEADS
GLA_RANK = 16
GLA_GATE_NORM = 16.0
GLA_CHUNK = 64
GLA_GL_PAD = 128
GLA_QKVZ = 2 * GLA_DK + 2 * GLA_DV
MOD_ROWS = 8
EPS = 1e-6
F32 = jnp.float32
BF16 = jnp.bfloat16

VMEM_LIMIT_BYTES = 62 * 1024 * 1024

_NT = (((1,), (1,)), ((), ()))
_TN = (((0,), (0,)), ((), ()))


def _silu(v):
    return v / (1.0 + jnp.exp(-v))


def _mod_norm(x, g, shift, scale):
    ms = jnp.mean(x * x, axis=-1, keepdims=True)
    return x * lax.rsqrt(ms + EPS) * (g * (1.0 + scale)) + shift


def _const_spec(shape, lead=(), tail=None):
    idx = tuple(lead) + tuple(tail if tail is not None else (0,) * len(shape))
    return pl.BlockSpec((None,) * len(lead) + tuple(shape), lambda *_: idx, pipeline_mode=pl.Buffered(1))


def _mod_specs(tb, li, row0):
    rows = max(tb, MOD_ROWS)
    assert row0 % rows == 0 and rows % tb == 0
    return [pl.BlockSpec((None, rows, D_MODEL), lambda bi, *_, j=j: (li * 3 + j, (row0 + bi * tb) // rows, 0))
            for j in range(3)]


def _load_mods(refs, tb, row0):
    if tb >= MOD_ROWS:
        return [ref[...][:, None, :] for ref in refs]
    assert tb == 1
    r = (row0 + pl.program_id(0)) % MOD_ROWS
    return [ref[pl.ds(r, 1), :] for ref in refs]


def _ada_kernel(c_ref, w_ref, b_ref, o_ref):
    cs = _silu(c_ref[...]).astype(BF16)
    for j in range(3):
        cols = slice(j * D_MODEL, (j + 1) * D_MODEL)
        o_ref[0, j] = jnp.dot(cs, w_ref[0, :, cols].astype(BF16), preferred_element_type=F32) + b_ref[0, :, cols]


def _ada_mods(c_all, ada_w, ada_b):
    nb = c_all.shape[0]
    return pl.pallas_call(
        _ada_kernel,
        grid=(DEPTH,),
        in_specs=[
            pl.BlockSpec((nb, D_MODEL), lambda li: (0, 0)),
            pl.BlockSpec((1, D_MODEL, 3 * D_MODEL), lambda li: (li, 0, 0)),
            pl.BlockSpec((1, 1, 3 * D_MODEL), lambda li: (li, 0, 0)),
        ],
        out_specs=pl.BlockSpec((1, 3, nb, D_MODEL), lambda li: (li, 0, 0, 0)),
        out_shape=jax.ShapeDtypeStruct((DEPTH, 3, nb, D_MODEL), F32),
        compiler_params=pltpu.CompilerParams(vmem_limit_bytes=VMEM_LIMIT_BYTES),
        name="ada_mods",
    )(c_all, ada_w, ada_b.reshape(DEPTH, 1, 3 * D_MODEL))


def _pool_mix(ext_ref, uz, wgrp_ref, pscale_ref, gated_ref, *, halo, time_axis, rows, pos):
    m = uz.shape[0]

    def rows_of(a, lo, n):
        idx = [slice(None)] * 3
        idx[time_axis] = slice(lo, lo + n)
        return a[tuple(idx)]

    def window_sum(c0, w):
        idx = [slice(None)] * 3
        idx[2] = slice(c0, c0 + POOL_GROUP)
        acc = ext_ref[tuple(idx)]
        first, k = 0, 1
        while k < w:
            n = acc.shape[time_axis] - k
            acc = rows_of(acc, k, n) + rows_of(acc, 0, n)
            first, k = first + k, 2 * k
        return rows_of(acc, halo - first, rows)

    diffs = []
    for gi, w in enumerate(POOL_WINDOWS):
        c0 = gi * POOL_GROUP
        u_idx = [slice(None)] * 3
        u_idx[time_axis] = slice(halo, halo + rows)
        u_idx[2] = slice(c0, c0 + POOL_GROUP)
        u_g = ext_ref[tuple(u_idx)]
        s = window_sum(c0, w)
        mean = s * (1.0 / w) if pos is None else s / jnp.minimum(pos + 1, w).astype(F32)
        diffs.append((mean - u_g).reshape(m, POOL_GROUP).astype(BF16))
    mixed = [jnp.dot(diffs[gi], wgrp_ref[gi], preferred_element_type=F32) for gi in range(len(POOL_WINDOWS))]
    for gi in range(len(POOL_WINDOWS)):
        c0 = gi * POOL_GROUP
        z_g = uz[:, D_INNER + c0:D_INNER + c0 + POOL_GROUP]
        gated_ref[:, c0:c0 + POOL_GROUP] = (
            mixed[gi] * pscale_ref[:, c0:c0 + POOL_GROUP] * _silu(z_g)).astype(BF16)


def _pool_kernel(*refs, tl, n_l, row0, has_prev):
    it = iter(refs)
    x_ref = next(it)
    mod_refs = [next(it) for _ in range(3)]
    g_ref, win_ref, wgrp_ref, pscale_ref, wout_ref = (next(it) for _ in range(5))
    if has_prev:
        next(it)
    y_ref, nbuf_ref, ext_ref, gated_ref = (next(it) for _ in range(4))

    l = pl.program_id(1)

    @pl.when(l == 0)
    def _():
        ext_ref[:, 0:POOL_HALO, :] = jnp.zeros((1, POOL_HALO, D_INNER), F32)

    shift, scale, gate = _load_mods(mod_refs, 1, row0)
    x = x_ref[...]
    h = _mod_norm(x, g_ref[...], shift, scale).reshape(tl, D_MODEL).astype(BF16)
    uz = jnp.dot(h, win_ref[...], preferred_element_type=F32)
    ext_ref[:, POOL_HALO:POOL_HALO + tl, :] = uz[:, :D_INNER].reshape(1, tl, D_INNER)
    pos = l * tl + lax.broadcasted_iota(jnp.int32, (1, tl, POOL_GROUP), 1)
    _pool_mix(ext_ref, uz, wgrp_ref, pscale_ref, gated_ref, halo=POOL_HALO, time_axis=1, rows=tl, pos=pos)
    y = jnp.dot(gated_ref[...], wout_ref[...], preferred_element_type=F32)
    y_ref[...] = x + gate * y.reshape(1, tl, D_MODEL)

    @pl.when(l == n_l - 1)
    def _():
        nbuf_ref[0] = ext_ref[:, tl + 1:tl + POOL_HALO, :]

    if n_l > 1:
        @pl.when(l < n_l - 1)
        def _():
            ext_ref[:, 0:POOL_HALO, :] = ext_ref[:, tl:tl + POOL_HALO, :]


def _pool_cont_kernel(*refs, tb, tl, pos0, has_prev):
    it = iter(refs)
    x_ref = next(it)
    mod_refs = [next(it) for _ in range(3)]
    g_ref, win_ref, wgrp_ref, pscale_ref, wout_ref, buf_ref = (next(it) for _ in range(6))
    if has_prev:
        next(it)
    y_ref, nbuf_ref, ext_ref, gated_ref = (next(it) for _ in range(4))
    m = tb * tl

    shift, scale, gate = _load_mods(mod_refs, tb, 0)
    x = x_ref[...]
    h = jnp.transpose(_mod_norm(x, g_ref[...], shift, scale), (1, 0, 2))
    uz = jnp.dot(h.reshape(m, D_MODEL).astype(BF16), win_ref[...], preferred_element_type=F32)
    ext_ref[0:POOL_BUF] = buf_ref[...]
    ext_ref[POOL_BUF:POOL_BUF + tl] = uz[:, :D_INNER].reshape(tl, tb, D_INNER)
    pos = None
    if pos0 < POOL_BUF:
        pos = pos0 + lax.broadcasted_iota(jnp.int32, (tl, tb, POOL_GROUP), 0)
    _pool_mix(ext_ref, uz, wgrp_ref, pscale_ref, gated_ref, halo=POOL_BUF, time_axis=0, rows=tl, pos=pos)
    y = jnp.dot(gated_ref[...], wout_ref[...], preferred_element_type=F32)
    y_ref[...] = x + gate * jnp.transpose(y.reshape(tl, tb, D_MODEL), (1, 0, 2))
    nbuf_ref[...] = ext_ref[tl:tl + POOL_BUF]


def _pool_layer(x, mods, li, row0, norm_g, w_in, w_grp, pscale, w_out, state, prev_out, *, layer, n_layers,
                tb, tl, pos0):
    b, L, _ = x.shape
    fresh = state is None
    has_prev = prev_out is not None
    in_specs = [
        None,
        *_mod_specs(tb, li, row0),
        _const_spec((1, D_MODEL), lead=(li,)),
        _const_spec((D_MODEL, 2 * D_INNER), lead=(layer,)),
        _const_spec((len(POOL_WINDOWS), POOL_GROUP, POOL_GROUP), lead=(layer,)),
        _const_spec((1, D_INNER), lead=(layer,)),
        _const_spec((D_INNER, D_MODEL), lead=(layer,)),
    ]
    args = [x, mods, mods, mods, norm_g, w_in, w_grp, pscale, w_out]
    if fresh:
        assert tb == 1 and pos0 == 0
        n_l = L // tl
        grid = (b, n_l)
        x_spec = pl.BlockSpec((1, tl, D_MODEL), lambda bi, l: (bi, l, 0))
        buf_shape = (n_layers, b, POOL_BUF, D_INNER)
        buf_spec = pl.BlockSpec((1, 1, POOL_BUF, D_INNER), lambda bi, l: (layer, bi, 0, 0))
        ext_shape = (1, POOL_HALO + tl, D_INNER)
        kern = functools.partial(_pool_kernel, tl=tl, n_l=n_l, row0=row0, has_prev=has_prev)
        semantics = ("arbitrary", "arbitrary")
    else:
        assert tl == L and row0 == 0
        grid = (b // tb,)
        x_spec = pl.BlockSpec((tb, tl, D_MODEL), lambda bi: (bi, 0, 0))
        buf_shape = (n_layers, POOL_BUF, b, D_INNER)
        buf_spec = pl.BlockSpec((None, POOL_BUF, tb, D_INNER), lambda bi: (layer, 0, bi, 0))
        ext_shape = (POOL_BUF + tl, tb, D_INNER)
        in_specs.append(buf_spec)
        args.append(state)
        kern = functools.partial(_pool_cont_kernel, tb=tb, tl=tl, pos0=pos0, has_prev=has_prev)
        semantics = ("arbitrary",)
    in_specs[0] = x_spec
    aliases = {}
    if has_prev:
        in_specs.append(pl.BlockSpec(memory_space=pl.ANY))
        aliases = {len(args): 1}
        args.append(prev_out)
    return pl.pallas_call(
        kern,
        grid=grid,
        in_specs=in_specs,
        out_specs=[x_spec, buf_spec],
        out_shape=[jax.ShapeDtypeStruct(x.shape, F32), jax.ShapeDtypeStruct(buf_shape, F32)],
        scratch_shapes=[pltpu.VMEM(ext_shape, F32), pltpu.VMEM((tb * tl, D_INNER), BF16)],
        input_output_aliases=aliases,
        compiler_params=pltpu.CompilerParams(dimension_semantics=semantics, vmem_limit_bytes=VMEM_LIMIT_BYTES),
        name=f"pool_layer{layer}_{'fresh' if fresh else 'cont'}",
    )(*args)


def _gla_kernel(*refs, tl, n_l, tb_s, tl_s, row0_p, has_prev, final):
    it = iter(refs)
    xp_ref, xs_ref = next(it), next(it)
    modp_refs = [next(it) for _ in range(3)]
    mods_refs = [next(it) for _ in range(3)]
    g_ref, win_ref, wgl_ref, wup_ref, bg_ref, ng_ref, wout_ref = (next(it) for _ in range(7))
    fg_ref = next(it) if final else None
    s0_ref = next(it)
    if has_prev:
        next(it), next(it)
    yp_ref, ys_ref, sp_ref, ss_ref, gated_ref = (next(it) for _ in range(5))

    l = pl.program_id(1)
    step = pl.program_id(0) * n_l + l
    m_p, m_s = tl, tb_s * tl_s

    @pl.when(l == 0)
    def _():
        sp_ref[...] = jnp.zeros(sp_ref.shape, F32)

    shift_p, scale_p, gate_p = _load_mods(modp_refs, 1, row0_p)
    first = (step * tb_s) % MOD_ROWS
    shift_s, scale_s, gate_s = [
        sum(jnp.where(first == r0, ref[r0:r0 + tb_s, :], 0.0) for r0 in range(0, MOD_ROWS, tb_s))[:, None, :]
        for ref in mods_refs]
    xp, xs = xp_ref[...], xs_ref[...]
    h = jnp.concatenate([
        _mod_norm(xp, g_ref[...], shift_p, scale_p).reshape(m_p, D_MODEL),
        _mod_norm(xs, g_ref[...], shift_s, scale_s).reshape(m_s, D_MODEL)], axis=0).astype(BF16)
    qk = jnp.dot(h, win_ref[:, :2 * GLA_DK], preferred_element_type=F32)
    gl = jnp.dot(h, wgl_ref[...], preferred_element_type=F32).astype(BF16)
    gate_pre = jnp.dot(gl, wup_ref[...], preferred_element_type=F32) + bg_ref[...]
    lg = (jnp.minimum(gate_pre, 0.0) - jnp.log(1.0 + jnp.exp(-jnp.abs(gate_pre)))) * (1.0 / GLA_GATE_NORM)

    def project_v():
        return jnp.dot(h, win_ref[:, 2 * GLA_DK:2 * GLA_DK + GLA_DV], preferred_element_type=F32).astype(BF16)

    def project_sz():
        return _silu(jnp.dot(h, win_ref[:, 2 * GLA_DK + GLA_DV:], preferred_element_type=F32)).astype(BF16)

    def finish_head(rows, hd, o_h, sz):
        o_h = o_h * lax.rsqrt(jnp.mean(o_h * o_h, axis=-1, keepdims=True) + EPS)
        hc = slice(hd * GLA_HV, (hd + 1) * GLA_HV)
        gated_ref[rows, hc] = o_h.astype(BF16) * ng_ref[:, hc].astype(BF16) * sz[rows, hc]

    v_all, sz = _gla_long_chunks(qk, lg, project_v, project_sz, sp_ref, finish_head, m=m_p, tl=tl)
    _gla_short_chunks(qk, lg, v_all, sz, s0_ref, ss_ref, finish_head, row0=m_p, m=m_s, tl=tl_s)

    y = jnp.dot(gated_ref[...], wout_ref[...], preferred_element_type=F32)
    out_p = xp + gate_p * y[:m_p].reshape(1, tl, D_MODEL)
    out_s = xs + gate_s * y[m_p:].reshape(tb_s, tl_s, D_MODEL)
    if final:
        out_p = out_p * lax.rsqrt(jnp.mean(out_p * out_p, axis=-1, keepdims=True) + EPS) * fg_ref[...]
        out_s = out_s * lax.rsqrt(jnp.mean(out_s * out_s, axis=-1, keepdims=True) + EPS) * fg_ref[...]
    yp_ref[...] = out_p
    ys_ref[...] = out_s


def _prefix_sums(pat, lg_rows):
    hi = lg_rows.astype(BF16)
    lo = (lg_rows - hi.astype(F32)).astype(BF16)
    return jnp.dot(jnp.concatenate([pat, pat], axis=1), jnp.concatenate([hi, lo], axis=0),
                   preferred_element_type=F32)


def _decay_cols(decay_row):
    d_col = jnp.transpose(jnp.broadcast_to(decay_row, (GLA_HK, GLA_HK)))
    return jnp.concatenate([d_col] * (GLA_HV // GLA_HK), axis=1)


def _gla_long_chunks(qk, lg, project_v, project_sz, s_ref, finish_head, *, m, tl):
    hb = GLA_CHUNK
    cr = 2 * hb
    ri = lax.broadcasted_iota(jnp.int32, (cr, cr), 0)
    ci = lax.broadcasted_iota(jnp.int32, (cr, cr), 1)
    causal = ci <= ri
    tri = jnp.where(causal, 1.0, 0.0).astype(BF16)
    zeros = jnp.zeros((hb, GLA_HK), BF16)
    heads = range(GLA_HEADS)
    kcs = [slice(hd * GLA_HK, (hd + 1) * GLA_HK) for hd in heads]

    def prepare(c):
        rows = slice(c * cr, (c + 1) * cr)
        bcum = _prefix_sums(tri, lg[rows])
        b_a, b_b = bcum[:hb], bcum[hb:]
        ref_a, bnd = bcum[hb // 2 - 1:hb // 2], bcum[hb - 1:hb]
        ref_b, tot = bcum[hb + hb // 2 - 1:hb + hb // 2], bcum[cr - 1:cr]
        q = qk[rows, 0:GLA_DK] * (GLA_HK ** -0.5)
        k = qk[rows, GLA_DK:2 * GLA_DK]
        q_a, q_b, k_a, k_b = q[:hb], q[hb:], k[:hb], k[hb:]
        qi_a = (q_a * jnp.exp(b_a - ref_a)).astype(BF16)
        ki_a = (k_a * jnp.exp(ref_a - b_a)).astype(BF16)
        qi_b = (q_b * jnp.exp(b_b - ref_b)).astype(BF16)
        ki_b = (k_b * jnp.exp(ref_b - b_b)).astype(BF16)
        qx_b = (q_b * jnp.exp(b_b - bnd)).astype(BF16)
        kx_a = (k_a * jnp.exp(bnd - b_a)).astype(BF16)
        q_dec = (q * jnp.exp(bcum)).astype(BF16)
        k_state = (k * jnp.exp(tot - bcum)).astype(BF16)
        decay = jnp.exp(tot)
        return qi_a, ki_a, qi_b, ki_b, qx_b, kx_a, q_dec, k_state, decay

    sz = project_sz()
    prepared = [prepare(c) for c in range(m // cr)]
    v_all = project_v()
    for c, (qi_a, ki_a, qi_b, ki_b, qx_b, kx_a, q_dec, k_state, decay) in enumerate(prepared):
        rows = slice(c * cr, (c + 1) * cr)
        si = (c * cr) // tl
        v = [v_all[rows, hd * GLA_HV:(hd + 1) * GLA_HV] for hd in heads]
        s_old = [s_ref[0, si, hd] for hd in heads]
        att_a = [lax.dot_general(qi_a[:, kc], jnp.concatenate([ki_a[:, kc], zeros], axis=0), _NT,
                                 preferred_element_type=F32) for kc in kcs]
        att_b = [lax.dot_general(
            jnp.concatenate([qx_b[:, kc], qi_b[:, kc]], axis=1),
            jnp.concatenate([jnp.concatenate([kx_a[:, kc], zeros], axis=1),
                             jnp.concatenate([zeros, ki_b[:, kc]], axis=1)], axis=0),
            _NT, preferred_element_type=F32) for kc in kcs]
        upd = [lax.dot_general(k_state[:, kc], v[hd], _TN, preferred_element_type=F32)
               for hd, kc in enumerate(kcs)]
        att = [jnp.where(causal, jnp.concatenate([att_a[hd], att_b[hd]], axis=0), 0.0).astype(BF16)
               for hd in heads]
        o = [jnp.dot(jnp.concatenate([att[hd], q_dec[:, kc]], axis=1),
                     jnp.concatenate([v[hd], s_old[hd].astype(BF16)], axis=0), preferred_element_type=F32)
             for hd, kc in enumerate(kcs)]
        for hd, kc in enumerate(kcs):
            s_ref[0, si, hd] = s_old[hd] * _decay_cols(decay[:, kc]) + upd[hd]
        for hd in heads:
            finish_head(rows, hd, o[hd], sz)
    return v_all, sz


def _gla_short_chunks(qk, lg, v_all, sz, s_in_ref, s_out_ref, finish_head, *, row0, m, tl):
    seg = tl
    chunk = min(GLA_CHUNK, m)
    n_seg = chunk // seg
    ri = lax.broadcasted_iota(jnp.int32, (chunk, chunk), 0)
    ci = lax.broadcasted_iota(jnp.int32, (chunk, chunk), 1)
    sh = seg.bit_length() - 1
    same = (ri >> sh) == (ci >> sh)
    causal = same & (ci <= ri)
    mid = same & ((ci & (seg - 1)) <= (seg // 2 - 1))
    pat = jnp.concatenate([jnp.where(mk, 1.0, 0.0).astype(BF16) for mk in (causal, mid, same)], axis=0)

    for c in range(m // chunk):
        r0 = c * chunk
        rows = slice(row0 + r0, row0 + r0 + chunk)
        sums = _prefix_sums(pat, lg[rows])
        bcum, bref, btot = sums[:chunk], sums[chunk:2 * chunk], sums[2 * chunk:]
        q = qk[rows, 0:GLA_DK] * (GLA_HK ** -0.5)
        k = qk[rows, GLA_DK:2 * GLA_DK]
        q_intra = (q * jnp.exp(bcum - bref)).astype(BF16)
        k_intra = (k * jnp.exp(bref - bcum)).astype(BF16)
        q_dec = (q * jnp.exp(bcum)).astype(BF16)
        k_state = (k * jnp.exp(btot - bcum)).astype(BF16)
        decay = jnp.exp(btot)
        for hd in range(GLA_HEADS):
            kc = slice(hd * GLA_HK, (hd + 1) * GLA_HK)
            v_h = v_all[rows, hd * GLA_HV:(hd + 1) * GLA_HV]
            att = lax.dot_general(q_intra[:, kc], k_intra[:, kc], _NT, preferred_element_type=F32)
            att = jnp.where(causal, att, 0.0).astype(BF16)
            o_h = jnp.dot(att, v_h, preferred_element_type=F32)
            o_inter = []
            for sg in range(n_seg):
                srows = slice(sg * seg, (sg + 1) * seg)
                si = (r0 + sg * seg) // tl
                s_old = s_in_ref[0, si, hd]
                o_inter.append(jnp.dot(q_dec[srows, kc], s_old.astype(BF16), preferred_element_type=F32))
                upd = lax.dot_general(k_state[srows, kc], v_h[srows], _TN, preferred_element_type=F32)
                s_out_ref[0, si, hd] = s_old * _decay_cols(decay[sg * seg:sg * seg + 1, kc]) + upd
            finish_head(rows, hd, o_h + jnp.concatenate(o_inter, axis=0), sz)


def _gla_layer(x_p, x_s, mods, li, row0_p, norm_g, gw, final_g, state_s, prev_p, prev_s, *, layer, n_layers,
               tl, tb_s):
    bp, L, _ = x_p.shape
    bs, tl_s, _ = x_s.shape
    n_l = L // tl
    assert bp * n_l * tb_s == bs and MOD_ROWS % tb_s == 0
    has_prev = prev_p is not None
    final = final_g is not None
    m = tl + tb_s * tl_s
    step = lambda bi, l: bi * n_l + l
    tile_p = lambda bi, l: (bi, l, 0)
    tile_s = lambda bi, l: (step(bi, l), 0, 0)
    st_p = pl.BlockSpec((1, 1, GLA_HEADS, GLA_HK, GLA_HV), lambda bi, l: (layer, bi, 0, 0, 0))
    st_s = pl.BlockSpec((1, tb_s, GLA_HEADS, GLA_HK, GLA_HV), lambda bi, l: (layer, step(bi, l), 0, 0, 0))
    in_specs = [
        pl.BlockSpec((1, tl, D_MODEL), tile_p),
        pl.BlockSpec((tb_s, tl_s, D_MODEL), tile_s),
        *_mod_specs(1, li, row0_p),
        *[pl.BlockSpec((None, MOD_ROWS, D_MODEL),
                       lambda bi, l, j=j: (li * 3 + j, (step(bi, l) * tb_s) // MOD_ROWS, 0)) for j in range(3)],
        _const_spec((1, D_MODEL), lead=(li,)),
        _const_spec((D_MODEL, GLA_QKVZ), lead=(layer,)),
        _const_spec((D_MODEL, GLA_GL_PAD), lead=(layer,)),
        _const_spec((GLA_GL_PAD, GLA_DK), lead=(layer,)),
        _const_spec((1, GLA_DK), lead=(layer,)),
        _const_spec((1, GLA_DV), lead=(layer,)),
        _const_spec((GLA_DV, D_MODEL), lead=(layer,)),
    ]
    args = [x_p, x_s, mods, mods, mods, mods, mods, mods, norm_g, gw["w_in"], gw["w_gl"], gw["w_up"],
            gw["b_gate"], gw["norm_g"], gw["w_out"]]
    if final:
        in_specs.append(_const_spec((1, D_MODEL)))
        args.append(final_g)
    in_specs.append(st_s)
    args.append(state_s)
    aliases = {}
    if has_prev:
        in_specs += [pl.BlockSpec(memory_space=pl.ANY)] * 2
        aliases = {len(args): 2, len(args) + 1: 3}
        args += [prev_p, prev_s]
    kern = functools.partial(_gla_kernel, tl=tl, n_l=n_l, tb_s=tb_s, tl_s=tl_s, row0_p=row0_p, has_prev=has_prev,
                             final=final)
    return pl.pallas_call(
        kern,
        grid=(bp, n_l),
        in_specs=in_specs,
        out_specs=[pl.BlockSpec((1, tl, D_MODEL), tile_p), pl.BlockSpec((tb_s, tl_s, D_MODEL), tile_s), st_p, st_s],
        out_shape=[
            jax.ShapeDtypeStruct(x_p.shape, F32),
            jax.ShapeDtypeStruct(x_s.shape, F32),
            jax.ShapeDtypeStruct((n_layers, bp, GLA_HEADS, GLA_HK, GLA_HV), F32),
            jax.ShapeDtypeStruct((n_layers, bs, GLA_HEADS, GLA_HK, GLA_HV), F32),
        ],
        scratch_shapes=[pltpu.VMEM((m, GLA_DV), BF16)],
        input_output_aliases=aliases,
        compiler_params=pltpu.CompilerParams(
            dimension_semantics=("arbitrary", "arbitrary"), vmem_limit_bytes=VMEM_LIMIT_BYTES),
        name=f"gla_layer{layer}",
    )(*args)


def _run_trunk(x_p, x_s, mods, row0_p, pool_state_s, gla_state_s, pos0_s, w, *, tl_p, tb_pool_s, tb_gla_s):
    n_pool, n_gla = (DEPTH + 1) // 2, DEPTH // 2
    pool_p = pool_s = gla_p = gla_s = None
    for li in range(DEPTH):
        j = li // 2
        if li % 2 == 0:
            pool_w = (w["norm_g"], w["pool_w_in"], w["pool_w_grp"], w["pool_scale"], w["pool_w_out"])
            x_p, pool_p = _pool_layer(x_p, mods, li, row0_p, *pool_w, None, pool_p,
                                      layer=j, n_layers=n_pool, tb=1, tl=tl_p, pos0=0)
            x_s, pool_s = _pool_layer(x_s, mods, li, 0, *pool_w, pool_state_s, pool_s,
                                      layer=j, n_layers=n_pool, tb=tb_pool_s, tl=x_s.shape[1], pos0=pos0_s)
        else:
            x_p, x_s, gla_p, gla_s = _gla_layer(
                x_p, x_s, mods, li, row0_p, w["norm_g"], w["gla"], w["final_g"] if li == DEPTH - 1 else None,
                gla_state_s, gla_p, gla_s, layer=j, n_layers=n_gla, tl=tl_p, tb_s=tb_gla_s)
    return x_p, x_s, pool_p, gla_p, pool_s, gla_s


def kernel(x_prompt, x_sample, c_prompt, c_sample, state_pool, state_gla, ada_w, ada_b, norm_g, pool_w_in,
           pool_w_grp, pool_scale, pool_w_out, gla_w_in, gla_w_gate_up, gla_b_gate, gla_norm_g, gla_w_out,
           final_g):
    n_sample = x_sample.shape[0]
    n_gla = gla_w_in.shape[0]
    c_all = jnp.concatenate([c_sample, c_prompt], axis=0)
    mods = _ada_mods(c_all, ada_w, ada_b).reshape(DEPTH * 3, c_all.shape[0], D_MODEL)

    assert gla_w_in.shape[-1] == GLA_QKVZ + GLA_RANK
    w_gl_pad = jnp.concatenate(
        [gla_w_in[:, :, GLA_QKVZ:], jnp.zeros((n_gla, D_MODEL, GLA_GL_PAD - GLA_RANK), gla_w_in.dtype)], axis=-1)
    w_up_pad = jnp.concatenate(
        [gla_w_gate_up, jnp.zeros((n_gla, GLA_GL_PAD - GLA_RANK, GLA_DK), gla_w_gate_up.dtype)], axis=1)
    w = {
        "norm_g": norm_g.reshape(DEPTH, 1, D_MODEL),
        "pool_w_in": pool_w_in.astype(BF16),
        "pool_w_grp": pool_w_grp.astype(BF16),
        "pool_scale": pool_scale.reshape(-1, 1, D_INNER),
        "pool_w_out": pool_w_out.astype(BF16),
        "gla": {
            "w_in": gla_w_in.astype(BF16),
            "w_gl": w_gl_pad.astype(BF16),
            "w_up": w_up_pad.astype(BF16),
            "b_gate": gla_b_gate.reshape(-1, 1, GLA_DK),
            "norm_g": gla_norm_g.reshape(-1, 1, GLA_DV),
            "w_out": gla_w_out.astype(BF16),
        },
        "final_g": final_g.reshape(1, D_MODEL),
    }
    y_p, y_s, pool_p, gla_p, pool_s, gla_s = _run_trunk(
        x_prompt, x_sample, mods, n_sample, state_pool.transpose(0, 2, 1, 3), state_gla, PAST_LEN, w,
        tl_p=512, tb_pool_s=32, tb_gla_s=4)
    return (y_p, y_s, pool_p, gla_p, pool_s.transpose(0, 2, 1, 3), gla_s)
```
